```python
import jax, jax.numpy as jnp
from jax import lax
import numpy as np

D_MODEL = 1024
BATCH = 2
SEQ = 8192
DEPTH = 2

A_HEADS = 8
A_HEAD_DIM = 64
A_PATTERNS = ((128, 1), (512, 4), (2048, 16))
A_WIDTH = A_HEADS * A_HEAD_DIM
B_GROUPS = 4
B_GROUP_DIM = 128
B_CHUNK = 128
B_WIDTH = B_GROUPS * B_GROUP_DIM
C_HEADS = 16
C_HEAD_DIM = 64
C_WIDTH = C_HEADS * C_HEAD_DIM
C_BLOCK = 256
C_TOPK = 3
C_QCHUNK = 32
D_FF = 2816
EPS = 1e-6

AB_IN_WIDTH = 3 * A_WIDTH + 2 * B_WIDTH
AB_MIX_WIDTH = A_WIDTH + B_WIDTH
N_EVEN = (DEPTH + 1) // 2
N_ODD = DEPTH // 2

kernel_name = 'hybrid_dilated_gmlp_moba_macaron'


def rms_norm(x, g):
    xf = x.astype(jnp.float32)
    y = xf * lax.rsqrt(jnp.mean(xf * xf, axis=-1, keepdims=True) + EPS)
    return (y * g.astype(jnp.float32)).astype(x.dtype)


def swiglu(x, w_gate, w_up, w_down):
    return (jax.nn.silu(x @ w_gate) * (x @ w_up)) @ w_down


def to_heads(t, n_heads, head_dim):
    b, s, _ = t.shape
    return t.reshape(b, s, n_heads, head_dim).transpose(0, 2, 1, 3)


def from_heads(t):
    b, h, s, hd = t.shape
    return t.transpose(0, 2, 1, 3).reshape(b, s, h * hd)


def dilated_window_attention(q, k, v, window, dilation):
    b, h, s, hd = q.shape
    w = window // dilation
    span = w * dilation
    sp = -(-s // span) * span
    nb = sp // span

    def to_sub(t):
        t = jnp.pad(t, ((0, 0), (0, 0), (0, sp - s), (0, 0)))
        t = t.reshape(b, h, sp // dilation, dilation, hd)
        t = jnp.swapaxes(t, 2, 3)
        return t.reshape(b, h, dilation, nb, w, hd)

    def with_prev(t):
        prev = jnp.pad(t[:, :, :, :-1], ((0, 0), (0, 0), (0, 0), (1, 0), (0, 0), (0, 0)))
        return jnp.concatenate([prev, t], axis=-2)

    qb = to_sub(q)
    kk = with_prev(to_sub(k))
    vv = with_prev(to_sub(v))
    scores = jnp.einsum('bhrnqd,bhrnkd->bhrnqk', qb, kk).astype(jnp.float32) * (hd ** -0.5)
    qi = jnp.arange(w)[:, None]
    kj = jnp.arange(2 * w)[None, :] - w
    dist = qi - kj
    band = (dist >= 0) & (dist <= w)
    has_prev = (jnp.arange(nb)[:, None, None] > 0) | (kj >= 0)[None]
    mask = band[None] & has_prev
    scores = jnp.where(mask, scores, -jnp.inf)
    m = jnp.max(scores, axis=-1, keepdims=True)
    p = jnp.exp(scores - m)
    l = jnp.sum(p, axis=-1, keepdims=True)
    out = jnp.einsum('bhrnqk,bhrnkd->bhrnqd', (p / l).astype(v.dtype), vv)
    lse = (m + jnp.log(l))[..., 0]

    def from_sub(t):
        rest = t.shape[5:]
        t = t.reshape((b, h, dilation, sp // dilation) + rest)
        t = jnp.swapaxes(t, 2, 3).reshape((b, h, sp) + rest)
        return t[:, :, :s]

    return from_sub(out), from_sub(lse)


def mixture_of_dilations(q, k, v):
    outs, lses = [], []
    for window, dilation in A_PATTERNS:
        o, lse = dilated_window_attention(q, k, v, window, dilation)
        outs.append(o.astype(jnp.float32))
        lses.append(lse)
    wts = jax.nn.softmax(jnp.stack(lses, axis=0), axis=0)
    return jnp.sum(wts[..., None] * jnp.stack(outs, axis=0), axis=0).astype(q.dtype)


def chunked_spatial_gating(u, v, v_norm_g, w_s, b_s):
    b, s, g, c = u.shape
    u = jax.nn.gelu(u)
    v = rms_norm(jax.nn.gelu(v), v_norm_g)
    sp = -(-s // B_CHUNK) * B_CHUNK
    v = jnp.pad(v, ((0, 0), (0, sp - s), (0, 0), (0, 0))).reshape(b, sp // B_CHUNK, B_CHUNK, g, c)
    causal = jnp.tril(jnp.ones((B_CHUNK, B_CHUNK), dtype=bool))
    ws = jnp.where(causal[None], w_s, jnp.zeros_like(w_s))
    mixed = jnp.einsum('gpq,bnqgc->bnpgc', ws, v) + jnp.swapaxes(b_s, 0, 1)[None, None, :, :, None]
    mixed = mixed.reshape(b, sp, g, c)[:, :s]
    return (u * mixed).reshape(b, s, g * c)


def ab_mixer(x, w_in, v_norm, w_s, b_s, w_out):
    z = x @ w_in
    qa, ka, va, ub, vb = jnp.split(
        z, [A_WIDTH, 2 * A_WIDTH, 3 * A_WIDTH, 3 * A_WIDTH + B_WIDTH], axis=-1)
    b, s, _ = x.shape
    a_out = mixture_of_dilations(to_heads(qa, A_HEADS, A_HEAD_DIM),
                                 to_heads(ka, A_HEADS, A_HEAD_DIM),
                                 to_heads(va, A_HEADS, A_HEAD_DIM))
    a_out = from_heads(a_out)
    b_out = chunked_spatial_gating(ub.reshape(b, s, B_GROUPS, B_GROUP_DIM),
                                   vb.reshape(b, s, B_GROUPS, B_GROUP_DIM),
                                   v_norm, w_s, b_s)
    return jnp.concatenate([a_out, b_out], axis=-1) @ w_out


def moba_attention(q, k, v):
    b, h, s, hd = q.shape
    sp = -(-s // C_BLOCK) * C_BLOCK
    pad = ((0, 0), (0, 0), (0, sp - s), (0, 0))
    q, k, v = jnp.pad(q, pad), jnp.pad(k, pad), jnp.pad(v, pad)
    nb = sp // C_BLOCK
    topk = min(C_TOPK, nb)
    kb = k.reshape(b, h, nb, C_BLOCK, hd)
    vb = v.reshape(b, h, nb, C_BLOCK, hd)
    k_mean = jnp.mean(kb.astype(jnp.float32), axis=3).astype(k.dtype)
    scale = hd ** -0.5
    n_chunks = sp // C_QCHUNK
    qc = jnp.moveaxis(q.reshape(b, h, n_chunks, C_QCHUNK, hd), 2, 0)
    bi = jnp.arange(b)[:, None, None, None]
    hi = jnp.arange(h)[None, :, None, None]
    block_ids = jnp.arange(nb)

    def one_chunk(args):
        c, qq = args
        start = c * C_QCHUNK
        own = start // C_BLOCK
        qpos = start + jnp.arange(C_QCHUNK)
        gate = jnp.einsum('bhqd,bhnd->bhqn', qq, k_mean).astype(jnp.float32)
        gate = jnp.where(block_ids < own, gate, -jnp.inf)
        _, idx = lax.top_k(gate, topk)
        valid = idx < own
        k_sel = kb[bi, hi, idx]
        v_sel = vb[bi, hi, idx]
        s_sel = jnp.einsum('bhqd,bhqtkd->bhqtk', qq, k_sel).astype(jnp.float32) * scale
        s_sel = jnp.where(valid[..., None], s_sel, -jnp.inf).reshape(b, h, C_QCHUNK, topk * C_BLOCK)
        k_own = lax.dynamic_index_in_dim(kb, own, axis=2, keepdims=False)
        v_own = lax.dynamic_index_in_dim(vb, own, axis=2, keepdims=False)
        s_own = jnp.einsum('bhqd,bhkd->bhqk', qq, k_own).astype(jnp.float32) * scale
        kpos = own * C_BLOCK + jnp.arange(C_BLOCK)
        s_own = jnp.where(kpos[None, :] <= qpos[:, None], s_own, -jnp.inf)
        p = jax.nn.softmax(jnp.concatenate([s_sel, s_own], axis=-1), axis=-1).astype(v.dtype)
        p_sel = p[..., :topk * C_BLOCK].reshape(b, h, C_QCHUNK, topk, C_BLOCK)
        p_own = p[..., topk * C_BLOCK:]
        return (jnp.einsum('bhqtk,bhqtkd->bhqd', p_sel, v_sel)
                + jnp.einsum('bhqk,bhkd->bhqd', p_own, v_own))

    out = lax.map(one_chunk, (jnp.arange(n_chunks), qc))
    out = jnp.moveaxis(out, 0, 2).reshape(b, h, sp, hd)
    return out[:, :, :s]


def c_mixer(x, w_in, w_out):
    q, k, v = jnp.split(x @ w_in, 3, axis=-1)
    o = moba_attention(to_heads(q, C_HEADS, C_HEAD_DIM),
                       to_heads(k, C_HEADS, C_HEAD_DIM),
                       to_heads(v, C_HEADS, C_HEAD_DIM))
    return from_heads(o) @ w_out


def setup_inputs(seed: int = 0) -> dict:
    key = jax.random.key(seed)
    ks = jax.random.split(key, 18)
    f32 = jnp.float32

    def nrm(k, shape, scale):
        return jax.random.normal(k, shape, f32) * scale

    def gain(k, shape):
        return 1.0 + 0.02 * jax.random.normal(k, shape, f32)

    return {
        'x': nrm(ks[0], (BATCH, SEQ, D_MODEL), 1.0),
        'ffn1_norm': gain(ks[1], (DEPTH, D_MODEL)),
        'ffn1_w_gate': nrm(ks[2], (DEPTH, D_MODEL, D_FF), D_MODEL ** -0.5),
        'ffn1_w_up': nrm(ks[3], (DEPTH, D_MODEL, D_FF), D_MODEL ** -0.5),
        'ffn1_w_down': nrm(ks[4], (DEPTH, D_FF, D_MODEL), D_FF ** -0.5),
        'mix_norm': gain(ks[5], (DEPTH, D_MODEL)),
        'ffn2_norm': gain(ks[6], (DEPTH, D_MODEL)),
        'ffn2_w_gate': nrm(ks[7], (DEPTH, D_MODEL, D_FF), D_MODEL ** -0.5),
        'ffn2_w_up': nrm(ks[8], (DEPTH, D_MODEL, D_FF), D_MODEL ** -0.5),
        'ffn2_w_down': nrm(ks[9], (DEPTH, D_FF, D_MODEL), D_FF ** -0.5),
        'ab_w_in': nrm(ks[10], (N_EVEN, D_MODEL, AB_IN_WIDTH), D_MODEL ** -0.5),
        'ab_v_norm': gain(ks[11], (N_EVEN, B_GROUPS, B_GROUP_DIM)),
        'ab_w_spatial': nrm(ks[12], (N_EVEN, B_GROUPS, B_CHUNK, B_CHUNK), B_CHUNK ** -0.5),
        'ab_b_spatial': gain(ks[13], (N_EVEN, B_GROUPS, B_CHUNK)),
        'ab_w_out': nrm(ks[14], (N_EVEN, AB_MIX_WIDTH, D_MODEL), AB_MIX_WIDTH ** -0.5),
        'c_w_in': nrm(ks[15], (N_ODD, D_MODEL, 3 * C_WIDTH), D_MODEL ** -0.5),
        'c_w_out': nrm(ks[16], (N_ODD, C_WIDTH, D_MODEL), C_WIDTH ** -0.5),
        'final_norm': gain(ks[17], (D_MODEL,)),
    }


def reference(x, ffn1_norm, ffn1_w_gate, ffn1_w_up, ffn1_w_down, mix_norm,
              ffn2_norm, ffn2_w_gate, ffn2_w_up, ffn2_w_down,
              ab_w_in, ab_v_norm, ab_w_spatial, ab_b_spatial, ab_w_out,
              c_w_in, c_w_out, final_norm):
    h = x
    for layer in range(DEPTH):
        h = h + 0.5 * swiglu(rms_norm(h, ffn1_norm[layer]),
                             ffn1_w_gate[layer], ffn1_w_up[layer], ffn1_w_down[layer])
        hn = rms_norm(h, mix_norm[layer])
        if layer % 2 == 0:
            i = layer // 2
            h = h + ab_mixer(hn, ab_w_in[i], ab_v_norm[i], ab_w_spatial[i],
                             ab_b_spatial[i], ab_w_out[i])
        else:
            i = layer // 2
            h = h + c_mixer(hn, c_w_in[i], c_w_out[i])
        h = h + 0.5 * swiglu(rms_norm(h, ffn2_norm[layer]),
                             ffn2_w_gate[layer], ffn2_w_up[layer], ffn2_w_down[layer])
    return rms_norm(h, final_norm)
```

```python
import functools

import jax
import jax.numpy as jnp
from jax import lax
from jax.experimental import pallas as pl
from jax.experimental.pallas import tpu as pltpu

F32 = jnp.float32
BF16 = jnp.bfloat16

D_MODEL = 1024
D_FF = 2816
EPS = 1e-6
HEAD_DIM = 64
LANES = 128
A_HEADS = 8
A_PATTERNS = ((128, 1), (512, 4), (2048, 16))
A_WIDTH = A_HEADS * HEAD_DIM
B_GROUPS = 4
B_CHUNK = 128
B_WIDTH = B_GROUPS * LANES
C_HEADS = 16
C_WIDTH = C_HEADS * HEAD_DIM
C_BLOCK = 256
C_TOPK = 3
SCALE = HEAD_DIM ** -0.5
NEG_BIG = -1e30
POS_BIG = 1e30

VMEM_LIMIT_BYTES = 56 * 1024 * 1024

FFN_ROWS = 1024
FFN_CHUNK = 256
PROJ_ROWS = 512
SWA_ROWS = 512
SWA_WIN = 128


def _params(*sem):
    return pltpu.CompilerParams(dimension_semantics=sem, vmem_limit_bytes=VMEM_LIMIT_BYTES)


def _rms_norm(x, g):
    return x * lax.rsqrt(jnp.mean(x * x, axis=-1, keepdims=True) + EPS) * g


def _resident(shape):
    nd = len(shape)
    return pl.BlockSpec(shape, lambda *_: (0,) * nd, pipeline_mode=pl.Buffered(1))


def _ffn_kernel(x_ref, g_ref, wg_ref, wu_ref, wd_ref, *rest, final):
    if final:
        gf_ref, o_ref, xn_ref, acc_ref = rest
    else:
        o_ref, xn_ref, acc_ref = rest
    xn_ref[...] = _rms_norm(x_ref[...], g_ref[...]).astype(BF16)
    acc_ref[...] = jnp.zeros_like(acc_ref)

    def body(c, carry):
        off = pl.multiple_of(c * FFN_CHUNK, FFN_CHUNK)
        xn = xn_ref[...]
        gate = jnp.dot(xn, wg_ref[:, pl.ds(off, FFN_CHUNK)], preferred_element_type=F32)
        up = jnp.dot(xn, wu_ref[:, pl.ds(off, FFN_CHUNK)], preferred_element_type=F32)
        act = (jax.nn.silu(gate) * up).astype(BF16)
        acc_ref[...] += jnp.dot(act, wd_ref[pl.ds(off, FFN_CHUNK), :], preferred_element_type=F32)
        return carry

    lax.fori_loop(0, D_FF // FFN_CHUNK, body, 0)
    y = x_ref[...] + 0.5 * acc_ref[...]
    if final:
        y = _rms_norm(y, gf_ref[...])
    o_ref[...] = y


def _ffn(x2, g, wg, wu, wd, final_g=None):
    t, d = x2.shape
    final = final_g is not None
    row_spec = pl.BlockSpec((FFN_ROWS, d), lambda i: (i, 0))
    in_specs = [row_spec, _resident((1, d)), _resident((d, D_FF)), _resident((d, D_FF)), _resident((D_FF, d))]
    args = [x2, g.reshape(1, d), wg, wu, wd]
    if final:
        in_specs.append(_resident((1, d)))
        args.append(final_g.reshape(1, d))
    return pl.pallas_call(
        functools.partial(_ffn_kernel, final=final),
        grid=(t // FFN_ROWS,),
        in_specs=in_specs,
        out_specs=row_spec,
        out_shape=jax.ShapeDtypeStruct((t, d), F32),
        scratch_shapes=[pltpu.VMEM((FFN_ROWS, d), BF16), pltpu.VMEM((FFN_ROWS, d), F32)],
        compiler_params=_params("parallel"),
        name="ffn_final" if final else "ffn",
    )(*args)


def _ab_in_kernel(h_ref, g_ref, w_ref, vn_ref, ws_ref, bs_ref, q_ref, k_ref, v_ref, b_ref):
    hn = _rms_norm(h_ref[...], g_ref[...]).astype(BF16)
    z = jnp.dot(hn, w_ref[...], preferred_element_type=F32)
    for hp in range(A_WIDTH // LANES):
        lo = hp * LANES
        q_ref[hp] = (z[:, lo:lo + LANES] * SCALE).astype(BF16)
        k_ref[hp] = z[:, A_WIDTH + lo:A_WIDTH + lo + LANES].astype(BF16)
        v_ref[hp] = z[:, 2 * A_WIDTH + lo:2 * A_WIDTH + lo + LANES].astype(BF16)
    row = lax.broadcasted_iota(jnp.int32, (B_CHUNK, B_CHUNK), 0)
    col = lax.broadcasted_iota(jnp.int32, (B_CHUNK, B_CHUNK), 1)
    causal = row >= col
    u_off, v_off = 3 * A_WIDTH, 3 * A_WIDTH + B_WIDTH
    for g in range(B_GROUPS):
        lo = g * LANES
        u = jax.nn.gelu(z[:, u_off + lo:u_off + lo + LANES])
        v = jax.nn.gelu(z[:, v_off + lo:v_off + lo + LANES])
        vn = _rms_norm(v, vn_ref[:, lo:lo + LANES]).astype(BF16)
        ws = jnp.where(causal, ws_ref[g], 0.0).astype(BF16)
        bias = bs_ref[:, g:g + 1]
        for c in range(PROJ_ROWS // B_CHUNK):
            r0 = c * B_CHUNK
            mixed = jnp.dot(ws, vn[r0:r0 + B_CHUNK], preferred_element_type=F32) + bias
            b_ref[r0:r0 + B_CHUNK, lo:lo + LANES] = (u[r0:r0 + B_CHUNK] * mixed).astype(BF16)


def _ab_in(h, g, w_in, v_norm, w_s, b_s):
    b, s, d = h.shape
    hp = A_WIDTH // LANES
    width = w_in.shape[1]
    qkv_spec = pl.BlockSpec((None, hp, PROJ_ROWS, LANES), lambda bi, i: (bi, 0, i, 0))
    qkv_shape = jax.ShapeDtypeStruct((b, hp, s, LANES), BF16)
    return pl.pallas_call(
        _ab_in_kernel,
        grid=(b, s // PROJ_ROWS),
        in_specs=[
            pl.BlockSpec((None, PROJ_ROWS, d), lambda bi, i: (bi, i, 0)),
            _resident((1, d)),
            _resident((d, width)),
            _resident((1, B_WIDTH)),
            _resident((B_GROUPS, B_CHUNK, B_CHUNK)),
            _resident((B_CHUNK, B_GROUPS)),
        ],
        out_specs=[qkv_spec, qkv_spec, qkv_spec,
                   pl.BlockSpec((None, PROJ_ROWS, B_WIDTH), lambda bi, i: (bi, i, 0))],
        out_shape=[qkv_shape, qkv_shape, qkv_shape, jax.ShapeDtypeStruct((b, s, B_WIDTH), BF16)],
        compiler_params=_params("parallel", "parallel"),
        name="ab_in_proj",
    )(h, g.reshape(1, d), w_in, v_norm.reshape(1, B_WIDTH), w_s, b_s.T)


def _swa_kernel(q_ref, kp_ref, kc_ref, vp_ref, vc_ref, o_ref, lse_ref, kk_ref, vv_ref):
    n = pl.program_id(3)
    kk_ref[0:SWA_WIN] = kp_ref[...]
    kk_ref[SWA_WIN:] = kc_ref[...]
    vv_ref[0:SWA_WIN] = vp_ref[...]
    vv_ref[SWA_WIN:] = vc_ref[...]
    lane = lax.broadcasted_iota(jnp.int32, (SWA_WIN, LANES), 1)
    head0 = lane < HEAD_DIM
    qi = lax.broadcasted_iota(jnp.int32, (SWA_WIN, 2 * SWA_WIN), 0)
    kc = lax.broadcasted_iota(jnp.int32, (SWA_WIN, 2 * SWA_WIN), 1)
    band = (kc >= qi) & (kc <= qi + SWA_WIN)
    for j in range(SWA_ROWS // SWA_WIN):
        r0 = j * SWA_WIN
        q = q_ref[r0:r0 + SWA_WIN]
        kk = kk_ref[r0:r0 + 2 * SWA_WIN]
        vv = vv_ref[r0:r0 + 2 * SWA_WIN]
        mask = band & ((kc >= SWA_WIN) | (n * SWA_ROWS + r0 > 0))
        outs, lses = [], []
        for hmask in (head0, ~head0):
            qm = jnp.where(hmask, q, jnp.zeros_like(q))
            s = lax.dot_general(qm, kk, (((1,), (1,)), ((), ())), preferred_element_type=F32)
            s = jnp.where(mask, s, NEG_BIG)
            m = jnp.max(s, axis=-1, keepdims=True)
            p = jnp.exp(s - m)
            l = jnp.sum(p, axis=-1, keepdims=True)
            pv = jnp.dot(p.astype(BF16), vv, preferred_element_type=F32)
            outs.append(pv / l)
            lses.append(jnp.broadcast_to(m + jnp.log(l), (SWA_WIN, LANES)))
        o_ref[r0:r0 + SWA_WIN] = jnp.where(head0, outs[0], outs[1]).astype(BF16)
        lse_ref[r0:r0 + SWA_WIN] = jnp.where(head0, lses[0], lses[1])


def _swa(q, k, v, dilation):
    b, hp, s, _ = q.shape
    sub = s // dilation
    view = (b, hp, sub, dilation * LANES)
    q, k, v = q.reshape(view), k.reshape(view), v.reshape(view)
    per_step = SWA_ROWS // SWA_WIN
    cur = pl.BlockSpec((None, None, SWA_ROWS, LANES), lambda bi, h, r, n: (bi, h, n, r))
    prev = pl.BlockSpec((None, None, SWA_WIN, LANES),
                        lambda bi, h, r, n: (bi, h, jnp.maximum(n * per_step - 1, 0), r))
    o, lse = pl.pallas_call(
        _swa_kernel,
        grid=(b, hp, dilation, sub // SWA_ROWS),
        in_specs=[cur, prev, cur, prev, cur],
        out_specs=[cur, cur],
        out_shape=[jax.ShapeDtypeStruct(view, BF16), jax.ShapeDtypeStruct(view, F32)],
        scratch_shapes=[pltpu.VMEM((SWA_ROWS + SWA_WIN, LANES), BF16),
                        pltpu.VMEM((SWA_ROWS + SWA_WIN, LANES), BF16)],
        compiler_params=_params("parallel", "parallel", "parallel", "parallel"),
        name=f"swa_d{dilation}",
    )(q, k, k, v, v)
    return o.reshape(b, hp, s, LANES), lse.reshape(b, hp, s, LANES)


def _ab_out_kernel(h_ref, o1_ref, o2_ref, o3_ref, l1_ref, l2_ref, l3_ref, b_ref, w_ref, out_ref, cat_ref):
    for hp in range(A_WIDTH // LANES):
        l1, l2, l3 = l1_ref[hp], l2_ref[hp], l3_ref[hp]
        m = jnp.maximum(jnp.maximum(l1, l2), l3)
        e1, e2, e3 = jnp.exp(l1 - m), jnp.exp(l2 - m), jnp.exp(l3 - m)
        mix = (e1 * o1_ref[hp].astype(F32) + e2 * o2_ref[hp].astype(F32) + e3 * o3_ref[hp].astype(F32))
        cat_ref[:, hp * LANES:(hp + 1) * LANES] = (mix / (e1 + e2 + e3)).astype(BF16)
    cat_ref[:, A_WIDTH:] = b_ref[...]
    out_ref[...] = h_ref[...] + jnp.dot(cat_ref[...], w_ref[...], preferred_element_type=F32)


def _ab_out(h, outs, lses, b_out, w_out):
    b, s, d = h.shape
    hp = A_WIDTH // LANES
    row = pl.BlockSpec((None, PROJ_ROWS, d), lambda bi, i: (bi, i, 0))
    heads = pl.BlockSpec((None, hp, PROJ_ROWS, LANES), lambda bi, i: (bi, 0, i, 0))
    return pl.pallas_call(
        _ab_out_kernel,
        grid=(b, s // PROJ_ROWS),
        in_specs=[row] + [heads] * 6 + [pl.BlockSpec((None, PROJ_ROWS, B_WIDTH), lambda bi, i: (bi, i, 0)),
                                        _resident((A_WIDTH + B_WIDTH, d))],
        out_specs=row,
        out_shape=jax.ShapeDtypeStruct((b, s, d), F32),
        scratch_shapes=[pltpu.VMEM((PROJ_ROWS, A_WIDTH + B_WIDTH), BF16)],
        compiler_params=_params("parallel", "parallel"),
        name="ab_out_proj",
    )(h, *outs, *lses, b_out, w_out)


def _c_in_kernel(h_ref, g_ref, wqk_ref, wvt_ref, q_ref, k_ref, vt_ref):
    hn = _rms_norm(h_ref[...], g_ref[...]).astype(BF16)
    z = jnp.dot(hn, wqk_ref[...], preferred_element_type=F32)
    zt = lax.dot_general(wvt_ref[...], hn, (((1,), (1,)), ((), ())), preferred_element_type=F32)
    for hp in range(C_WIDTH // LANES):
        lo = hp * LANES
        for blk in range(PROJ_ROWS // C_BLOCK):
            r0 = blk * C_BLOCK
            q_ref[hp, blk] = (z[r0:r0 + C_BLOCK, lo:lo + LANES] * SCALE).astype(BF16)
            k_ref[hp, blk] = z[r0:r0 + C_BLOCK, C_WIDTH + lo:C_WIDTH + lo + LANES].astype(BF16)
            vt_ref[hp, blk] = zt[lo:lo + LANES, r0:r0 + C_BLOCK].astype(BF16)


def _c_in(h, g, w_qk, w_vt):
    b, s, d = h.shape
    hp = C_WIDTH // LANES
    nb = s // C_BLOCK
    per_step = PROJ_ROWS // C_BLOCK
    qk_spec = pl.BlockSpec((None, hp, per_step, C_BLOCK, LANES), lambda bi, i: (bi, 0, i, 0, 0))
    vt_spec = pl.BlockSpec((None, hp, per_step, LANES, C_BLOCK), lambda bi, i: (bi, 0, i, 0, 0))
    qk_shape = jax.ShapeDtypeStruct((b, hp, nb, C_BLOCK, LANES), BF16)
    return pl.pallas_call(
        _c_in_kernel,
        grid=(b, s // PROJ_ROWS),
        in_specs=[pl.BlockSpec((None, PROJ_ROWS, d), lambda bi, i: (bi, i, 0)),
                  _resident((1, d)), _resident((d, 2 * C_WIDTH)), _resident((C_WIDTH, d))],
        out_specs=[qk_spec, qk_spec, vt_spec],
        out_shape=[qk_shape, qk_shape, jax.ShapeDtypeStruct((b, hp, nb, LANES, C_BLOCK), BF16)],
        compiler_params=_params("parallel", "parallel"),
        name="c_in_proj",
    )(h, g.reshape(1, d), w_qk, w_vt)


def _moba_kernel(q_ref, k_ref, vt_ref, o_ref, kmean_ref, sel_ref):
    c = pl.program_id(2)
    nb = k_ref.shape[0]

    @pl.when(c == 0)
    def _():
        for j in range(nb):
            kmean_ref[j:j + 1, :] = jnp.mean(k_ref[j].astype(F32), axis=0, keepdims=True)

    q = q_ref[...]
    lane = lax.broadcasted_iota(jnp.int32, q.shape, 1)
    qms = [jnp.where(lane < HEAD_DIM, q, jnp.zeros_like(q)), jnp.where(lane >= HEAD_DIM, q, jnp.zeros_like(q))]
    contract_last = (((1,), (1,)), ((), ()))

    kmean = kmean_ref[...].astype(BF16)
    blk = lax.broadcasted_iota(jnp.int32, (nb, C_BLOCK), 0)
    for h in range(2):
        gate = lax.dot_general(kmean, qms[h], contract_last, preferred_element_type=F32)
        gate = jnp.where(blk < c, gate, NEG_BIG)
        rank = jnp.zeros((nb, C_BLOCK), jnp.int32)
        for jp in range(nb):
            other = gate[jp:jp + 1, :]
            ahead = (other > gate) | ((other == gate) & (blk > jp))
            rank = rank + ahead.astype(jnp.int32)
        sel_ref[h] = ((rank < C_TOPK) & (blk < c)).astype(F32)

    def attend(kj, vtj, carry, shift_fn):
        new = []
        for h in range(2):
            m, l, acc = carry[h]
            s = lax.dot_general(kj, qms[h], contract_last, preferred_element_type=F32)
            s, m_new, shift = shift_fn(h, s, m)
            alpha = jnp.exp(m - m_new)
            p = jnp.exp(s - shift)
            l = alpha * l + jnp.sum(p, axis=0, keepdims=True)
            pv = jnp.dot(vtj[h * HEAD_DIM:(h + 1) * HEAD_DIM, :], p.astype(BF16), preferred_element_type=F32)
            new.append((m_new, l, alpha * acc + pv))
        return tuple(new)

    def past_block(j, carry):
        def shift_fn(h, s, m):
            chosen = sel_ref[h, pl.ds(j, 1), :] > 0.5
            m_new = jnp.where(chosen, jnp.maximum(m, jnp.max(s, axis=0, keepdims=True)), m)
            return s, m_new, jnp.where(chosen, m_new, POS_BIG)
        return attend(k_ref[j], vt_ref[j], carry, shift_fn)

    init = tuple((jnp.full((1, C_BLOCK), NEG_BIG, F32), jnp.zeros((1, C_BLOCK), F32),
                  jnp.zeros((HEAD_DIM, C_BLOCK), F32)) for _ in range(2))
    carry = lax.fori_loop(0, c, past_block, init)

    kpos = lax.broadcasted_iota(jnp.int32, (C_BLOCK, C_BLOCK), 0)
    qpos = lax.broadcasted_iota(jnp.int32, (C_BLOCK, C_BLOCK), 1)

    def own_shift(h, s, m):
        s = jnp.where(kpos <= qpos, s, NEG_BIG)
        m_new = jnp.maximum(m, jnp.max(s, axis=0, keepdims=True))
        return s, m_new, m_new

    carry = attend(k_ref[c], vt_ref[c], carry, own_shift)
    out_t = jnp.concatenate([acc / l for (_, l, acc) in carry], axis=0)
    o_ref[...] = out_t.T.astype(BF16)


def _moba(q, k, vt):
    b, hp, nb = q.shape[:3]
    return pl.pallas_call(
        _moba_kernel,
        grid=(b, hp, nb),
        in_specs=[
            pl.BlockSpec((None, None, None, C_BLOCK, LANES), lambda bi, h, c: (bi, h, c, 0, 0)),
            pl.BlockSpec((None, None, nb, C_BLOCK, LANES), lambda bi, h, c: (bi, h, 0, 0, 0)),
            pl.BlockSpec((None, None, nb, LANES, C_BLOCK), lambda bi, h, c: (bi, h, 0, 0, 0)),
        ],
        out_specs=pl.BlockSpec((None, None, None, C_BLOCK, LANES), lambda bi, h, c: (bi, h, c, 0, 0)),
        out_shape=jax.ShapeDtypeStruct((b, hp, nb, C_BLOCK, LANES), BF16),
        scratch_shapes=[pltpu.VMEM((nb, LANES), F32), pltpu.VMEM((2, nb, C_BLOCK), F32)],
        compiler_params=_params("parallel", "parallel", "arbitrary"),
        name="moba",
    )(q, k, vt)


def _c_out_kernel(h_ref, o_ref, w_ref, out_ref, cat_ref):
    for hp in range(C_WIDTH // LANES):
        cat_ref[:, hp * LANES:(hp + 1) * LANES] = o_ref[hp]
    out_ref[...] = h_ref[...] + jnp.dot(cat_ref[...], w_ref[...], preferred_element_type=F32)


def _c_out(h, o, w_out):
    b, s, d = h.shape
    hp = C_WIDTH // LANES
    row = pl.BlockSpec((None, PROJ_ROWS, d), lambda bi, i: (bi, i, 0))
    return pl.pallas_call(
        _c_out_kernel,
        grid=(b, s // PROJ_ROWS),
        in_specs=[row, pl.BlockSpec((None, hp, PROJ_ROWS, LANES), lambda bi, i: (bi, 0, i, 0)),
                  _resident((C_WIDTH, d))],
        out_specs=row,
        out_shape=jax.ShapeDtypeStruct((b, s, d), F32),
        scratch_shapes=[pltpu.VMEM((PROJ_ROWS, C_WIDTH), BF16)],
        compiler_params=_params("parallel", "parallel"),
        name="c_out_proj",
    )(h, o, w_out)


def kernel(x, ffn1_norm, ffn1_w_gate, ffn1_w_up, ffn1_w_down, mix_norm, ffn2_norm, ffn2_w_gate, ffn2_w_up,
           ffn2_w_down, ab_w_in, ab_v_norm, ab_w_spatial, ab_b_spatial, ab_w_out, c_w_in, c_w_out, final_norm):
    b, s, d = x.shape
    assert d == D_MODEL and s % (A_PATTERNS[-1][0]) == 0 and s % PROJ_ROWS == 0 and (b * s) % FFN_ROWS == 0
    bf = lambda w: w.astype(BF16)

    def ffn(h, layer, norm, wg, wu, wd, final_g=None):
        y = _ffn(h.reshape(b * s, d), norm[layer], bf(wg[layer]), bf(wu[layer]), bf(wd[layer]), final_g)
        return y.reshape(b, s, d)

    h = x
    h = ffn(h, 0, ffn1_norm, ffn1_w_gate, ffn1_w_up, ffn1_w_down)
    q, k, v, b_out = _ab_in(h, mix_norm[0], bf(ab_w_in[0]), ab_v_norm[0], ab_w_spatial[0], ab_b_spatial[0])
    outs, lses = zip(*[_swa(q, k, v, dil) for _, dil in A_PATTERNS])
    h = _ab_out(h, outs, lses, b_out, bf(ab_w_out[0]))
    h = ffn(h, 0, ffn2_norm, ffn2_w_gate, ffn2_w_up, ffn2_w_down)
    h = ffn(h, 1, ffn1_norm, ffn1_w_gate, ffn1_w_up, ffn1_w_down)
    w_c = c_w_in[0]
    qc, kc, vtc = _c_in(h, mix_norm[1], bf(w_c[:, :2 * C_WIDTH]), bf(w_c[:, 2 * C_WIDTH:].T))
    o = _moba(qc, kc, vtc).reshape(b, C_WIDTH // LANES, s, LANES)
    h = _c_out(h, o, bf(c_w_out[0]))
    h = ffn(h, 1, ffn2_norm, ffn2_w_gate, ffn2_w_up, ffn2_w_down, final_g=final_norm)
    return h
```

```python
import functools

import jax
import jax.numpy as jnp
from jax import lax
from jax.experimental import pallas as pl
from jax.experimental.pallas import tpu as pltpu

F32 = jnp.float32
BF16 = jnp.bfloat16

D_MODEL = 1024
D_FF = 2816
EPS = 1e-6
HEAD_DIM = 64
LANES = 128
A_HEADS = 8
A_PATTERNS = ((128, 1), (512, 4), (2048, 16))
A_WIDTH = A_HEADS * HEAD_DIM
B_GROUPS = 4
B_CHUNK = 128
B_WIDTH = B_GROUPS * LANES
C_HEADS = 16
C_WIDTH = C_HEADS * HEAD_DIM
C_BLOCK = 256
C_TOPK = 3
SCALE = HEAD_DIM ** -0.5
NEG_BIG = -1e30

M_INIT = -(2.0 ** 60)
SCORE_OFF = -(2.0 ** 100)

VMEM_LIMIT_BYTES = 56 * 1024 * 1024

FFN_ROWS = 1024
FFN_CHUNK = 256
PROJ_ROWS = 512
SWA_ROWS = 512
SWA_WIN = 128
MOBA_ROWS = 1024

CONTRACT_LAST = (((1,), (1,)), ((), ()))


def _params(*sem):
    return pltpu.CompilerParams(dimension_semantics=sem, vmem_limit_bytes=VMEM_LIMIT_BYTES)


def _rms_norm(x, g):
    return x * lax.rsqrt(jnp.mean(x * x, axis=-1, keepdims=True) + EPS) * g


def _resident(shape):
    nd = len(shape)
    return pl.BlockSpec(shape, lambda *_: (0,) * nd, pipeline_mode=pl.Buffered(1))


def _ffn_kernel(x_ref, g_ref, wg_ref, wu_ref, wd_ref, *rest, final):
    if final:
        gf_ref, o_ref, xn_ref, acc_ref = rest
    else:
        o_ref, xn_ref, acc_ref = rest
    xn_ref[...] = _rms_norm(x_ref[...], g_ref[...]).astype(BF16)
    acc_ref[...] = jnp.zeros_like(acc_ref)

    def body(c, carry):
        off = pl.multiple_of(c * FFN_CHUNK, FFN_CHUNK)
        xn = xn_ref[...]
        gate = jnp.dot(xn, wg_ref[:, pl.ds(off, FFN_CHUNK)], preferred_element_type=F32)
        up = jnp.dot(xn, wu_ref[:, pl.ds(off, FFN_CHUNK)], preferred_element_type=F32)
        act = (jax.nn.silu(gate) * up).astype(BF16)
        acc_ref[...] += jnp.dot(act, wd_ref[pl.ds(off, FFN_CHUNK), :], preferred_element_type=F32)
        return carry

    lax.fori_loop(0, D_FF // FFN_CHUNK, body, 0)
    y = x_ref[...] + 0.5 * acc_ref[...]
    if final:
        y = _rms_norm(y, gf_ref[...])
    o_ref[...] = y


def _ffn(x2, g, wg, wu, wd, final_g=None):
    t, d = x2.shape
    final = final_g is not None
    row_spec = pl.BlockSpec((FFN_ROWS, d), lambda i: (i, 0))
    in_specs = [row_spec, _resident((1, d)), _resident((d, D_FF)), _resident((d, D_FF)), _resident((D_FF, d))]
    args = [x2, g.reshape(1, d), wg, wu, wd]
    if final:
        in_specs.append(_resident((1, d)))
        args.append(final_g.reshape(1, d))
    return pl.pallas_call(
        functools.partial(_ffn_kernel, final=final),
        grid=(t // FFN_ROWS,),
        in_specs=in_specs,
        out_specs=row_spec,
        out_shape=jax.ShapeDtypeStruct((t, d), F32),
        scratch_shapes=[pltpu.VMEM((FFN_ROWS, d), BF16), pltpu.VMEM((FFN_ROWS, d), F32)],
        compiler_params=_params("parallel"),
        name="ffn_final" if final else "ffn",
    )(*args)


def _ab_in_kernel(h_ref, g_ref, w_ref, vn_ref, ws_ref, bs_ref, q_ref, k_ref, v_ref, b_ref):
    hn = _rms_norm(h_ref[...], g_ref[...]).astype(BF16)
    z = jnp.dot(hn, w_ref[...], preferred_element_type=F32)
    for hp in range(A_WIDTH // LANES):
        lo = hp * LANES
        q_ref[hp] = (z[:, lo:lo + LANES] * SCALE).astype(BF16)
        k_ref[hp] = z[:, A_WIDTH + lo:A_WIDTH + lo + LANES].astype(BF16)
        v_ref[hp] = z[:, 2 * A_WIDTH + lo:2 * A_WIDTH + lo + LANES].astype(BF16)
    row = lax.broadcasted_iota(jnp.int32, (B_CHUNK, B_CHUNK), 0)
    col = lax.broadcasted_iota(jnp.int32, (B_CHUNK, B_CHUNK), 1)
    causal = row >= col
    u_off, v_off = 3 * A_WIDTH, 3 * A_WIDTH + B_WIDTH
    for g in range(B_GROUPS):
        lo = g * LANES
        u = jax.nn.gelu(z[:, u_off + lo:u_off + lo + LANES])
        v = jax.nn.gelu(z[:, v_off + lo:v_off + lo + LANES])
        vn = _rms_norm(v, vn_ref[:, lo:lo + LANES]).astype(BF16)
        ws = jnp.where(causal, ws_ref[g], 0.0).astype(BF16)
        bias = bs_ref[:, g:g + 1]
        for c in range(PROJ_ROWS // B_CHUNK):
            r0 = c * B_CHUNK
            mixed = jnp.dot(ws, vn[r0:r0 + B_CHUNK], preferred_element_type=F32) + bias
            b_ref[r0:r0 + B_CHUNK, lo:lo + LANES] = (u[r0:r0 + B_CHUNK] * mixed).astype(BF16)


def _ab_in(h, g, w_in, v_norm, w_s, b_s):
    b, s, d = h.shape
    hp = A_WIDTH // LANES
    width = w_in.shape[1]
    qkv_spec = pl.BlockSpec((None, hp, PROJ_ROWS, LANES), lambda bi, i: (bi, 0, i, 0))
    qkv_shape = jax.ShapeDtypeStruct((b, hp, s, LANES), BF16)
    return pl.pallas_call(
        _ab_in_kernel,
        grid=(b, s // PROJ_ROWS),
        in_specs=[
            pl.BlockSpec((None, PROJ_ROWS, d), lambda bi, i: (bi, i, 0)),
            _resident((1, d)),
            _resident((d, width)),
            _resident((1, B_WIDTH)),
            _resident((B_GROUPS, B_CHUNK, B_CHUNK)),
            _resident((B_CHUNK, B_GROUPS)),
        ],
        out_specs=[qkv_spec, qkv_spec, qkv_spec,
                   pl.BlockSpec((None, PROJ_ROWS, B_WIDTH), lambda bi, i: (bi, i, 0))],
        out_shape=[qkv_shape, qkv_shape, qkv_shape, jax.ShapeDtypeStruct((b, s, B_WIDTH), BF16)],
        compiler_params=_params("parallel", "parallel"),
        name="ab_in_proj",
    )(h, g.reshape(1, d), w_in, v_norm.reshape(1, B_WIDTH), w_s, b_s.T)


def _swa_kernel(q_ref, kp_ref, kc_ref, vp_ref, vc_ref, o_ref, lse_ref, kk_ref, vv_ref):
    n = pl.program_id(3)
    kk_ref[0:SWA_WIN] = kp_ref[...]
    kk_ref[SWA_WIN:] = kc_ref[...]
    vv_ref[0:SWA_WIN] = vp_ref[...]
    vv_ref[SWA_WIN:] = vc_ref[...]
    lane = lax.broadcasted_iota(jnp.int32, (SWA_WIN, LANES), 1)
    head0 = lane < HEAD_DIM
    qi = lax.broadcasted_iota(jnp.int32, (SWA_WIN, 2 * SWA_WIN), 0)
    kc = lax.broadcasted_iota(jnp.int32, (SWA_WIN, 2 * SWA_WIN), 1)
    band = (kc >= qi) & (kc <= qi + SWA_WIN)
    for j in range(SWA_ROWS // SWA_WIN):
        r0 = j * SWA_WIN
        q = q_ref[r0:r0 + SWA_WIN]
        kk = kk_ref[r0:r0 + 2 * SWA_WIN]
        vv = vv_ref[r0:r0 + 2 * SWA_WIN]
        mask = band & ((kc >= SWA_WIN) | (n * SWA_ROWS + r0 > 0))
        outs, lses = [], []
        for hmask in (head0, ~head0):
            qm = jnp.where(hmask, q, jnp.zeros_like(q))
            s = lax.dot_general(qm, kk, CONTRACT_LAST, preferred_element_type=F32)
            s = jnp.where(mask, s, NEG_BIG)
            m = jnp.max(s, axis=-1, keepdims=True)
            p = jnp.exp(s - m)
            l = jnp.sum(p, axis=-1, keepdims=True)
            pv = jnp.dot(p.astype(BF16), vv, preferred_element_type=F32)
            outs.append(pv / l)
            lses.append(jnp.broadcast_to(m + jnp.log(l), (SWA_WIN, LANES)))
        o_ref[r0:r0 + SWA_WIN] = jnp.where(head0, outs[0], outs[1]).astype(BF16)
        lse_ref[r0:r0 + SWA_WIN] = jnp.where(head0, lses[0], lses[1])


def _swa(q, k, v, dilation):
    b, hp, s, _ = q.shape
    sub = s // dilation
    view = (b, hp, sub, dilation * LANES)
    q, k, v = q.reshape(view), k.reshape(view), v.reshape(view)
    per_step = SWA_ROWS // SWA_WIN
    cur = pl.BlockSpec((None, None, SWA_ROWS, LANES), lambda bi, h, r, n: (bi, h, n, r))
    prev = pl.BlockSpec((None, None, SWA_WIN, LANES),
                        lambda bi, h, r, n: (bi, h, jnp.maximum(n * per_step - 1, 0), r))
    o, lse = pl.pallas_call(
        _swa_kernel,
        grid=(b, hp, dilation, sub // SWA_ROWS),
        in_specs=[cur, prev, cur, prev, cur],
        out_specs=[cur, cur],
        out_shape=[jax.ShapeDtypeStruct(view, BF16), jax.ShapeDtypeStruct(view, F32)],
        scratch_shapes=[pltpu.VMEM((SWA_ROWS + SWA_WIN, LANES), BF16),
                        pltpu.VMEM((SWA_ROWS + SWA_WIN, LANES), BF16)],
        compiler_params=_params("parallel", "parallel", "parallel", "parallel"),
        name=f"swa_d{dilation}",
    )(q, k, k, v, v)
    return o.reshape(b, hp, s, LANES), lse.reshape(b, hp, s, LANES)


def _ab_out_kernel(h_ref, o1_ref, o2_ref, o3_ref, l1_ref, l2_ref, l3_ref, b_ref, w_ref, out_ref, cat_ref):
    for hp in range(A_WIDTH // LANES):
        l1, l2, l3 = l1_ref[hp], l2_ref[hp], l3_ref[hp]
        m = jnp.maximum(jnp.maximum(l1, l2), l3)
        e1, e2, e3 = jnp.exp(l1 - m), jnp.exp(l2 - m), jnp.exp(l3 - m)
        mix = (e1 * o1_ref[hp].astype(F32) + e2 * o2_ref[hp].astype(F32) + e3 * o3_ref[hp].astype(F32))
        cat_ref[:, hp * LANES:(hp + 1) * LANES] = (mix / (e1 + e2 + e3)).astype(BF16)
    cat_ref[:, A_WIDTH:] = b_ref[...]
    out_ref[...] = h_ref[...] + jnp.dot(cat_ref[...], w_ref[...], preferred_element_type=F32)


def _ab_out(h, outs, lses, b_out, w_out):
    b, s, d = h.shape
    hp = A_WIDTH // LANES
    row = pl.BlockSpec((None, PROJ_ROWS, d), lambda bi, i: (bi, i, 0))
    heads = pl.BlockSpec((None, hp, PROJ_ROWS, LANES), lambda bi, i: (bi, 0, i, 0))
    return pl.pallas_call(
        _ab_out_kernel,
        grid=(b, s // PROJ_ROWS),
        in_specs=[row] + [heads] * 6 + [pl.BlockSpec((None, PROJ_ROWS, B_WIDTH), lambda bi, i: (bi, i, 0)),
                                        _resident((A_WIDTH + B_WIDTH, d))],
        out_specs=row,
        out_shape=jax.ShapeDtypeStruct((b, s, d), F32),
        scratch_shapes=[pltpu.VMEM((PROJ_ROWS, A_WIDTH + B_WIDTH), BF16)],
        compiler_params=_params("parallel", "parallel"),
        name="ab_out_proj",
    )(h, *outs, *lses, b_out, w_out)


def _c_in_kernel(h_ref, g_ref, w_ref, q_ref, k_ref, v_ref):
    hn = _rms_norm(h_ref[...], g_ref[...]).astype(BF16)
    z = jnp.dot(hn, w_ref[...], preferred_element_type=F32)
    for hp in range(C_WIDTH // LANES):
        lo = hp * LANES
        q_ref[hp] = (z[:, lo:lo + LANES] * SCALE).astype(BF16)
        k_ref[hp] = z[:, C_WIDTH + lo:C_WIDTH + lo + LANES].astype(BF16)
        v_ref[hp] = z[:, 2 * C_WIDTH + lo:2 * C_WIDTH + lo + LANES].astype(BF16)


def _c_in(h, g, w_in):
    b, s, d = h.shape
    hp = C_WIDTH // LANES
    spec = pl.BlockSpec((None, hp, PROJ_ROWS, LANES), lambda bi, i: (bi, 0, i, 0))
    shape = jax.ShapeDtypeStruct((b, hp, s, LANES), BF16)
    return pl.pallas_call(
        _c_in_kernel,
        grid=(b, s // PROJ_ROWS),
        in_specs=[pl.BlockSpec((None, PROJ_ROWS, d), lambda bi, i: (bi, i, 0)),
                  _resident((1, d)), _resident((d, 3 * C_WIDTH))],
        out_specs=[spec, spec, spec],
        out_shape=[shape, shape, shape],
        compiler_params=_params("parallel", "parallel"),
        name="c_in_proj",
    )(h, g.reshape(1, d), w_in)


def _moba_select(q_ref, k_ref, qaug_ref, st_ref, m_ref, nb):
    seq = q_ref.shape[0]
    kmean = jnp.concatenate(
        [jnp.mean(k_ref[j * C_BLOCK:(j + 1) * C_BLOCK].astype(F32), axis=0, keepdims=True) for j in range(nb)],
        axis=0).astype(BF16)
    lane = lax.broadcasted_iota(jnp.int32, (MOBA_ROWS, LANES), 1)
    blk = lax.broadcasted_iota(jnp.int32, (nb, MOBA_ROWS), 0)
    qpos = lax.broadcasted_iota(jnp.int32, (nb, MOBA_ROWS), 1)
    zeros = lambda n: jnp.zeros((n, MOBA_ROWS), F32)

    def chunk(t, carry):
        r0 = pl.multiple_of(t * MOBA_ROWS, MOBA_ROWS)
        q = q_ref[pl.ds(r0, MOBA_ROWS), :]
        past = blk < (qpos + r0) // C_BLOCK
        for h in range(2):
            qm = jnp.where(lane < HEAD_DIM if h == 0 else lane >= HEAD_DIM, q, jnp.zeros_like(q))
            gate = lax.dot_general(kmean, qm, CONTRACT_LAST, preferred_element_type=F32)
            gate = jnp.where(past, gate, -jnp.inf)
            chosen = jnp.zeros(gate.shape, jnp.bool_)
            for _ in range(C_TOPK):
                best = jnp.max(gate, axis=0, keepdims=True)
                first = jnp.min(jnp.where(gate == best, blk, nb), axis=0, keepdims=True)
                hit = blk == first
                chosen = chosen | hit
                gate = jnp.where(hit, -jnp.inf, gate)
            unselected = 1.0 - (chosen & past).astype(F32)
            if h == 0:
                extra_t = jnp.concatenate([zeros(HEAD_DIM), unselected, zeros(HEAD_DIM - nb)], axis=0)
            else:
                extra_t = jnp.concatenate([unselected, zeros(LANES - nb)], axis=0)
            qaug_ref[h, pl.ds(r0, MOBA_ROWS), :] = (qm.astype(F32) + extra_t.T).astype(BF16)
        return carry

    lax.fori_loop(0, seq // MOBA_ROWS, chunk, 0)
    st_ref[...] = jnp.zeros_like(st_ref)
    m_ref[...] = jnp.full(m_ref.shape, M_INIT, F32)


def _moba_kernel(q_ref, k_ref, v_ref, o_ref, qaug_ref, st_ref, m_ref):
    j = pl.program_id(2)
    seq = q_ref.shape[0]
    nb = seq // C_BLOCK

    @pl.when(j == 0)
    def _():
        _moba_select(q_ref, k_ref, qaug_ref, st_ref, m_ref, nb)

    j0 = pl.multiple_of(j * C_BLOCK, C_BLOCK)
    kj = k_ref[pl.ds(j0, C_BLOCK), :]
    vj = v_ref[pl.ds(j0, C_BLOCK), :]
    lane = lax.broadcasted_iota(jnp.int32, (C_BLOCK, LANES), 1)
    in_head = [lane < HEAD_DIM, lane >= HEAD_DIM]
    flag_lane = [HEAD_DIM + j, j]
    zero = jnp.zeros_like(kj)
    k_own = [jnp.where(in_head[h], kj, zero) for h in range(2)]
    k_past = [jnp.where(in_head[h], kj, jnp.where(lane == flag_lane[h], SCORE_OFF, 0.0).astype(BF16))
              for h in range(2)]
    v_aug = [jnp.where(in_head[h], vj, jnp.ones_like(vj)) for h in range(2)]

    def update(h, rows, n, k_aug, mask):
        qa = qaug_ref[h, pl.ds(rows, n), :]
        s = lax.dot_general(qa, k_aug, CONTRACT_LAST, preferred_element_type=F32)
        if mask is not None:
            s = jnp.where(mask, s, SCORE_OFF)
        m_old = m_ref[h, pl.ds(rows, n), :]
        m_new = jnp.maximum(m_old, jnp.max(s, axis=-1, keepdims=True))
        p = jnp.exp(s - jnp.concatenate([m_new, m_new], axis=1))
        pv = jnp.dot(p.astype(BF16), v_aug[h], preferred_element_type=F32)
        st = jnp.exp(m_old - m_new) * st_ref[h, pl.ds(rows, n), :] + pv
        return st, m_new

    qi = lax.broadcasted_iota(jnp.int32, (C_BLOCK, C_BLOCK), 0)
    ki = lax.broadcasted_iota(jnp.int32, (C_BLOCK, C_BLOCK), 1)
    done = []
    for h in range(2):
        st, _ = update(h, j0, C_BLOCK, k_own[h], ki <= qi)
        done.append(st / pltpu.roll(st, HEAD_DIM, 1))
    o_ref[pl.ds(j0, C_BLOCK), :] = jnp.where(in_head[0], done[0], done[1]).astype(BF16)

    def chunk(t, carry):
        rows = pl.multiple_of(t * MOBA_ROWS, MOBA_ROWS)
        for h in range(2):
            st, m_new = update(h, rows, MOBA_ROWS, k_past[h], None)
            st_ref[h, pl.ds(rows, MOBA_ROWS), :] = st
            m_ref[h, pl.ds(rows, MOBA_ROWS), :] = m_new
        return carry

    lax.fori_loop((j0 + C_BLOCK) // MOBA_ROWS, seq // MOBA_ROWS, chunk, 0)


def _moba(q, k, v):
    b, hp, s, _ = q.shape
    whole = pl.BlockSpec((None, None, s, LANES), lambda bi, h, j: (bi, h, 0, 0))
    return pl.pallas_call(
        _moba_kernel,
        grid=(b, hp, s // C_BLOCK),
        in_specs=[whole, whole, whole],
        out_specs=whole,
        out_shape=jax.ShapeDtypeStruct((b, hp, s, LANES), BF16),
        scratch_shapes=[pltpu.VMEM((2, s, LANES), BF16),
                        pltpu.VMEM((2, s, LANES), F32),
                        pltpu.VMEM((2, s, LANES), F32)],
        compiler_params=_params("parallel", "parallel", "arbitrary"),
        name="moba",
    )(q, k, v)


def _c_out_kernel(h_ref, o_ref, w_ref, out_ref, cat_ref):
    for hp in range(C_WIDTH // LANES):
        cat_ref[:, hp * LANES:(hp + 1) * LANES] = o_ref[hp]
    out_ref[...] = h_ref[...] + jnp.dot(cat_ref[...], w_ref[...], preferred_element_type=F32)


def _c_out(h, o, w_out):
    b, s, d = h.shape
    hp = C_WIDTH // LANES
    row = pl.BlockSpec((None, PROJ_ROWS, d), lambda bi, i: (bi, i, 0))
    return pl.pallas_call(
        _c_out_kernel,
        grid=(b, s // PROJ_ROWS),
        in_specs=[row, pl.BlockSpec((None, hp, PROJ_ROWS, LANES), lambda bi, i: (bi, 0, i, 0)),
                  _resident((C_WIDTH, d))],
        out_specs=row,
        out_shape=jax.ShapeDtypeStruct((b, s, d), F32),
        scratch_shapes=[pltpu.VMEM((PROJ_ROWS, C_WIDTH), BF16)],
        compiler_params=_params("parallel", "parallel"),
        name="c_out_proj",
    )(h, o, w_out)


def kernel(x, ffn1_norm, ffn1_w_gate, ffn1_w_up, ffn1_w_down, mix_norm, ffn2_norm, ffn2_w_gate, ffn2_w_up,
           ffn2_w_down, ab_w_in, ab_v_norm, ab_w_spatial, ab_b_spatial, ab_w_out, c_w_in, c_w_out, final_norm):
    b, s, d = x.shape
    assert d == D_MODEL and s % (A_PATTERNS[-1][0]) == 0 and s % MOBA_ROWS == 0 and (b * s) % FFN_ROWS == 0
    assert s // C_BLOCK <= HEAD_DIM
    bf = lambda w: w.astype(BF16)

    def ffn(h, layer, norm, wg, wu, wd, final_g=None):
        y = _ffn(h.reshape(b * s, d), norm[layer], bf(wg[layer]), bf(wu[layer]), bf(wd[layer]), final_g)
        return y.reshape(b, s, d)

    h = x
    h = ffn(h, 0, ffn1_norm, ffn1_w_gate, ffn1_w_up, ffn1_w_down)
    q, k, v, b_out = _ab_in(h, mix_norm[0], bf(ab_w_in[0]), ab_v_norm[0], ab_w_spatial[0], ab_b_spatial[0])
    outs, lses = zip(*[_swa(q, k, v, dil) for _, dil in A_PATTERNS])
    h = _ab_out(h, outs, lses, b_out, bf(ab_w_out[0]))
    h = ffn(h, 0, ffn2_norm, ffn2_w_gate, ffn2_w_up, ffn2_w_down)
    h = ffn(h, 1, ffn1_norm, ffn1_w_gate, ffn1_w_up, ffn1_w_down)
    qc, kc, vc = _c_in(h, mix_norm[1], bf(c_w_in[0]))
    h = _c_out(h, _moba(qc, kc, vc), bf(c_w_out[0]))
    h = ffn(h, 1, ffn2_norm, ffn2_w_gate, ffn2_w_up, ffn2_w_down, final_g=final_norm)
    return h
```

```python
import functools

import jax
import jax.numpy as jnp
from jax import lax
from jax.experimental import pallas as pl
from jax.experimental.pallas import tpu as pltpu

F32 = jnp.float32
BF16 = jnp.bfloat16

D_MODEL = 1024
D_FF = 2816
EPS = 1e-6
HEAD_DIM = 64
LANES = 128
A_HEADS = 8
A_PATTERNS = ((128, 1), (512, 4), (2048, 16))
A_WIDTH = A_HEADS * HEAD_DIM
B_GROUPS = 4
B_CHUNK = 128
B_WIDTH = B_GROUPS * LANES
C_HEADS = 16
C_WIDTH = C_HEADS * HEAD_DIM
C_BLOCK = 256
C_TOPK = 3
SCALE = HEAD_DIM ** -0.5
NEG_BIG = -1e30

M_INIT = -(2.0 ** 60)
SCORE_OFF = -(2.0 ** 100)

VMEM_LIMIT_BYTES = 56 * 1024 * 1024

FFN_ROWS = 1024
FFN_CHUNK = 256
PROJ_ROWS = 512
SWA_ROWS = 512
SWA_WIN = 128
MOBA_ROWS = 1024

CONTRACT_LAST = (((1,), (1,)), ((), ()))


def _params(*sem):
    return pltpu.CompilerParams(dimension_semantics=sem, vmem_limit_bytes=VMEM_LIMIT_BYTES)


def _rms_norm(x, g):
    return x * lax.rsqrt(jnp.mean(x * x, axis=-1, keepdims=True) + EPS) * g


def _resident(shape):
    nd = len(shape)
    return pl.BlockSpec(shape, lambda *_: (0,) * nd, pipeline_mode=pl.Buffered(1))


def _ffn_kernel(x_ref, g_ref, wg_ref, wu_ref, wd_ref, *rest, final):
    if final:
        gf_ref, o_ref, xn_ref, acc_ref = rest
    else:
        o_ref, xn_ref, acc_ref = rest
    xn_ref[...] = _rms_norm(x_ref[...], g_ref[...]).astype(BF16)
    acc_ref[...] = jnp.zeros_like(acc_ref)

    def body(c, carry):
        off = pl.multiple_of(c * FFN_CHUNK, FFN_CHUNK)
        xn = xn_ref[...]
        gate = jnp.dot(xn, wg_ref[:, pl.ds(off, FFN_CHUNK)], preferred_element_type=F32)
        up = jnp.dot(xn, wu_ref[:, pl.ds(off, FFN_CHUNK)], preferred_element_type=F32)
        act = (jax.nn.silu(gate) * up).astype(BF16)
        acc_ref[...] += jnp.dot(act, wd_ref[pl.ds(off, FFN_CHUNK), :], preferred_element_type=F32)
        return carry

    lax.fori_loop(0, D_FF // FFN_CHUNK, body, 0)
    y = x_ref[...] + 0.5 * acc_ref[...]
    if final:
        y = _rms_norm(y, gf_ref[...])
    o_ref[...] = y


def _ffn(x2, g, wg, wu, wd, final_g=None):
    t, d = x2.shape
    final = final_g is not None
    row_spec = pl.BlockSpec((FFN_ROWS, d), lambda i: (i, 0))
    in_specs = [row_spec, _resident((1, d)), _resident((d, D_FF)), _resident((d, D_FF)), _resident((D_FF, d))]
    args = [x2, g.reshape(1, d), wg, wu, wd]
    if final:
        in_specs.append(_resident((1, d)))
        args.append(final_g.reshape(1, d))
    return pl.pallas_call(
        functools.partial(_ffn_kernel, final=final),
        grid=(t // FFN_ROWS,),
        in_specs=in_specs,
        out_specs=row_spec,
        out_shape=jax.ShapeDtypeStruct((t, d), F32),
        scratch_shapes=[pltpu.VMEM((FFN_ROWS, d), BF16), pltpu.VMEM((FFN_ROWS, d), F32)],
        compiler_params=_params("parallel"),
        name="ffn_final" if final else "ffn",
    )(*args)


def _ab_in_kernel(h_ref, g_ref, w_ref, vn_ref, ws_ref, bs_ref, q_ref, k_ref, v_ref, b_ref):
    hn = _rms_norm(h_ref[...], g_ref[...]).astype(BF16)
    z = jnp.dot(hn, w_ref[...], preferred_element_type=F32)
    for hp in range(A_WIDTH // LANES):
        lo = hp * LANES
        q_ref[hp] = (z[:, lo:lo + LANES] * SCALE).astype(BF16)
        k_ref[hp] = z[:, A_WIDTH + lo:A_WIDTH + lo + LANES].astype(BF16)
        v_ref[hp] = z[:, 2 * A_WIDTH + lo:2 * A_WIDTH + lo + LANES].astype(BF16)
    row = lax.broadcasted_iota(jnp.int32, (B_CHUNK, B_CHUNK), 0)
    col = lax.broadcasted_iota(jnp.int32, (B_CHUNK, B_CHUNK), 1)
    causal = row >= col
    u_off, v_off = 3 * A_WIDTH, 3 * A_WIDTH + B_WIDTH
    for g in range(B_GROUPS):
        lo = g * LANES
        u = jax.nn.gelu(z[:, u_off + lo:u_off + lo + LANES])
        v = jax.nn.gelu(z[:, v_off + lo:v_off + lo + LANES])
        vn = _rms_norm(v, vn_ref[:, lo:lo + LANES]).astype(BF16)
        ws = jnp.where(causal, ws_ref[g], 0.0).astype(BF16)
        bias = bs_ref[:, g:g + 1]
        for c in range(PROJ_ROWS // B_CHUNK):
            r0 = c * B_CHUNK
            mixed = jnp.dot(ws, vn[r0:r0 + B_CHUNK], preferred_element_type=F32) + bias
            b_ref[r0:r0 + B_CHUNK, lo:lo + LANES] = (u[r0:r0 + B_CHUNK] * mixed).astype(BF16)


def _ab_in(h, g, w_in, v_norm, w_s, b_s):
    b, s, d = h.shape
    hp = A_WIDTH // LANES
    width = w_in.shape[1]
    qkv_spec = pl.BlockSpec((None, hp, PROJ_ROWS, LANES), lambda bi, i: (bi, 0, i, 0))
    qkv_shape = jax.ShapeDtypeStruct((b, hp, s, LANES), BF16)
    return pl.pallas_call(
        _ab_in_kernel,
        grid=(b, s // PROJ_ROWS),
        in_specs=[
            pl.BlockSpec((None, PROJ_ROWS, d), lambda bi, i: (bi, i, 0)),
            _resident((1, d)),
            _resident((d, width)),
            _resident((1, B_WIDTH)),
            _resident((B_GROUPS, B_CHUNK, B_CHUNK)),
            _resident((B_CHUNK, B_GROUPS)),
        ],
        out_specs=[qkv_spec, qkv_spec, qkv_spec,
                   pl.BlockSpec((None, PROJ_ROWS, B_WIDTH), lambda bi, i: (bi, i, 0))],
        out_shape=[qkv_shape, qkv_shape, qkv_shape, jax.ShapeDtypeStruct((b, s, B_WIDTH), BF16)],
        compiler_params=_params("parallel", "parallel"),
        name="ab_in_proj",
    )(h, g.reshape(1, d), w_in, v_norm.reshape(1, B_WIDTH), w_s, b_s.T)


def _swa_kernel(q_ref, kp_ref, kc_ref, vp_ref, vc_ref, o_ref, lse_ref, kk_ref, vv_ref):
    n = pl.program_id(3)
    kk_ref[0:SWA_WIN] = kp_ref[...]
    kk_ref[SWA_WIN:] = kc_ref[...]
    vv_ref[0:SWA_WIN] = vp_ref[...]
    vv_ref[SWA_WIN:] = vc_ref[...]
    lane = lax.broadcasted_iota(jnp.int32, (SWA_WIN, LANES), 1)
    head0 = lane < HEAD_DIM
    qi = lax.broadcasted_iota(jnp.int32, (SWA_WIN, 2 * SWA_WIN), 0)
    kc = lax.broadcasted_iota(jnp.int32, (SWA_WIN, 2 * SWA_WIN), 1)
    band = (kc >= qi) & (kc <= qi + SWA_WIN)
    for j in range(SWA_ROWS // SWA_WIN):
        r0 = j * SWA_WIN
        q = q_ref[r0:r0 + SWA_WIN]
        kk = kk_ref[r0:r0 + 2 * SWA_WIN]
        vv = vv_ref[r0:r0 + 2 * SWA_WIN]
        mask = band & ((kc >= SWA_WIN) | (n * SWA_ROWS + r0 > 0))
        outs, lses = [], []
        for hmask in (head0, ~head0):
            qm = jnp.where(hmask, q, jnp.zeros_like(q))
            s = lax.dot_general(qm, kk, CONTRACT_LAST, preferred_element_type=F32)
            s = jnp.where(mask, s, NEG_BIG)
            m = jnp.max(s, axis=-1, keepdims=True)
            p = jnp.exp(s - m)
            l = jnp.sum(p, axis=-1, keepdims=True)
            pv = jnp.dot(p.astype(BF16), vv, preferred_element_type=F32)
            outs.append(pv / l)
            lses.append(jnp.broadcast_to(m + jnp.log(l), (SWA_WIN, LANES)))
        o_ref[r0:r0 + SWA_WIN] = jnp.where(head0, outs[0], outs[1]).astype(BF16)
        lse_ref[r0:r0 + SWA_WIN] = jnp.where(head0, lses[0], lses[1])


def _swa(q, k, v, dilation):
    b, hp, s, _ = q.shape
    sub = s // dilation
    view = (b, hp, sub, dilation * LANES)
    q, k, v = q.reshape(view), k.reshape(view), v.reshape(view)
    per_step = SWA_ROWS // SWA_WIN
    cur = pl.BlockSpec((None, None, SWA_ROWS, LANES), lambda bi, h, r, n: (bi, h, n, r))
    prev = pl.BlockSpec((None, None, SWA_WIN, LANES),
                        lambda bi, h, r, n: (bi, h, jnp.maximum(n * per_step - 1, 0), r))
    o, lse = pl.pallas_call(
        _swa_kernel,
        grid=(b, hp, dilation, sub // SWA_ROWS),
        in_specs=[cur, prev, cur, prev, cur],
        out_specs=[cur, cur],
        out_shape=[jax.ShapeDtypeStruct(view, BF16), jax.ShapeDtypeStruct(view, F32)],
        scratch_shapes=[pltpu.VMEM((SWA_ROWS + SWA_WIN, LANES), BF16),
                        pltpu.VMEM((SWA_ROWS + SWA_WIN, LANES), BF16)],
        compiler_params=_params("parallel", "parallel", "parallel", "parallel"),
        name=f"swa_d{dilation}",
    )(q, k, k, v, v)
    return o.reshape(b, hp, s, LANES), lse.reshape(b, hp, s, LANES)


def _ab_out_kernel(h_ref, o1_ref, o2_ref, o3_ref, l1_ref, l2_ref, l3_ref, b_ref, w_ref, out_ref, cat_ref):
    for hp in range(A_WIDTH // LANES):
        l1, l2, l3 = l1_ref[hp], l2_ref[hp], l3_ref[hp]
        m = jnp.maximum(jnp.maximum(l1, l2), l3)
        e1, e2, e3 = jnp.exp(l1 - m), jnp.exp(l2 - m), jnp.exp(l3 - m)
        mix = (e1 * o1_ref[hp].astype(F32) + e2 * o2_ref[hp].astype(F32) + e3 * o3_ref[hp].astype(F32))
        cat_ref[:, hp * LANES:(hp + 1) * LANES] = (mix / (e1 + e2 + e3)).astype(BF16)
    cat_ref[:, A_WIDTH:] = b_ref[...]
    out_ref[...] = h_ref[...] + jnp.dot(cat_ref[...], w_ref[...], preferred_element_type=F32)


def _ab_out(h, outs, lses, b_out, w_out):
    b, s, d = h.shape
    hp = A_WIDTH // LANES
    row = pl.BlockSpec((None, PROJ_ROWS, d), lambda bi, i: (bi, i, 0))
    heads = pl.BlockSpec((None, hp, PROJ_ROWS, LANES), lambda bi, i: (bi, 0, i, 0))
    return pl.pallas_call(
        _ab_out_kernel,
        grid=(b, s // PROJ_ROWS),
        in_specs=[row] + [heads] * 6 + [pl.BlockSpec((None, PROJ_ROWS, B_WIDTH), lambda bi, i: (bi, i, 0)),
                                        _resident((A_WIDTH + B_WIDTH, d))],
        out_specs=row,
        out_shape=jax.ShapeDtypeStruct((b, s, d), F32),
        scratch_shapes=[pltpu.VMEM((PROJ_ROWS, A_WIDTH + B_WIDTH), BF16)],
        compiler_params=_params("parallel", "parallel"),
        name="ab_out_proj",
    )(h, *outs, *lses, b_out, w_out)


def _c_in_kernel(h_ref, g_ref, w_ref, q_ref, k_ref, v_ref):
    hn = _rms_norm(h_ref[...], g_ref[...]).astype(BF16)
    z = jnp.dot(hn, w_ref[...], preferred_element_type=F32)
    for hp in range(C_WIDTH // LANES):
        lo = hp * LANES
        q_ref[hp] = (z[:, lo:lo + LANES] * SCALE).astype(BF16)
        k_ref[hp] = z[:, C_WIDTH + lo:C_WIDTH + lo + LANES].astype(BF16)
        v_ref[hp] = z[:, 2 * C_WIDTH + lo:2 * C_WIDTH + lo + LANES].astype(BF16)


def _c_in(h, g, w_in):
    b, s, d = h.shape
    hp = C_WIDTH // LANES
    spec = pl.BlockSpec((None, hp, PROJ_ROWS, LANES), lambda bi, i: (bi, 0, i, 0))
    shape = jax.ShapeDtypeStruct((b, hp, s, LANES), BF16)
    return pl.pallas_call(
        _c_in_kernel,
        grid=(b, s // PROJ_ROWS),
        in_specs=[pl.BlockSpec((None, PROJ_ROWS, d), lambda bi, i: (bi, i, 0)),
                  _resident((1, d)), _resident((d, 3 * C_WIDTH))],
        out_specs=[spec, spec, spec],
        out_shape=[shape, shape, shape],
        compiler_params=_params("parallel", "parallel"),
        name="c_in_proj",
    )(h, g.reshape(1, d), w_in)


def _moba_select(q_ref, k_ref, qaug_ref, st_ref, m_ref, nb):
    seq = q_ref.shape[0]
    kmean = jnp.concatenate(
        [jnp.mean(k_ref[j * C_BLOCK:(j + 1) * C_BLOCK].astype(F32), axis=0, keepdims=True) for j in range(nb)],
        axis=0).astype(BF16)
    lane = lax.broadcasted_iota(jnp.int32, (MOBA_ROWS, LANES), 1)
    blk = lax.broadcasted_iota(jnp.int32, (nb, MOBA_ROWS), 0)
    qpos = lax.broadcasted_iota(jnp.int32, (nb, MOBA_ROWS), 1)
    zeros = lambda n: jnp.zeros((n, MOBA_ROWS), F32)

    def chunk(t, carry):
        r0 = pl.multiple_of(t * MOBA_ROWS, MOBA_ROWS)
        q = q_ref[pl.ds(r0, MOBA_ROWS), :]
        own = (qpos + r0) // C_BLOCK
        past = blk < own
        for h in range(2):
            qm = jnp.where(lane < HEAD_DIM if h == 0 else lane >= HEAD_DIM, q, jnp.zeros_like(q))
            gate = lax.dot_general(kmean, qm, CONTRACT_LAST, preferred_element_type=F32)
            gate = jnp.where(past, gate, -jnp.inf)
            chosen = jnp.zeros(gate.shape, jnp.bool_)
            for _ in range(C_TOPK):
                best = jnp.max(gate, axis=0, keepdims=True)
                first = jnp.min(jnp.where(gate == best, blk, nb), axis=0, keepdims=True)
                hit = blk == first
                chosen = chosen | hit
                gate = jnp.where(hit, -jnp.inf, gate)
            unselected = 1.0 - ((chosen & past) | (blk == own)).astype(F32)
            if h == 0:
                extra_t = jnp.concatenate([zeros(HEAD_DIM), unselected, zeros(HEAD_DIM - nb)], axis=0)
            else:
                extra_t = jnp.concatenate([unselected, zeros(LANES - nb)], axis=0)
            qaug_ref[h, pl.ds(r0, MOBA_ROWS), :] = (qm.astype(F32) + extra_t.T).astype(BF16)
        return carry

    lax.fori_loop(0, seq // MOBA_ROWS, chunk, 0)
    st_ref[...] = jnp.zeros_like(st_ref)
    m_ref[...] = jnp.full(m_ref.shape, M_INIT, F32)


def _moba_kernel(q_ref, k_ref, v_ref, o_ref, qaug_ref, st_ref, m_ref):
    j = pl.program_id(2)
    seq = q_ref.shape[0]
    nb = seq // C_BLOCK

    @pl.when(j == 0)
    def _():
        _moba_select(q_ref, k_ref, qaug_ref, st_ref, m_ref, nb)

    j0 = pl.multiple_of(j * MOBA_ROWS, MOBA_ROWS)
    kj = k_ref[pl.ds(j0, MOBA_ROWS), :]
    vj = v_ref[pl.ds(j0, MOBA_ROWS), :]
    lane = lax.broadcasted_iota(jnp.int32, (MOBA_ROWS, LANES), 1)
    key_blk = j * (MOBA_ROWS // C_BLOCK) + lax.broadcasted_iota(jnp.int32, (MOBA_ROWS, LANES), 0) // C_BLOCK
    in_head = [lane < HEAD_DIM, lane >= HEAD_DIM]
    flag_lane = [HEAD_DIM + key_blk, key_blk]
    k_aug = [jnp.where(in_head[h], kj, jnp.where(lane == flag_lane[h], SCORE_OFF, 0.0).astype(BF16))
             for h in range(2)]
    v_aug = [jnp.where(in_head[h], vj, jnp.ones_like(vj)) for h in range(2)]

    def update(h, rows, causal):
        qa = qaug_ref[h, pl.ds(rows, MOBA_ROWS), :]
        s = lax.dot_general(qa, k_aug[h], CONTRACT_LAST, preferred_element_type=F32)
        if causal:
            qi = lax.broadcasted_iota(jnp.int32, s.shape, 0)
            ki = lax.broadcasted_iota(jnp.int32, s.shape, 1)
            s = jnp.where(ki <= qi, s, SCORE_OFF)
        m_old = m_ref[h, pl.ds(rows, MOBA_ROWS), :]
        m_new = jnp.maximum(m_old, jnp.max(s, axis=-1, keepdims=True))
        p = jnp.exp(s - jnp.concatenate([m_new] * (MOBA_ROWS // LANES), axis=1))
        pv = jnp.dot(p.astype(BF16), v_aug[h], preferred_element_type=F32)
        st = jnp.exp(m_old - m_new) * st_ref[h, pl.ds(rows, MOBA_ROWS), :] + pv
        return st, m_new

    done = []
    for h in range(2):
        st, _ = update(h, j0, True)
        done.append(st / pltpu.roll(st, HEAD_DIM, 1))
    o_ref[pl.ds(j0, MOBA_ROWS), :] = jnp.where(in_head[0], done[0], done[1]).astype(BF16)

    def chunk(t, carry):
        rows = pl.multiple_of(t * MOBA_ROWS, MOBA_ROWS)
        for h in range(2):
            st, m_new = update(h, rows, False)
            st_ref[h, pl.ds(rows, MOBA_ROWS), :] = st
            m_ref[h, pl.ds(rows, MOBA_ROWS), :] = m_new
        return carry

    lax.fori_loop(j + 1, seq // MOBA_ROWS, chunk, 0)


def _moba(q, k, v):
    b, hp, s, _ = q.shape
    whole = pl.BlockSpec((None, None, s, LANES), lambda bi, h, j: (bi, h, 0, 0))
    whole_in = pl.BlockSpec((None, None, s, LANES), lambda bi, h, j: (bi, h, 0, 0), pipeline_mode=pl.Buffered(1))
    return pl.pallas_call(
        _moba_kernel,
        grid=(b, hp, s // MOBA_ROWS),
        in_specs=[whole_in, whole_in, whole_in],
        out_specs=whole,
        out_shape=jax.ShapeDtypeStruct((b, hp, s, LANES), BF16),
        scratch_shapes=[pltpu.VMEM((2, s, LANES), BF16),
                        pltpu.VMEM((2, s, LANES), F32),
                        pltpu.VMEM((2, s, LANES), F32)],
        compiler_params=_params("parallel", "parallel", "arbitrary"),
        name="moba",
    )(q, k, v)


def _c_out_kernel(h_ref, o_ref, w_ref, out_ref, cat_ref):
    for hp in range(C_WIDTH // LANES):
        cat_ref[:, hp * LANES:(hp + 1) * LANES] = o_ref[hp]
    out_ref[...] = h_ref[...] + jnp.dot(cat_ref[...], w_ref[...], preferred_element_type=F32)


def _c_out(h, o, w_out):
    b, s, d = h.shape
    hp = C_WIDTH // LANES
    row = pl.BlockSpec((None, PROJ_ROWS, d), lambda bi, i: (bi, i, 0))
    return pl.pallas_call(
        _c_out_kernel,
        grid=(b, s // PROJ_ROWS),
        in_specs=[row, pl.BlockSpec((None, hp, PROJ_ROWS, LANES), lambda bi, i: (bi, 0, i, 0)),
                  _resident((C_WIDTH, d))],
        out_specs=row,
        out_shape=jax.ShapeDtypeStruct((b, s, d), F32),
        scratch_shapes=[pltpu.VMEM((PROJ_ROWS, C_WIDTH), BF16)],
        compiler_params=_params("parallel", "parallel"),
        name="c_out_proj",
    )(h, o, w_out)


def kernel(x, ffn1_norm, ffn1_w_gate, ffn1_w_up, ffn1_w_down, mix_norm, ffn2_norm, ffn2_w_gate, ffn2_w_up,
           ffn2_w_down, ab_w_in, ab_v_norm, ab_w_spatial, ab_b_spatial, ab_w_out, c_w_in, c_w_out, final_norm):
    b, s, d = x.shape
    assert d == D_MODEL and s % (A_PATTERNS[-1][0]) == 0 and s % MOBA_ROWS == 0 and (b * s) % FFN_ROWS == 0
    assert s // C_BLOCK <= HEAD_DIM
    bf = lambda w: w.astype(BF16)

    def ffn(h, layer, norm, wg, wu, wd, final_g=None):
        y = _ffn(h.reshape(b * s, d), norm[layer], bf(wg[layer]), bf(wu[layer]), bf(wd[layer]), final_g)
        return y.reshape(b, s, d)

    h = x
    h = ffn(h, 0, ffn1_norm, ffn1_w_gate, ffn1_w_up, ffn1_w_down)
    q, k, v, b_out = _ab_in(h, mix_norm[0], bf(ab_w_in[0]), ab_v_norm[0], ab_w_spatial[0], ab_b_spatial[0])
    outs, lses = zip(*[_swa(q, k, v, dil) for _, dil in A_PATTERNS])
    h = _ab_out(h, outs, lses, b_out, bf(ab_w_out[0]))
    h = ffn(h, 0, ffn2_norm, ffn2_w_gate, ffn2_w_up, ffn2_w_down)
    h = ffn(h, 1, ffn1_norm, ffn1_w_gate, ffn1_w_up, ffn1_w_down)
    qc, kc, vc = _c_in(h, mix_norm[1], bf(c_w_in[0]))
    h = _c_out(h, _moba(qc, kc, vc), bf(c_w_out[0]))
    h = ffn(h, 1, ffn2_norm, ffn2_w_gate, ffn2_w_up, ffn2_w_down, final_g=final_norm)
    return h
```

```python
import functools

import jax
import jax.numpy as jnp
from jax import lax
from jax.experimental import pallas as pl
from jax.experimental.pallas import tpu as pltpu

F32 = jnp.float32
BF16 = jnp.bfloat16

D_MODEL = 1024
D_FF = 2816
EPS = 1e-6
HEAD_DIM = 64
LANES = 128
A_HEADS = 8
A_PATTERNS = ((128, 1), (512, 4), (2048, 16))
A_WIDTH = A_HEADS * HEAD_DIM
DILATIONS = tuple(d for _, d in A_PATTERNS)
B_GROUPS = 4
B_CHUNK = 128
B_WIDTH = B_GROUPS * LANES
C_HEADS = 16
C_WIDTH = C_HEADS * HEAD_DIM
C_BLOCK = 256
C_TOPK = 3
SCALE = HEAD_DIM ** -0.5
NEG_BIG = -1e30

M_INIT = -(2.0 ** 60)
SCORE_OFF = -(2.0 ** 100)

VMEM_LIMIT_BYTES = 56 * 1024 * 1024

FFN_ROWS = 1024
FFN_CHUNK = 256
PROJ_ROWS = 512
SWA_TOKENS = 2048
SWA_WIN = 128
MOBA_ROWS = 1024

CONTRACT_LAST = (((1,), (1,)), ((), ()))


def _params(*sem):
    return pltpu.CompilerParams(dimension_semantics=sem, vmem_limit_bytes=VMEM_LIMIT_BYTES)


def _rms_norm(x, g):
    return x * lax.rsqrt(jnp.mean(x * x, axis=-1, keepdims=True) + EPS) * g


def _resident(shape):
    nd = len(shape)
    return pl.BlockSpec(shape, lambda *_: (0,) * nd, pipeline_mode=pl.Buffered(1))


def _ffn_kernel(x_ref, g_ref, wg_ref, wu_ref, wd_ref, *rest, final):
    if final:
        gf_ref, o_ref, xn_ref, acc_ref = rest
    else:
        o_ref, xn_ref, acc_ref = rest
    xn_ref[...] = _rms_norm(x_ref[...], g_ref[...]).astype(BF16)
    acc_ref[...] = jnp.zeros_like(acc_ref)

    def body(c, carry):
        off = pl.multiple_of(c * FFN_CHUNK, FFN_CHUNK)
        xn = xn_ref[...]
        gate = jnp.dot(xn, wg_ref[:, pl.ds(off, FFN_CHUNK)], preferred_element_type=F32)
        up = jnp.dot(xn, wu_ref[:, pl.ds(off, FFN_CHUNK)], preferred_element_type=F32)
        act = (jax.nn.silu(gate) * up).astype(BF16)
        acc_ref[...] += jnp.dot(act, wd_ref[pl.ds(off, FFN_CHUNK), :], preferred_element_type=F32)
        return carry

    lax.fori_loop(0, D_FF // FFN_CHUNK, body, 0)
    y = x_ref[...] + 0.5 * acc_ref[...]
    if final:
        y = _rms_norm(y, gf_ref[...])
    o_ref[...] = y


def _ffn(x2, g, wg, wu, wd, final_g=None):
    t, d = x2.shape
    final = final_g is not None
    row_spec = pl.BlockSpec((FFN_ROWS, d), lambda i: (i, 0))
    in_specs = [row_spec, _resident((1, d)), _resident((d, D_FF)), _resident((d, D_FF)), _resident((D_FF, d))]
    args = [x2, g.reshape(1, d), wg, wu, wd]
    if final:
        in_specs.append(_resident((1, d)))
        args.append(final_g.reshape(1, d))
    return pl.pallas_call(
        functools.partial(_ffn_kernel, final=final),
        grid=(t // FFN_ROWS,),
        in_specs=in_specs,
        out_specs=row_spec,
        out_shape=jax.ShapeDtypeStruct((t, d), F32),
        scratch_shapes=[pltpu.VMEM((FFN_ROWS, d), BF16), pltpu.VMEM((FFN_ROWS, d), F32)],
        compiler_params=_params("parallel"),
        name="ffn_final" if final else "ffn",
    )(*args)


def _ab_in_kernel(h_ref, g_ref, w_ref, vn_ref, ws_ref, bs_ref, *rest):
    qkv_refs, b_ref, zs_ref = rest[:3 * len(DILATIONS)], rest[-2], rest[-1]
    n_hp = A_WIDTH // LANES
    hn = _rms_norm(h_ref[...], g_ref[...]).astype(BF16)
    z = jnp.dot(hn, w_ref[...], preferred_element_type=F32)
    for t in range(3):
        for hp in range(n_hp):
            col = t * A_WIDTH + hp * LANES
            x = z[:, col:col + LANES]
            zs_ref[t * n_hp + hp] = x * SCALE if t == 0 else x
    for di, d in enumerate(DILATIONS):
        rows = PROJ_ROWS // d
        for t in range(3):
            out_ref = qkv_refs[3 * di + t]
            for hp in range(n_hp):
                for r in range(d):
                    x = zs_ref[t * n_hp + hp, pl.ds(r, rows, stride=d), :] if d > 1 else zs_ref[t * n_hp + hp]
                    out_ref[hp, :, r * LANES:(r + 1) * LANES] = x.astype(BF16)
    row = lax.broadcasted_iota(jnp.int32, (B_CHUNK, B_CHUNK), 0)
    col = lax.broadcasted_iota(jnp.int32, (B_CHUNK, B_CHUNK), 1)
    causal = row >= col
    u_off, v_off = 3 * A_WIDTH, 3 * A_WIDTH + B_WIDTH
    for g in range(B_GROUPS):
        lo = g * LANES
        u = jax.nn.gelu(z[:, u_off + lo:u_off + lo + LANES])
        v = jax.nn.gelu(z[:, v_off + lo:v_off + lo + LANES])
        vn = _rms_norm(v, vn_ref[:, lo:lo + LANES]).astype(BF16)
        ws = jnp.where(causal, ws_ref[g], 0.0).astype(BF16)
        bias = bs_ref[:, g:g + 1]
        for c in range(PROJ_ROWS // B_CHUNK):
            r0 = c * B_CHUNK
            mixed = jnp.dot(ws, vn[r0:r0 + B_CHUNK], preferred_element_type=F32) + bias
            b_ref[r0:r0 + B_CHUNK, lo:lo + LANES] = (u[r0:r0 + B_CHUNK] * mixed).astype(BF16)


def _dilated_spec(hp, d):
    return pl.BlockSpec((None, hp, PROJ_ROWS // d, d * LANES), lambda bi, i: (bi, 0, i, 0))


def _ab_in(h, g, w_in, v_norm, w_s, b_s):
    b, s, d = h.shape
    hp = A_WIDTH // LANES
    width = w_in.shape[1]
    qkv_specs = [_dilated_spec(hp, dil) for dil in DILATIONS for _ in range(3)]
    qkv_shapes = [jax.ShapeDtypeStruct((b, hp, s // dil, dil * LANES), BF16) for dil in DILATIONS for _ in range(3)]
    outs = pl.pallas_call(
        _ab_in_kernel,
        grid=(b, s // PROJ_ROWS),
        in_specs=[
            pl.BlockSpec((None, PROJ_ROWS, d), lambda bi, i: (bi, i, 0)),
            _resident((1, d)),
            _resident((d, width)),
            _resident((1, B_WIDTH)),
            _resident((B_GROUPS, B_CHUNK, B_CHUNK)),
            _resident((B_CHUNK, B_GROUPS)),
        ],
        out_specs=qkv_specs + [pl.BlockSpec((None, PROJ_ROWS, B_WIDTH), lambda bi, i: (bi, i, 0))],
        out_shape=qkv_shapes + [jax.ShapeDtypeStruct((b, s, B_WIDTH), BF16)],
        scratch_shapes=[pltpu.VMEM((3 * hp, PROJ_ROWS, LANES), F32)],
        compiler_params=_params("parallel", "parallel"),
        name="ab_in_proj",
    )(h, g.reshape(1, d), w_in, v_norm.reshape(1, B_WIDTH), w_s, b_s.T)
    return [outs[3 * i:3 * i + 3] for i in range(len(DILATIONS))], outs[-1]


def _swa_kernel(q_ref, kp_ref, kc_ref, vp_ref, vc_ref, o_ref, lse_ref, kk_ref, vv_ref, *, rows, res):
    n = pl.program_id(3)
    nsub = rows // SWA_WIN
    kk_ref[0:SWA_WIN] = kp_ref[...]
    kk_ref[SWA_WIN:] = kc_ref[...]
    vv_ref[0:SWA_WIN] = vp_ref[...]
    vv_ref[SWA_WIN:] = vc_ref[...]
    cols = [slice(r * LANES, (r + 1) * LANES) for r in range(res)]
    q = jnp.concatenate([q_ref[:, c].reshape(nsub, SWA_WIN, LANES) for c in cols], axis=0)
    kwin = jnp.stack([kk_ref[j * SWA_WIN:(j + 2) * SWA_WIN, c] for c in cols for j in range(nsub)])
    vwin = jnp.stack([vv_ref[j * SWA_WIN:(j + 2) * SWA_WIN, c] for c in cols for j in range(nsub)])
    qi = lax.broadcasted_iota(jnp.int32, (SWA_WIN, 2 * SWA_WIN), 0)
    kc = lax.broadcasted_iota(jnp.int32, (SWA_WIN, 2 * SWA_WIN), 1)
    band = (kc >= qi) & (kc <= qi + SWA_WIN)
    bias = jnp.where(band, 0.0, NEG_BIG)
    bias_first = jnp.where(n == 0, jnp.where(band & (kc >= SWA_WIN), 0.0, NEG_BIG), bias)
    bias = jnp.stack([bias_first if j == 0 else bias for _ in cols for j in range(nsub)])
    q_head0 = lax.broadcasted_iota(jnp.int32, q.shape, 2) < HEAD_DIM
    zero = jnp.zeros_like(q)
    ms, ps, pvs = [], [], []
    for qm in (jnp.where(q_head0, q, zero), jnp.where(q_head0, zero, q)):
        s = jnp.einsum("bqd,bkd->bqk", qm, kwin, preferred_element_type=F32) + bias
        m = jnp.max(s, axis=-1, keepdims=True)
        p = jnp.exp(s - m).astype(BF16)
        ms.append(m)
        ps.append(p)
        pvs.append(jnp.einsum("bqk,bkd->bqd", p, vwin, preferred_element_type=F32))
    krow = lax.broadcasted_iota(jnp.int32, (4 * SWA_WIN, LANES), 0)
    lane = lax.broadcasted_iota(jnp.int32, (4 * SWA_WIN, LANES), 1)
    ones_sel = ((krow < 2 * SWA_WIN) == (lane < HEAD_DIM)).astype(BF16)
    den = jnp.einsum("bqk,kd->bqd", jnp.concatenate(ps, axis=-1), ones_sel, preferred_element_type=F32)
    out = jnp.where(q_head0, pvs[0], pvs[1]) / den
    lse = jnp.where(q_head0, ms[0], ms[1]) + jnp.log(den)
    for r, c in enumerate(cols):
        o_ref[:, c] = out[r * nsub:(r + 1) * nsub].reshape(rows, LANES).astype(BF16)
        lse_ref[:, c] = lse[r * nsub:(r + 1) * nsub].reshape(rows, LANES)


def _swa(q, k, v, dilation):
    b, hp, sub, _ = q.shape
    res = min(dilation, SWA_TOKENS // SWA_WIN)
    rows = min(sub, SWA_TOKENS // res)
    per_step = rows // SWA_WIN
    cur = pl.BlockSpec((None, None, rows, res * LANES), lambda bi, h, r, n: (bi, h, n, r))
    prev = pl.BlockSpec((None, None, SWA_WIN, res * LANES),
                        lambda bi, h, r, n: (bi, h, jnp.maximum(n * per_step - 1, 0), r))
    return pl.pallas_call(
        functools.partial(_swa_kernel, rows=rows, res=res),
        grid=(b, hp, dilation // res, sub // rows),
        in_specs=[cur, prev, cur, prev, cur],
        out_specs=[cur, cur],
        out_shape=[jax.ShapeDtypeStruct(q.shape, BF16), jax.ShapeDtypeStruct(q.shape, F32)],
        scratch_shapes=[pltpu.VMEM((rows + SWA_WIN, res * LANES), BF16),
                        pltpu.VMEM((rows + SWA_WIN, res * LANES), BF16)],
        compiler_params=_params("parallel", "parallel", "parallel", "parallel"),
        name=f"swa_d{dilation}",
    )(q, k, k, v, v)


def _ab_out_kernel(h_ref, *rest):
    n_pat = len(DILATIONS)
    o_refs, l_refs = rest[:n_pat], rest[n_pat:2 * n_pat]
    b_ref, w_ref, out_ref, cat_ref, tok_ref = rest[2 * n_pat:]
    for hp in range(A_WIDTH // LANES):
        outs, lses = [], []
        for i, d in enumerate(DILATIONS):
            vals = []
            for kind, ref in enumerate((o_refs[i], l_refs[i])):
                if d == 1:
                    vals.append(ref[hp].astype(F32))
                    continue
                slot = (hp * n_pat + i) * 2 + kind
                for r in range(d):
                    tok_ref[slot, pl.ds(r, PROJ_ROWS // d, stride=d), :] = (
                        ref[hp, :, r * LANES:(r + 1) * LANES].astype(F32))
                vals.append(tok_ref[slot])
            outs.append(vals[0])
            lses.append(vals[1])
        m = functools.reduce(jnp.maximum, lses)
        es = [jnp.exp(l - m) for l in lses]
        mix = sum(e * o for e, o in zip(es, outs)) / sum(es)
        cat_ref[:, hp * LANES:(hp + 1) * LANES] = mix.astype(BF16)
    cat_ref[:, A_WIDTH:] = b_ref[...]
    out_ref[...] = h_ref[...] + jnp.dot(cat_ref[...], w_ref[...], preferred_element_type=F32)


def _ab_out(h, outs, lses, b_out, w_out):
    b, s, d = h.shape
    hp = A_WIDTH // LANES
    row = pl.BlockSpec((None, PROJ_ROWS, d), lambda bi, i: (bi, i, 0))
    heads = [_dilated_spec(hp, dil) for dil in DILATIONS]
    return pl.pallas_call(
        _ab_out_kernel,
        grid=(b, s // PROJ_ROWS),
        in_specs=[row] + heads + heads + [pl.BlockSpec((None, PROJ_ROWS, B_WIDTH), lambda bi, i: (bi, i, 0)),
                                          _resident((A_WIDTH + B_WIDTH, d))],
        out_specs=row,
        out_shape=jax.ShapeDtypeStruct((b, s, d), F32),
        scratch_shapes=[pltpu.VMEM((PROJ_ROWS, A_WIDTH + B_WIDTH), BF16),
                        pltpu.VMEM((hp * len(DILATIONS) * 2, PROJ_ROWS, LANES), F32)],
        compiler_params=_params("parallel", "parallel"),
        name="ab_out_proj",
    )(h, *outs, *lses, b_out, w_out)


def _c_in_kernel(h_ref, g_ref, w_ref, q_ref, k_ref, v_ref):
    hn = _rms_norm(h_ref[...], g_ref[...]).astype(BF16)
    z = jnp.dot(hn, w_ref[...], preferred_element_type=F32)
    for hp in range(C_WIDTH // LANES):
        lo = hp * LANES
        q_ref[hp] = (z[:, lo:lo + LANES] * SCALE).astype(BF16)
        k_ref[hp] = z[:, C_WIDTH + lo:C_WIDTH + lo + LANES].astype(BF16)
        v_ref[hp] = z[:, 2 * C_WIDTH + lo:2 * C_WIDTH + lo + LANES].astype(BF16)


def _c_in(h, g, w_in):
    b, s, d = h.shape
    hp = C_WIDTH // LANES
    spec = pl.BlockSpec((None, hp, PROJ_ROWS, LANES), lambda bi, i: (bi, 0, i, 0))
    shape = jax.ShapeDtypeStruct((b, hp, s, LANES), BF16)
    return pl.pallas_call(
        _c_in_kernel,
        grid=(b, s // PROJ_ROWS),
        in_specs=[pl.BlockSpec((None, PROJ_ROWS, d), lambda bi, i: (bi, i, 0)),
                  _resident((1, d)), _resident((d, 3 * C_WIDTH))],
        out_specs=[spec, spec, spec],
        out_shape=[shape, shape, shape],
        compiler_params=_params("parallel", "parallel"),
        name="c_in_proj",
    )(h, g.reshape(1, d), w_in)


def _moba_select(q_ref, k_ref, qaug_ref, st_ref, m_ref, nb):
    seq = q_ref.shape[0]
    kmean = jnp.concatenate(
        [jnp.mean(k_ref[j * C_BLOCK:(j + 1) * C_BLOCK].astype(F32), axis=0, keepdims=True) for j in range(nb)],
        axis=0).astype(BF16)
    lane = lax.broadcasted_iota(jnp.int32, (MOBA_ROWS, LANES), 1)
    blk = lax.broadcasted_iota(jnp.int32, (nb, MOBA_ROWS), 0)
    qpos = lax.broadcasted_iota(jnp.int32, (nb, MOBA_ROWS), 1)
    zeros = lambda n: jnp.zeros((n, MOBA_ROWS), F32)

    def chunk(t, carry):
        r0 = pl.multiple_of(t * MOBA_ROWS, MOBA_ROWS)
        q = q_ref[pl.ds(r0, MOBA_ROWS), :]
        own = (qpos + r0) // C_BLOCK
        past = blk < own
        for h in range(2):
            qm = jnp.where(lane < HEAD_DIM if h == 0 else lane >= HEAD_DIM, q, jnp.zeros_like(q))
            gate = lax.dot_general(kmean, qm, CONTRACT_LAST, preferred_element_type=F32)
            gate = jnp.where(past, gate, -jnp.inf)
            chosen = jnp.zeros(gate.shape, jnp.bool_)
            for _ in range(C_TOPK):
                best = jnp.max(gate, axis=0, keepdims=True)
                first = jnp.min(jnp.where(gate == best, blk, nb), axis=0, keepdims=True)
                hit = blk == first
                chosen = chosen | hit
                gate = jnp.where(hit, -jnp.inf, gate)
            unselected = 1.0 - ((chosen & past) | (blk == own)).astype(F32)
            if h == 0:
                extra_t = jnp.concatenate([zeros(HEAD_DIM), unselected, zeros(HEAD_DIM - nb)], axis=0)
            else:
                extra_t = jnp.concatenate([unselected, zeros(LANES - nb)], axis=0)
            qaug_ref[h, pl.ds(r0, MOBA_ROWS), :] = (qm.astype(F32) + extra_t.T).astype(BF16)
        return carry

    lax.fori_loop(0, seq // MOBA_ROWS, chunk, 0)
    st_ref[...] = jnp.zeros_like(st_ref)
    m_ref[...] = jnp.full(m_ref.shape, M_INIT, F32)


def _moba_kernel(q_ref, k_ref, v_ref, o_ref, qaug_ref, st_ref, m_ref):
    j = pl.program_id(2)
    seq = q_ref.shape[0]
    nb = seq // C_BLOCK

    @pl.when(j == 0)
    def _():
        _moba_select(q_ref, k_ref, qaug_ref, st_ref, m_ref, nb)

    j0 = pl.multiple_of(j * MOBA_ROWS, MOBA_ROWS)
    kj = k_ref[pl.ds(j0, MOBA_ROWS), :]
    vj = v_ref[pl.ds(j0, MOBA_ROWS), :]
    lane = lax.broadcasted_iota(jnp.int32, (MOBA_ROWS, LANES), 1)
    key_blk = j * (MOBA_ROWS // C_BLOCK) + lax.broadcasted_iota(jnp.int32, (MOBA_ROWS, LANES), 0) // C_BLOCK
    in_head = [lane < HEAD_DIM, lane >= HEAD_DIM]
    flag_lane = [HEAD_DIM + key_blk, key_blk]
    k_aug = [jnp.where(in_head[h], kj, jnp.where(lane == flag_lane[h], SCORE_OFF, 0.0).astype(BF16))
             for h in range(2)]
    v_aug = [jnp.where(in_head[h], vj, jnp.ones_like(vj)) for h in range(2)]

    def update(h, rows, causal):
        qa = qaug_ref[h, pl.ds(rows, MOBA_ROWS), :]
        s = lax.dot_general(qa, k_aug[h], CONTRACT_LAST, preferred_element_type=F32)
        if causal:
            qi = lax.broadcasted_iota(jnp.int32, s.shape, 0)
            ki = lax.broadcasted_iota(jnp.int32, s.shape, 1)
            s = jnp.where(ki <= qi, s, SCORE_OFF)
        m_old = m_ref[h, pl.ds(rows, MOBA_ROWS), :]
        m_new = jnp.maximum(m_old, jnp.max(s, axis=-1, keepdims=True))
        p = jnp.exp(s - jnp.concatenate([m_new] * (MOBA_ROWS // LANES), axis=1))
        pv = jnp.dot(p.astype(BF16), v_aug[h], preferred_element_type=F32)
        st = jnp.exp(m_old - m_new) * st_ref[h, pl.ds(rows, MOBA_ROWS), :] + pv
        return st, m_new

    done = []
    for h in range(2):
        st, _ = update(h, j0, True)
        done.append(st / pltpu.roll(st, HEAD_DIM, 1))
    o_ref[pl.ds(j0, MOBA_ROWS), :] = jnp.where(in_head[0], done[0], done[1]).astype(BF16)

    def chunk(t, carry):
        rows = pl.multiple_of(t * MOBA_ROWS, MOBA_ROWS)
        for h in range(2):
            st, m_new = update(h, rows, False)
            st_ref[h, pl.ds(rows, MOBA_ROWS), :] = st
            m_ref[h, pl.ds(rows, MOBA_ROWS), :] = m_new
        return carry

    lax.fori_loop(j + 1, seq // MOBA_ROWS, chunk, 0)


def _moba(q, k, v):
    b, hp, s, _ = q.shape
    whole = pl.BlockSpec((None, None, s, LANES), lambda bi, h, j: (bi, h, 0, 0))
    whole_in = pl.BlockSpec((None, None, s, LANES), lambda bi, h, j: (bi, h, 0, 0), pipeline_mode=pl.Buffered(1))
    return pl.pallas_call(
        _moba_kernel,
        grid=(b, hp, s // MOBA_ROWS),
        in_specs=[whole_in, whole_in, whole_in],
        out_specs=whole,
        out_shape=jax.ShapeDtypeStruct((b, hp, s, LANES), BF16),
        scratch_shapes=[pltpu.VMEM((2, s, LANES), BF16),
                        pltpu.VMEM((2, s, LANES), F32),
                        pltpu.VMEM((2, s, LANES), F32)],
        compiler_params=_params("parallel", "parallel", "arbitrary"),
        name="moba",
    )(q, k, v)


def _c_out_kernel(h_ref, o_ref, w_ref, out_ref, cat_ref):
    for hp in range(C_WIDTH // LANES):
        cat_ref[:, hp * LANES:(hp + 1) * LANES] = o_ref[hp]
    out_ref[...] = h_ref[...] + jnp.dot(cat_ref[...], w_ref[...], preferred_element_type=F32)


def _c_out(h, o, w_out):
    b, s, d = h.shape
    hp = C_WIDTH // LANES
    row = pl.BlockSpec((None, PROJ_ROWS, d), lambda bi, i: (bi, i, 0))
    return pl.pallas_call(
        _c_out_kernel,
        grid=(b, s // PROJ_ROWS),
        in_specs=[row, pl.BlockSpec((None, hp, PROJ_ROWS, LANES), lambda bi, i: (bi, 0, i, 0)),
                  _resident((C_WIDTH, d))],
        out_specs=row,
        out_shape=jax.ShapeDtypeStruct((b, s, d), F32),
        scratch_shapes=[pltpu.VMEM((PROJ_ROWS, C_WIDTH), BF16)],
        compiler_params=_params("parallel", "parallel"),
        name="c_out_proj",
    )(h, o, w_out)


def kernel(x, ffn1_norm, ffn1_w_gate, ffn1_w_up, ffn1_w_down, mix_norm, ffn2_norm, ffn2_w_gate, ffn2_w_up,
           ffn2_w_down, ab_w_in, ab_v_norm, ab_w_spatial, ab_b_spatial, ab_w_out, c_w_in, c_w_out, final_norm):
    b, s, d = x.shape
    assert d == D_MODEL and s % SWA_TOKENS == 0 and s % MOBA_ROWS == 0 and (b * s) % FFN_ROWS == 0
    assert s // C_BLOCK <= HEAD_DIM
    bf = lambda w: w.astype(BF16)

    def ffn(h, layer, norm, wg, wu, wd, final_g=None):
        y = _ffn(h.reshape(b * s, d), norm[layer], bf(wg[layer]), bf(wu[layer]), bf(wd[layer]), final_g)
        return y.reshape(b, s, d)

    h = x
    h = ffn(h, 0, ffn1_norm, ffn1_w_gate, ffn1_w_up, ffn1_w_down)
    qkvs, b_out = _ab_in(h, mix_norm[0], bf(ab_w_in[0]), ab_v_norm[0], ab_w_spatial[0], ab_b_spatial[0])
    outs, lses = zip(*[_swa(*qkv, dil) for qkv, dil in zip(qkvs, DILATIONS)])
    h = _ab_out(h, outs, lses, b_out, bf(ab_w_out[0]))
    h = ffn(h, 0, ffn2_norm, ffn2_w_gate, ffn2_w_up, ffn2_w_down)
    h = ffn(h, 1, ffn1_norm, ffn1_w_gate, ffn1_w_up, ffn1_w_down)
    qc, kc, vc = _c_in(h, mix_norm[1], bf(c_w_in[0]))
    h = _c_out(h, _moba(qc, kc, vc), bf(c_w_out[0]))
    h = ffn(h, 1, ffn2_norm, ffn2_w_gate, ffn2_w_up, ffn2_w_down, final_g=final_norm)
    return h
```

```python
import functools

import jax
import jax.numpy as jnp
from jax import lax
from jax.experimental import pallas as pl
from jax.experimental.pallas import tpu as pltpu

F32 = jnp.float32
BF16 = jnp.bfloat16

D_MODEL = 1024
D_FF = 2816
EPS = 1e-6
HEAD_DIM = 64
LANES = 128
A_HEADS = 8
A_PATTERNS = ((128, 1), (512, 4), (2048, 16))
A_WIDTH = A_HEADS * HEAD_DIM
DILATIONS = tuple(d for _, d in A_PATTERNS)
B_GROUPS = 4
B_CHUNK = 128
B_WIDTH = B_GROUPS * LANES
C_HEADS = 16
C_WIDTH = C_HEADS * HEAD_DIM
C_BLOCK = 256
C_TOPK = 3
SCALE = HEAD_DIM ** -0.5
NEG_BIG = -1e30

M_INIT = -(2.0 ** 60)
SCORE_OFF = -(2.0 ** 100)

VMEM_LIMIT_BYTES = 56 * 1024 * 1024

FFN_ROWS = 1024
FFN_CHUNK = 256
PROJ_ROWS = 512
SWA_TOKENS = 2048
SWA_WIN = 128
MOBA_ROWS = 1024

CONTRACT_LAST = (((1,), (1,)), ((), ()))


def _params(*sem):
    return pltpu.CompilerParams(dimension_semantics=sem, vmem_limit_bytes=VMEM_LIMIT_BYTES)


def _rms_norm(x, g):
    return x * lax.rsqrt(jnp.mean(x * x, axis=-1, keepdims=True) + EPS) * g


def _resident(shape):
    nd = len(shape)
    return pl.BlockSpec(shape, lambda *_: (0,) * nd, pipeline_mode=pl.Buffered(1))


def _ffn_kernel(x_ref, g_ref, wg_ref, wu_ref, wd_ref, *rest, final):
    if final:
        gf_ref, o_ref, xn_ref, acc_ref = rest
    else:
        o_ref, xn_ref, acc_ref = rest
    xn_ref[...] = _rms_norm(x_ref[...], g_ref[...]).astype(BF16)
    acc_ref[...] = jnp.zeros_like(acc_ref)

    def body(c, carry):
        off = pl.multiple_of(c * FFN_CHUNK, FFN_CHUNK)
        xn = xn_ref[...]
        gate = jnp.dot(xn, wg_ref[:, pl.ds(off, FFN_CHUNK)], preferred_element_type=F32)
        up = jnp.dot(xn, wu_ref[:, pl.ds(off, FFN_CHUNK)], preferred_element_type=F32)
        act = (jax.nn.silu(gate) * up).astype(BF16)
        acc_ref[...] += jnp.dot(act, wd_ref[pl.ds(off, FFN_CHUNK), :], preferred_element_type=F32)
        return carry

    lax.fori_loop(0, D_FF // FFN_CHUNK, body, 0, unroll=True)
    y = x_ref[...] + 0.5 * acc_ref[...]
    if final:
        y = _rms_norm(y, gf_ref[...])
    o_ref[...] = y


def _ffn(x2, g, wg, wu, wd, final_g=None):
    t, d = x2.shape
    final = final_g is not None
    row_spec = pl.BlockSpec((FFN_ROWS, d), lambda i: (i, 0))
    in_specs = [row_spec, _resident((1, d)), _resident((d, D_FF)), _resident((d, D_FF)), _resident((D_FF, d))]
    args = [x2, g.reshape(1, d), wg, wu, wd]
    if final:
        in_specs.append(_resident((1, d)))
        args.append(final_g.reshape(1, d))
    return pl.pallas_call(
        functools.partial(_ffn_kernel, final=final),
        grid=(t // FFN_ROWS,),
        in_specs=in_specs,
        out_specs=row_spec,
        out_shape=jax.ShapeDtypeStruct((t, d), F32),
        scratch_shapes=[pltpu.VMEM((FFN_ROWS, d), BF16), pltpu.VMEM((FFN_ROWS, d), F32)],
        compiler_params=_params("parallel"),
        name="ffn_final" if final else "ffn",
    )(*args)


def _ab_in_kernel(h_ref, g_ref, w_ref, vn_ref, ws_ref, bs_ref, *rest):
    qkv_refs, b_ref, z1_ref, z4_ref = rest[:3 * len(DILATIONS)], rest[-3], rest[-2], rest[-1]
    n_hp = A_WIDTH // LANES
    hn = _rms_norm(h_ref[...], g_ref[...]).astype(BF16)

    def project(col, width):
        return jnp.dot(hn, w_ref[:, col:col + width], preferred_element_type=F32)

    for t in range(3):
        z = project(t * A_WIDTH, A_WIDTH)
        if t == 0:
            z = z * SCALE
        for hp in range(n_hp):
            x1 = z[:, hp * LANES:(hp + 1) * LANES]
            qkv_refs[t][hp] = x1.astype(BF16)
            z1_ref[t * n_hp + hp] = x1
        for hp in range(n_hp):
            for r1 in range(4):
                x4 = z1_ref[t * n_hp + hp, pl.ds(r1, PROJ_ROWS // 4, stride=4), :]
                qkv_refs[3 + t][hp, :, r1 * LANES:(r1 + 1) * LANES] = x4.astype(BF16)
                z4_ref[(t * n_hp + hp) * 4 + r1] = x4
        for hp in range(n_hp):
            for r1 in range(4):
                for r2 in range(4):
                    x16 = z4_ref[(t * n_hp + hp) * 4 + r1, pl.ds(r2, PROJ_ROWS // 16, stride=4), :]
                    r = 4 * r2 + r1
                    qkv_refs[6 + t][hp, :, r * LANES:(r + 1) * LANES] = x16.astype(BF16)
    row = lax.broadcasted_iota(jnp.int32, (B_CHUNK, B_CHUNK), 0)
    col = lax.broadcasted_iota(jnp.int32, (B_CHUNK, B_CHUNK), 1)
    causal = row >= col
    zu = project(3 * A_WIDTH, B_WIDTH)
    zv = project(3 * A_WIDTH + B_WIDTH, B_WIDTH)
    for g in range(B_GROUPS):
        lo = g * LANES
        u = jax.nn.gelu(zu[:, lo:lo + LANES])
        v = jax.nn.gelu(zv[:, lo:lo + LANES])
        vn = _rms_norm(v, vn_ref[:, lo:lo + LANES]).astype(BF16)
        ws = jnp.where(causal, ws_ref[g], 0.0).astype(BF16)
        bias = bs_ref[:, g:g + 1]
        for c in range(PROJ_ROWS // B_CHUNK):
            r0 = c * B_CHUNK
            mixed = jnp.dot(ws, vn[r0:r0 + B_CHUNK], preferred_element_type=F32) + bias
            b_ref[r0:r0 + B_CHUNK, lo:lo + LANES] = (u[r0:r0 + B_CHUNK] * mixed).astype(BF16)


def _dilated_spec(hp, d):
    return pl.BlockSpec((None, hp, PROJ_ROWS // d, d * LANES), lambda bi, i: (bi, 0, i, 0))


def _ab_in(h, g, w_in, v_norm, w_s, b_s):
    b, s, d = h.shape
    hp = A_WIDTH // LANES
    width = w_in.shape[1]
    qkv_specs = [_dilated_spec(hp, dil) for dil in DILATIONS for _ in range(3)]
    qkv_shapes = [jax.ShapeDtypeStruct((b, hp, s // dil, dil * LANES), BF16) for dil in DILATIONS for _ in range(3)]
    outs = pl.pallas_call(
        _ab_in_kernel,
        grid=(b, s // PROJ_ROWS),
        in_specs=[
            pl.BlockSpec((None, PROJ_ROWS, d), lambda bi, i: (bi, i, 0)),
            _resident((1, d)),
            _resident((d, width)),
            _resident((1, B_WIDTH)),
            _resident((B_GROUPS, B_CHUNK, B_CHUNK)),
            _resident((B_CHUNK, B_GROUPS)),
        ],
        out_specs=qkv_specs + [pl.BlockSpec((None, PROJ_ROWS, B_WIDTH), lambda bi, i: (bi, i, 0))],
        out_shape=qkv_shapes + [jax.ShapeDtypeStruct((b, s, B_WIDTH), BF16)],
        scratch_shapes=[pltpu.VMEM((3 * hp, PROJ_ROWS, LANES), F32),
                        pltpu.VMEM((3 * hp * 4, PROJ_ROWS // 4, LANES), F32)],
        compiler_params=_params("parallel", "parallel"),
        name="ab_in_proj",
    )(h, g.reshape(1, d), w_in, v_norm.reshape(1, B_WIDTH), w_s, b_s.T)
    return [outs[3 * i:3 * i + 3] for i in range(len(DILATIONS))], outs[-1]


def _swa_kernel(q_ref, kp_ref, kc_ref, vp_ref, vc_ref, o_ref, lse_ref, kk_ref, vv_ref, *, rows, res):
    n = pl.program_id(3)
    nsub = rows // SWA_WIN
    kk_ref[0:SWA_WIN] = kp_ref[...]
    kk_ref[SWA_WIN:] = kc_ref[...]
    vv_ref[0:SWA_WIN] = vp_ref[...]
    vv_ref[SWA_WIN:] = vc_ref[...]
    cols = [slice(r * LANES, (r + 1) * LANES) for r in range(res)]
    q = jnp.concatenate([q_ref[:, c].reshape(nsub, SWA_WIN, LANES) for c in cols], axis=0)
    kwin = jnp.stack([kk_ref[j * SWA_WIN:(j + 2) * SWA_WIN, c] for c in cols for j in range(nsub)])
    vwin = jnp.stack([vv_ref[j * SWA_WIN:(j + 2) * SWA_WIN, c] for c in cols for j in range(nsub)])
    qi = lax.broadcasted_iota(jnp.int32, (SWA_WIN, 2 * SWA_WIN), 0)
    kc = lax.broadcasted_iota(jnp.int32, (SWA_WIN, 2 * SWA_WIN), 1)
    band = (kc >= qi) & (kc <= qi + SWA_WIN)
    bias = jnp.where(band, 0.0, NEG_BIG)
    bias_first = jnp.where(n == 0, jnp.where(band & (kc >= SWA_WIN), 0.0, NEG_BIG), bias)
    bias = jnp.stack([bias_first if j == 0 else bias for _ in cols for j in range(nsub)])
    q_head0 = lax.broadcasted_iota(jnp.int32, q.shape, 2) < HEAD_DIM
    zero = jnp.zeros_like(q)
    ms, ps, pvs = [], [], []
    for qm in (jnp.where(q_head0, q, zero), jnp.where(q_head0, zero, q)):
        s = jnp.einsum("bqd,bkd->bqk", qm, kwin, preferred_element_type=F32) + bias
        m = jnp.max(s, axis=-1, keepdims=True)
        p = jnp.exp(s - m).astype(BF16)
        ms.append(m)
        ps.append(p)
        pvs.append(jnp.einsum("bqk,bkd->bqd", p, vwin, preferred_element_type=F32))
    krow = lax.broadcasted_iota(jnp.int32, (4 * SWA_WIN, LANES), 0)
    lane = lax.broadcasted_iota(jnp.int32, (4 * SWA_WIN, LANES), 1)
    ones_sel = ((krow < 2 * SWA_WIN) == (lane < HEAD_DIM)).astype(BF16)
    den = jnp.einsum("bqk,kd->bqd", jnp.concatenate(ps, axis=-1), ones_sel, preferred_element_type=F32)
    out = jnp.where(q_head0, pvs[0], pvs[1]) / den
    lse = jnp.where(q_head0, ms[0], ms[1]) + jnp.log(den)
    for r, c in enumerate(cols):
        o_ref[:, c] = out[r * nsub:(r + 1) * nsub].reshape(rows, LANES).astype(BF16)
        lse_ref[:, c] = lse[r * nsub:(r + 1) * nsub].reshape(rows, LANES)


def _swa(q, k, v, dilation):
    b, hp, sub, _ = q.shape
    res = min(dilation, SWA_TOKENS // SWA_WIN)
    rows = min(sub, SWA_TOKENS // res)
    per_step = rows // SWA_WIN
    cur = pl.BlockSpec((None, None, rows, res * LANES), lambda bi, h, r, n: (bi, h, n, r))
    prev = pl.BlockSpec((None, None, SWA_WIN, res * LANES),
                        lambda bi, h, r, n: (bi, h, jnp.maximum(n * per_step - 1, 0), r))
    return pl.pallas_call(
        functools.partial(_swa_kernel, rows=rows, res=res),
        grid=(b, hp, dilation // res, sub // rows),
        in_specs=[cur, prev, cur, prev, cur],
        out_specs=[cur, cur],
        out_shape=[jax.ShapeDtypeStruct(q.shape, BF16), jax.ShapeDtypeStruct(q.shape, F32)],
        scratch_shapes=[pltpu.VMEM((rows + SWA_WIN, res * LANES), BF16),
                        pltpu.VMEM((rows + SWA_WIN, res * LANES), BF16)],
        compiler_params=_params("parallel", "parallel", "parallel", "parallel"),
        name=f"swa_d{dilation}",
    )(q, k, k, v, v)


def _ab_out_kernel(h_ref, *rest):
    n_pat = len(DILATIONS)
    o_refs, l_refs = rest[:n_pat], rest[n_pat:2 * n_pat]
    b_ref, w_ref, out_ref, cat_ref, tok_ref = rest[2 * n_pat:]
    for hp in range(A_WIDTH // LANES):
        outs, lses = [], []
        for i, d in enumerate(DILATIONS):
            vals = []
            for kind, ref in enumerate((o_refs[i], l_refs[i])):
                if d == 1:
                    vals.append(ref[hp].astype(F32))
                    continue
                slot = (hp * n_pat + i) * 2 + kind
                for r in range(d):
                    tok_ref[slot, pl.ds(r, PROJ_ROWS // d, stride=d), :] = (
                        ref[hp, :, r * LANES:(r + 1) * LANES].astype(F32))
                vals.append(tok_ref[slot])
            outs.append(vals[0])
            lses.append(vals[1])
        m = functools.reduce(jnp.maximum, lses)
        es = [jnp.exp(l - m) for l in lses]
        mix = sum(e * o for e, o in zip(es, outs)) / sum(es)
        cat_ref[:, hp * LANES:(hp + 1) * LANES] = mix.astype(BF16)
    cat_ref[:, A_WIDTH:] = b_ref[...]
    out_ref[...] = h_ref[...] + jnp.dot(cat_ref[...], w_ref[...], preferred_element_type=F32)


def _ab_out(h, outs, lses, b_out, w_out):
    b, s, d = h.shape
    hp = A_WIDTH // LANES
    row = pl.BlockSpec((None, PROJ_ROWS, d), lambda bi, i: (bi, i, 0))
    heads = [_dilated_spec(hp, dil) for dil in DILATIONS]
    return pl.pallas_call(
        _ab_out_kernel,
        grid=(b, s // PROJ_ROWS),
        in_specs=[row] + heads + heads + [pl.BlockSpec((None, PROJ_ROWS, B_WIDTH), lambda bi, i: (bi, i, 0)),
                                          _resident((A_WIDTH + B_WIDTH, d))],
        out_specs=row,
        out_shape=jax.ShapeDtypeStruct((b, s, d), F32),
        scratch_shapes=[pltpu.VMEM((PROJ_ROWS, A_WIDTH + B_WIDTH), BF16),
                        pltpu.VMEM((hp * len(DILATIONS) * 2, PROJ_ROWS, LANES), F32)],
        compiler_params=_params("parallel", "parallel"),
        name="ab_out_proj",
    )(h, *outs, *lses, b_out, w_out)


def _c_in_kernel(h_ref, g_ref, w_ref, q_ref, k_ref, v_ref):
    hn = _rms_norm(h_ref[...], g_ref[...]).astype(BF16)
    z = jnp.dot(hn, w_ref[...], preferred_element_type=F32)
    for hp in range(C_WIDTH // LANES):
        lo = hp * LANES
        q_ref[hp] = (z[:, lo:lo + LANES] * SCALE).astype(BF16)
        k_ref[hp] = z[:, C_WIDTH + lo:C_WIDTH + lo + LANES].astype(BF16)
        v_ref[hp] = z[:, 2 * C_WIDTH + lo:2 * C_WIDTH + lo + LANES].astype(BF16)


def _c_in(h, g, w_in):
    b, s, d = h.shape
    hp = C_WIDTH // LANES
    spec = pl.BlockSpec((None, hp, PROJ_ROWS, LANES), lambda bi, i: (bi, 0, i, 0))
    shape = jax.ShapeDtypeStruct((b, hp, s, LANES), BF16)
    return pl.pallas_call(
        _c_in_kernel,
        grid=(b, s // PROJ_ROWS),
        in_specs=[pl.BlockSpec((None, PROJ_ROWS, d), lambda bi, i: (bi, i, 0)),
                  _resident((1, d)), _resident((d, 3 * C_WIDTH))],
        out_specs=[spec, spec, spec],
        out_shape=[shape, shape, shape],
        compiler_params=_params("parallel", "parallel"),
        name="c_in_proj",
    )(h, g.reshape(1, d), w_in)


def _moba_select(q_ref, k_ref, qaug_ref, st_ref, m_ref, nb):
    seq = q_ref.shape[0]
    kmean = jnp.concatenate(
        [jnp.mean(k_ref[j * C_BLOCK:(j + 1) * C_BLOCK].astype(F32), axis=0, keepdims=True) for j in range(nb)],
        axis=0).astype(BF16)
    lane = lax.broadcasted_iota(jnp.int32, (MOBA_ROWS, LANES), 1)
    blk = lax.broadcasted_iota(jnp.int32, (nb, MOBA_ROWS), 0)
    qpos = lax.broadcasted_iota(jnp.int32, (nb, MOBA_ROWS), 1)
    zeros = lambda n: jnp.zeros((n, MOBA_ROWS), F32)

    def chunk(t, carry):
        r0 = pl.multiple_of(t * MOBA_ROWS, MOBA_ROWS)
        q = q_ref[pl.ds(r0, MOBA_ROWS), :]
        own = (qpos + r0) // C_BLOCK
        past = blk < own
        for h in range(2):
            qm = jnp.where(lane < HEAD_DIM if h == 0 else lane >= HEAD_DIM, q, jnp.zeros_like(q))
            gate = lax.dot_general(kmean, qm, CONTRACT_LAST, preferred_element_type=F32)
            gate = jnp.where(past, gate, -jnp.inf)
            chosen = jnp.zeros(gate.shape, jnp.bool_)
            for _ in range(C_TOPK):
                best = jnp.max(gate, axis=0, keepdims=True)
                first = jnp.min(jnp.where(gate == best, blk, nb), axis=0, keepdims=True)
                hit = blk == first
                chosen = chosen | hit
                gate = jnp.where(hit, -jnp.inf, gate)
            unselected = 1.0 - ((chosen & past) | (blk == own)).astype(F32)
            if h == 0:
                extra_t = jnp.concatenate([zeros(HEAD_DIM), unselected, zeros(HEAD_DIM - nb)], axis=0)
            else:
                extra_t = jnp.concatenate([unselected, zeros(LANES - nb)], axis=0)
            qaug_ref[h, pl.ds(r0, MOBA_ROWS), :] = (qm.astype(F32) + extra_t.T).astype(BF16)
        return carry

    lax.fori_loop(0, seq // MOBA_ROWS, chunk, 0)
    st_ref[...] = jnp.zeros_like(st_ref)
    m_ref[...] = jnp.full(m_ref.shape, M_INIT, F32)


def _moba_kernel(q_ref, k_ref, v_ref, o_ref, qaug_ref, st_ref, m_ref):
    j = pl.program_id(2)
    seq = q_ref.shape[0]
    nb = seq // C_BLOCK

    @pl.when(j == 0)
    def _():
        _moba_select(q_ref, k_ref, qaug_ref, st_ref, m_ref, nb)

    j0 = pl.multiple_of(j * MOBA_ROWS, MOBA_ROWS)
    kj = k_ref[pl.ds(j0, MOBA_ROWS), :]
    vj = v_ref[pl.ds(j0, MOBA_ROWS), :]
    lane = lax.broadcasted_iota(jnp.int32, (MOBA_ROWS, LANES), 1)
    key_blk = j * (MOBA_ROWS // C_BLOCK) + lax.broadcasted_iota(jnp.int32, (MOBA_ROWS, LANES), 0) // C_BLOCK
    in_head = [lane < HEAD_DIM, lane >= HEAD_DIM]
    flag_lane = [HEAD_DIM + key_blk, key_blk]
    k_aug = [jnp.where(in_head[h], kj, jnp.where(lane == flag_lane[h], SCORE_OFF, 0.0).astype(BF16))
             for h in range(2)]
    v_aug = [jnp.where(in_head[h], vj, jnp.ones_like(vj)) for h in range(2)]

    def update(h, rows, causal):
        qa = qaug_ref[h, pl.ds(rows, MOBA_ROWS), :]
        s = lax.dot_general(qa, k_aug[h], CONTRACT_LAST, preferred_element_type=F32)
        if causal:
            qi = lax.broadcasted_iota(jnp.int32, s.shape, 0)
            ki = lax.broadcasted_iota(jnp.int32, s.shape, 1)
            s = jnp.where(ki <= qi, s, SCORE_OFF)
        m_old = m_ref[h, pl.ds(rows, MOBA_ROWS), :]
        m_new = jnp.maximum(m_old, jnp.max(s, axis=-1, keepdims=True))
        p = jnp.exp(s - jnp.concatenate([m_new] * (MOBA_ROWS // LANES), axis=1))
        pv = jnp.dot(p.astype(BF16), v_aug[h], preferred_element_type=F32)
        st = jnp.exp(m_old - m_new) * st_ref[h, pl.ds(rows, MOBA_ROWS), :] + pv
        return st, m_new

    done = []
    for h in range(2):
        st, _ = update(h, j0, True)
        done.append(st / pltpu.roll(st, HEAD_DIM, 1))
    o_ref[pl.ds(j0, MOBA_ROWS), :] = jnp.where(in_head[0], done[0], done[1]).astype(BF16)

    def chunk(t, carry):
        rows = pl.multiple_of(t * MOBA_ROWS, MOBA_ROWS)
        for h in range(2):
            st, m_new = update(h, rows, False)
            st_ref[h, pl.ds(rows, MOBA_ROWS), :] = st
            m_ref[h, pl.ds(rows, MOBA_ROWS), :] = m_new
        return carry

    lax.fori_loop(j + 1, seq // MOBA_ROWS, chunk, 0)


def _moba(q, k, v):
    b, hp, s, _ = q.shape
    whole = pl.BlockSpec((None, None, s, LANES), lambda bi, h, j: (bi, h, 0, 0))
    whole_in = pl.BlockSpec((None, None, s, LANES), lambda bi, h, j: (bi, h, 0, 0), pipeline_mode=pl.Buffered(1))
    return pl.pallas_call(
        _moba_kernel,
        grid=(b, hp, s // MOBA_ROWS),
        in_specs=[whole_in, whole_in, whole_in],
        out_specs=whole,
        out_shape=jax.ShapeDtypeStruct((b, hp, s, LANES), BF16),
        scratch_shapes=[pltpu.VMEM((2, s, LANES), BF16),
                        pltpu.VMEM((2, s, LANES), F32),
                        pltpu.VMEM((2, s, LANES), F32)],
        compiler_params=_params("parallel", "parallel", "arbitrary"),
        name="moba",
    )(q, k, v)


def _c_out_kernel(h_ref, o_ref, w_ref, out_ref, cat_ref):
    for hp in range(C_WIDTH // LANES):
        cat_ref[:, hp * LANES:(hp + 1) * LANES] = o_ref[hp]
    out_ref[...] = h_ref[...] + jnp.dot(cat_ref[...], w_ref[...], preferred_element_type=F32)


def _c_out(h, o, w_out):
    b, s, d = h.shape
    hp = C_WIDTH // LANES
    row = pl.BlockSpec((None, PROJ_ROWS, d), lambda bi, i: (bi, i, 0))
    return pl.pallas_call(
        _c_out_kernel,
        grid=(b, s // PROJ_ROWS),
        in_specs=[row, pl.BlockSpec((None, hp, PROJ_ROWS, LANES), lambda bi, i: (bi, 0, i, 0)),
                  _resident((C_WIDTH, d))],
        out_specs=row,
        out_shape=jax.ShapeDtypeStruct((b, s, d), F32),
        scratch_shapes=[pltpu.VMEM((PROJ_ROWS, C_WIDTH), BF16)],
        compiler_params=_params("parallel", "parallel"),
        name="c_out_proj",
    )(h, o, w_out)


def kernel(x, ffn1_norm, ffn1_w_gate, ffn1_w_up, ffn1_w_down, mix_norm, ffn2_norm, ffn2_w_gate, ffn2_w_up,
           ffn2_w_down, ab_w_in, ab_v_norm, ab_w_spatial, ab_b_spatial, ab_w_out, c_w_in, c_w_out, final_norm):
    b, s, d = x.shape
    assert DILATIONS == (1, 4, 16)
    assert d == D_MODEL and s % SWA_TOKENS == 0 and s % MOBA_ROWS == 0 and (b * s) % FFN_ROWS == 0
    assert s // C_BLOCK <= HEAD_DIM
    bf = lambda w: w.astype(BF16)

    def ffn(h, layer, norm, wg, wu, wd, final_g=None):
        y = _ffn(h.reshape(b * s, d), norm[layer], bf(wg[layer]), bf(wu[layer]), bf(wd[layer]), final_g)
        return y.reshape(b, s, d)

    h = x
    h = ffn(h, 0, ffn1_norm, ffn1_w_gate, ffn1_w_up, ffn1_w_down)
    qkvs, b_out = _ab_in(h, mix_norm[0], bf(ab_w_in[0]), ab_v_norm[0], ab_w_spatial[0], ab_b_spatial[0])
    outs, lses = zip(*[_swa(*qkv, dil) for qkv, dil in zip(qkvs, DILATIONS)])
    h = _ab_out(h, outs, lses, b_out, bf(ab_w_out[0]))
    h = ffn(h, 0, ffn2_norm, ffn2_w_gate, ffn2_w_up, ffn2_w_down)
    h = ffn(h, 1, ffn1_norm, ffn1_w_gate, ffn1_w_up, ffn1_w_down)
    qc, kc, vc = _c_in(h, mix_norm[1], bf(c_w_in[0]))
    h = _c_out(h, _moba(qc, kc, vc), bf(c_w_out[0]))
    h = ffn(h, 1, ffn2_norm, ffn2_w_gate, ffn2_w_up, ffn2_w_down, final_g=final_norm)
    return h
```

```python
import functools

import jax
import jax.numpy as jnp
from jax import lax
from jax.experimental import pallas as pl
from jax.experimental.pallas import tpu as pltpu

F32 = jnp.float32
BF16 = jnp.bfloat16

D_MODEL = 1024
D_FF = 2816
EPS = 1e-6
HEAD_DIM = 64
LANES = 128
A_HEADS = 8
A_PATTERNS = ((128, 1), (512, 4), (2048, 16))
A_WIDTH = A_HEADS * HEAD_DIM
DILATIONS = tuple(d for _, d in A_PATTERNS)
B_GROUPS = 4
B_CHUNK = 128
B_WIDTH = B_GROUPS * LANES
C_HEADS = 16
C_WIDTH = C_HEADS * HEAD_DIM
C_BLOCK = 256
C_TOPK = 3
SCALE = HEAD_DIM ** -0.5
NEG_BIG = -1e30

M_INIT = -(2.0 ** 60)
SCORE_OFF = -(2.0 ** 100)

VMEM_LIMIT_BYTES = 56 * 1024 * 1024

FFN_ROWS = 1024
FFN_CHUNK = 256
PROJ_ROWS = 512
SWA_TOKENS = 2048
SWA_WIN = 128
MOBA_ROWS = 1024

CONTRACT_LAST = (((1,), (1,)), ((), ()))


def _params(*sem):
    return pltpu.CompilerParams(dimension_semantics=sem, vmem_limit_bytes=VMEM_LIMIT_BYTES)


def _rms_norm(x, g):
    return x * lax.rsqrt(jnp.mean(x * x, axis=-1, keepdims=True) + EPS) * g


def _resident(shape):
    nd = len(shape)
    return pl.BlockSpec(shape, lambda *_: (0,) * nd, pipeline_mode=pl.Buffered(1))


FFN_WEIGHT_STEPS = D_FF // FFN_CHUNK


def _ffn_kernel(x_ref, g_ref, wg_ref, wu_ref, wd_ref, *rest, final):
    if final:
        gf_ref, o_ref, wg_bf, wu_bf, wd_bf, xn_ref, acc_ref = rest
    else:
        o_ref, wg_bf, wu_bf, wd_bf, xn_ref, acc_ref = rest
    i = pl.program_id(0)

    @pl.when(i < FFN_WEIGHT_STEPS)
    def _():
        off = pl.multiple_of(i * FFN_CHUNK, FFN_CHUNK)
        wg_bf[:, pl.ds(off, FFN_CHUNK)] = wg_ref[...].astype(BF16)
        wu_bf[:, pl.ds(off, FFN_CHUNK)] = wu_ref[...].astype(BF16)
        wd_bf[pl.ds(off, FFN_CHUNK), :] = wd_ref[...].astype(BF16)

    @pl.when(i >= FFN_WEIGHT_STEPS)
    def _():
        xn_ref[...] = _rms_norm(x_ref[...], g_ref[...]).astype(BF16)
        acc_ref[...] = jnp.zeros_like(acc_ref)

        def body(c, carry):
            off = pl.multiple_of(c * FFN_CHUNK, FFN_CHUNK)
            xn = xn_ref[...]
            gate = jnp.dot(xn, wg_bf[:, pl.ds(off, FFN_CHUNK)], preferred_element_type=F32)
            up = jnp.dot(xn, wu_bf[:, pl.ds(off, FFN_CHUNK)], preferred_element_type=F32)
            act = (jax.nn.silu(gate) * up).astype(BF16)
            acc_ref[...] += jnp.dot(act, wd_bf[pl.ds(off, FFN_CHUNK), :], preferred_element_type=F32)
            return carry

        lax.fori_loop(0, FFN_WEIGHT_STEPS, body, 0, unroll=True)
        y = x_ref[...] + 0.5 * acc_ref[...]
        if final:
            y = _rms_norm(y, gf_ref[...])
        o_ref[...] = y


def _ffn(x2, g, wg, wu, wd, layer, final_g=None):
    t, d = x2.shape
    final = final_g is not None
    row_spec = pl.BlockSpec((FFN_ROWS, d), lambda i: (jnp.maximum(i - FFN_WEIGHT_STEPS, 0), 0))
    col_chunk = pl.BlockSpec((None, d, FFN_CHUNK), lambda i: (layer, 0, jnp.minimum(i, FFN_WEIGHT_STEPS - 1)))
    row_chunk = pl.BlockSpec((None, FFN_CHUNK, d), lambda i: (layer, jnp.minimum(i, FFN_WEIGHT_STEPS - 1), 0))
    in_specs = [row_spec, _resident((1, d)), col_chunk, col_chunk, row_chunk]
    args = [x2, g.reshape(1, d), wg, wu, wd]
    if final:
        in_specs.append(_resident((1, d)))
        args.append(final_g.reshape(1, d))
    return pl.pallas_call(
        functools.partial(_ffn_kernel, final=final),
        grid=(FFN_WEIGHT_STEPS + t // FFN_ROWS,),
        in_specs=in_specs,
        out_specs=row_spec,
        out_shape=jax.ShapeDtypeStruct((t, d), F32),
        scratch_shapes=[pltpu.VMEM((d, D_FF), BF16), pltpu.VMEM((d, D_FF), BF16), pltpu.VMEM((D_FF, d), BF16),
                        pltpu.VMEM((FFN_ROWS, d), BF16), pltpu.VMEM((FFN_ROWS, d), F32)],
        compiler_params=_params("arbitrary"),
        name="ffn_final" if final else "ffn",
    )(*args)


def _ab_in_kernel(h_ref, g_ref, w_ref, vn_ref, ws_ref, bs_ref, *rest):
    qkv_refs, b_ref, z1_ref, z4_ref = rest[:3 * len(DILATIONS)], rest[-3], rest[-2], rest[-1]
    n_hp = A_WIDTH // LANES
    hn = _rms_norm(h_ref[...], g_ref[...]).astype(BF16)

    def project(col, width):
        return jnp.dot(hn, w_ref[:, col:col + width], preferred_element_type=F32)

    for t in range(3):
        z = project(t * A_WIDTH, A_WIDTH)
        if t == 0:
            z = z * SCALE
        for hp in range(n_hp):
            x1 = z[:, hp * LANES:(hp + 1) * LANES]
            qkv_refs[t][hp] = x1.astype(BF16)
            z1_ref[t * n_hp + hp] = x1
        for hp in range(n_hp):
            for r1 in range(4):
                x4 = z1_ref[t * n_hp + hp, pl.ds(r1, PROJ_ROWS // 4, stride=4), :]
                qkv_refs[3 + t][hp, :, r1 * LANES:(r1 + 1) * LANES] = x4.astype(BF16)
                z4_ref[(t * n_hp + hp) * 4 + r1] = x4
        for hp in range(n_hp):
            for r1 in range(4):
                for r2 in range(4):
                    x16 = z4_ref[(t * n_hp + hp) * 4 + r1, pl.ds(r2, PROJ_ROWS // 16, stride=4), :]
                    r = 4 * r2 + r1
                    qkv_refs[6 + t][hp, :, r * LANES:(r + 1) * LANES] = x16.astype(BF16)
    row = lax.broadcasted_iota(jnp.int32, (B_CHUNK, B_CHUNK), 0)
    col = lax.broadcasted_iota(jnp.int32, (B_CHUNK, B_CHUNK), 1)
    causal = row >= col
    zu = project(3 * A_WIDTH, B_WIDTH)
    zv = project(3 * A_WIDTH + B_WIDTH, B_WIDTH)
    for g in range(B_GROUPS):
        lo = g * LANES
        u = jax.nn.gelu(zu[:, lo:lo + LANES])
        v = jax.nn.gelu(zv[:, lo:lo + LANES])
        vn = _rms_norm(v, vn_ref[:, lo:lo + LANES]).astype(BF16)
        ws = jnp.where(causal, ws_ref[g], 0.0).astype(BF16)
        bias = bs_ref[:, g:g + 1]
        for c in range(PROJ_ROWS // B_CHUNK):
            r0 = c * B_CHUNK
            mixed = jnp.dot(ws, vn[r0:r0 + B_CHUNK], preferred_element_type=F32) + bias
            b_ref[r0:r0 + B_CHUNK, lo:lo + LANES] = (u[r0:r0 + B_CHUNK] * mixed).astype(BF16)


def _dilated_spec(hp, d):
    return pl.BlockSpec((None, hp, PROJ_ROWS // d, d * LANES), lambda bi, i: (bi, 0, i, 0))


def _ab_in(h, g, w_in, v_norm, w_s, b_s):
    b, s, d = h.shape
    hp = A_WIDTH // LANES
    width = w_in.shape[1]
    qkv_specs = [_dilated_spec(hp, dil) for dil in DILATIONS for _ in range(3)]
    qkv_shapes = [jax.ShapeDtypeStruct((b, hp, s // dil, dil * LANES), BF16) for dil in DILATIONS for _ in range(3)]
    outs = pl.pallas_call(
        _ab_in_kernel,
        grid=(b, s // PROJ_ROWS),
        in_specs=[
            pl.BlockSpec((None, PROJ_ROWS, d), lambda bi, i: (bi, i, 0)),
            _resident((1, d)),
            _resident((d, width)),
            _resident((1, B_WIDTH)),
            _resident((B_GROUPS, B_CHUNK, B_CHUNK)),
            _resident((B_CHUNK, B_GROUPS)),
        ],
        out_specs=qkv_specs + [pl.BlockSpec((None, PROJ_ROWS, B_WIDTH), lambda bi, i: (bi, i, 0))],
        out_shape=qkv_shapes + [jax.ShapeDtypeStruct((b, s, B_WIDTH), BF16)],
        scratch_shapes=[pltpu.VMEM((3 * hp, PROJ_ROWS, LANES), F32),
                        pltpu.VMEM((3 * hp * 4, PROJ_ROWS // 4, LANES), F32)],
        compiler_params=_params("parallel", "parallel"),
        name="ab_in_proj",
    )(h, g.reshape(1, d), w_in, v_norm.reshape(1, B_WIDTH), w_s, b_s.T)
    return [outs[3 * i:3 * i + 3] for i in range(len(DILATIONS))], outs[-1]


def _swa_kernel(q_ref, kp_ref, kc_ref, vp_ref, vc_ref, o_ref, lse_ref, kk_ref, vv_ref, *, rows, res):
    n = pl.program_id(3)
    nsub = rows // SWA_WIN
    kk_ref[0:SWA_WIN] = kp_ref[...]
    kk_ref[SWA_WIN:] = kc_ref[...]
    vv_ref[0:SWA_WIN] = vp_ref[...]
    vv_ref[SWA_WIN:] = vc_ref[...]
    cols = [slice(r * LANES, (r + 1) * LANES) for r in range(res)]
    q = jnp.concatenate([q_ref[:, c].reshape(nsub, SWA_WIN, LANES) for c in cols], axis=0)
    kwin = jnp.stack([kk_ref[j * SWA_WIN:(j + 2) * SWA_WIN, c] for c in cols for j in range(nsub)])
    vwin = jnp.stack([vv_ref[j * SWA_WIN:(j + 2) * SWA_WIN, c] for c in cols for j in range(nsub)])
    qi = lax.broadcasted_iota(jnp.int32, (SWA_WIN, 2 * SWA_WIN), 0)
    kc = lax.broadcasted_iota(jnp.int32, (SWA_WIN, 2 * SWA_WIN), 1)
    band = (kc >= qi) & (kc <= qi + SWA_WIN)
    bias = jnp.where(band, 0.0, NEG_BIG)
    bias_first = jnp.where(n == 0, jnp.where(band & (kc >= SWA_WIN), 0.0, NEG_BIG), bias)
    bias = jnp.stack([bias_first if j == 0 else bias for _ in cols for j in range(nsub)])
    q_head0 = lax.broadcasted_iota(jnp.int32, q.shape, 2) < HEAD_DIM
    zero = jnp.zeros_like(q)
    ms, ps, pvs = [], [], []
    for qm in (jnp.where(q_head0, q, zero), jnp.where(q_head0, zero, q)):
        s = jnp.einsum("bqd,bkd->bqk", qm, kwin, preferred_element_type=F32) + bias
        m = jnp.max(s, axis=-1, keepdims=True)
        p = jnp.exp(s - m).astype(BF16)
        ms.append(m)
        ps.append(p)
        pvs.append(jnp.einsum("bqk,bkd->bqd", p, vwin, preferred_element_type=F32))
    krow = lax.broadcasted_iota(jnp.int32, (4 * SWA_WIN, LANES), 0)
    lane = lax.broadcasted_iota(jnp.int32, (4 * SWA_WIN, LANES), 1)
    ones_sel = ((krow < 2 * SWA_WIN) == (lane < HEAD_DIM)).astype(BF16)
    den = jnp.einsum("bqk,kd->bqd", jnp.concatenate(ps, axis=-1), ones_sel, preferred_element_type=F32)
    out = jnp.where(q_head0, pvs[0], pvs[1]) / den
    lse = jnp.where(q_head0, ms[0], ms[1]) + jnp.log(den)
    for r, c in enumerate(cols):
        o_ref[:, c] = out[r * nsub:(r + 1) * nsub].reshape(rows, LANES).astype(BF16)
        lse_ref[:, c] = lse[r * nsub:(r + 1) * nsub].reshape(rows, LANES)


def _swa(q, k, v, dilation):
    b, hp, sub, _ = q.shape
    res = min(dilation, SWA_TOKENS // SWA_WIN)
    rows = min(sub, SWA_TOKENS // res)
    per_step = rows // SWA_WIN
    cur = pl.BlockSpec((None, None, rows, res * LANES), lambda bi, h, r, n: (bi, h, n, r))
    prev = pl.BlockSpec((None, None, SWA_WIN, res * LANES),
                        lambda bi, h, r, n: (bi, h, jnp.maximum(n * per_step - 1, 0), r))
    return pl.pallas_call(
        functools.partial(_swa_kernel, rows=rows, res=res),
        grid=(b, hp, dilation // res, sub // rows),
        in_specs=[cur, prev, cur, prev, cur],
        out_specs=[cur, cur],
        out_shape=[jax.ShapeDtypeStruct(q.shape, BF16), jax.ShapeDtypeStruct(q.shape, F32)],
        scratch_shapes=[pltpu.VMEM((rows + SWA_WIN, res * LANES), BF16),
                        pltpu.VMEM((rows + SWA_WIN, res * LANES), BF16)],
        compiler_params=_params("parallel", "parallel", "parallel", "parallel"),
        name=f"swa_d{dilation}",
    )(q, k, k, v, v)


def _ab_out_kernel(h_ref, *rest):
    n_pat = len(DILATIONS)
    o_refs, l_refs = rest[:n_pat], rest[n_pat:2 * n_pat]
    b_ref, w_ref, out_ref, cat_ref, tok_ref = rest[2 * n_pat:]
    for hp in range(A_WIDTH // LANES):
        outs, lses = [], []
        for i, d in enumerate(DILATIONS):
            vals = []
            for kind, ref in enumerate((o_refs[i], l_refs[i])):
                if d == 1:
                    vals.append(ref[hp].astype(F32))
                    continue
                slot = (hp * n_pat + i) * 2 + kind
                for r in range(d):
                    tok_ref[slot, pl.ds(r, PROJ_ROWS // d, stride=d), :] = (
                        ref[hp, :, r * LANES:(r + 1) * LANES].astype(F32))
                vals.append(tok_ref[slot])
            outs.append(vals[0])
            lses.append(vals[1])
        m = functools.reduce(jnp.maximum, lses)
        es = [jnp.exp(l - m) for l in lses]
        mix = sum(e * o for e, o in zip(es, outs)) / sum(es)
        cat_ref[:, hp * LANES:(hp + 1) * LANES] = mix.astype(BF16)
    cat_ref[:, A_WIDTH:] = b_ref[...]
    out_ref[...] = h_ref[...] + jnp.dot(cat_ref[...], w_ref[...], preferred_element_type=F32)


def _ab_out(h, outs, lses, b_out, w_out):
    b, s, d = h.shape
    hp = A_WIDTH // LANES
    row = pl.BlockSpec((None, PROJ_ROWS, d), lambda bi, i: (bi, i, 0))
    heads = [_dilated_spec(hp, dil) for dil in DILATIONS]
    return pl.pallas_call(
        _ab_out_kernel,
        grid=(b, s // PROJ_ROWS),
        in_specs=[row] + heads + heads + [pl.BlockSpec((None, PROJ_ROWS, B_WIDTH), lambda bi, i: (bi, i, 0)),
                                          _resident((A_WIDTH + B_WIDTH, d))],
        out_specs=row,
        out_shape=jax.ShapeDtypeStruct((b, s, d), F32),
        scratch_shapes=[pltpu.VMEM((PROJ_ROWS, A_WIDTH + B_WIDTH), BF16),
                        pltpu.VMEM((hp * len(DILATIONS) * 2, PROJ_ROWS, LANES), F32)],
        compiler_params=_params("parallel", "parallel"),
        name="ab_out_proj",
    )(h, *outs, *lses, b_out, w_out)


def _c_in_kernel(h_ref, g_ref, w_ref, q_ref, k_ref, v_ref):
    hn = _rms_norm(h_ref[...], g_ref[...]).astype(BF16)
    z = jnp.dot(hn, w_ref[...], preferred_element_type=F32)
    for hp in range(C_WIDTH // LANES):
        lo = hp * LANES
        q_ref[hp] = (z[:, lo:lo + LANES] * SCALE).astype(BF16)
        k_ref[hp] = z[:, C_WIDTH + lo:C_WIDTH + lo + LANES].astype(BF16)
        v_ref[hp] = z[:, 2 * C_WIDTH + lo:2 * C_WIDTH + lo + LANES].astype(BF16)


def _c_in(h, g, w_in):
    b, s, d = h.shape
    hp = C_WIDTH // LANES
    spec = pl.BlockSpec((None, hp, PROJ_ROWS, LANES), lambda bi, i: (bi, 0, i, 0))
    shape = jax.ShapeDtypeStruct((b, hp, s, LANES), BF16)
    return pl.pallas_call(
        _c_in_kernel,
        grid=(b, s // PROJ_ROWS),
        in_specs=[pl.BlockSpec((None, PROJ_ROWS, d), lambda bi, i: (bi, i, 0)),
                  _resident((1, d)), _resident((d, 3 * C_WIDTH))],
        out_specs=[spec, spec, spec],
        out_shape=[shape, shape, shape],
        compiler_params=_params("parallel", "parallel"),
        name="c_in_proj",
    )(h, g.reshape(1, d), w_in)


def _moba_select(q_ref, k_ref, qaug_ref, st_ref, m_ref, nb):
    seq = q_ref.shape[0]
    kmean = jnp.concatenate(
        [jnp.mean(k_ref[j * C_BLOCK:(j + 1) * C_BLOCK].astype(F32), axis=0, keepdims=True) for j in range(nb)],
        axis=0).astype(BF16)
    lane = lax.broadcasted_iota(jnp.int32, (MOBA_ROWS, LANES), 1)
    blk = lax.broadcasted_iota(jnp.int32, (nb, MOBA_ROWS), 0)
    qpos = lax.broadcasted_iota(jnp.int32, (nb, MOBA_ROWS), 1)
    zeros = lambda n: jnp.zeros((n, MOBA_ROWS), F32)

    def chunk(t, carry):
        r0 = pl.multiple_of(t * MOBA_ROWS, MOBA_ROWS)
        q = q_ref[pl.ds(r0, MOBA_ROWS), :]
        own = (qpos + r0) // C_BLOCK
        past = blk < own
        for h in range(2):
            qm = jnp.where(lane < HEAD_DIM if h == 0 else lane >= HEAD_DIM, q, jnp.zeros_like(q))
            gate = lax.dot_general(kmean, qm, CONTRACT_LAST, preferred_element_type=F32)
            gate = jnp.where(past, gate, -jnp.inf)
            chosen = jnp.zeros(gate.shape, jnp.bool_)
            for _ in range(C_TOPK):
                best = jnp.max(gate, axis=0, keepdims=True)
                first = jnp.min(jnp.where(gate == best, blk, nb), axis=0, keepdims=True)
                hit = blk == first
                chosen = chosen | hit
                gate = jnp.where(hit, -jnp.inf, gate)
            unselected = 1.0 - ((chosen & past) | (blk == own)).astype(F32)
            if h == 0:
                extra_t = jnp.concatenate([zeros(HEAD_DIM), unselected, zeros(HEAD_DIM - nb)], axis=0)
            else:
                extra_t = jnp.concatenate([unselected, zeros(LANES - nb)], axis=0)
            qaug_ref[h, pl.ds(r0, MOBA_ROWS), :] = (qm.astype(F32) + extra_t.T).astype(BF16)
        return carry

    lax.fori_loop(0, seq // MOBA_ROWS, chunk, 0)
    st_ref[...] = jnp.zeros_like(st_ref)
    m_ref[...] = jnp.full(m_ref.shape, M_INIT, F32)


def _moba_kernel(q_ref, k_ref, v_ref, o_ref, qaug_ref, st_ref, m_ref):
    j = pl.program_id(2)
    seq = q_ref.shape[0]
    nb = seq // C_BLOCK

    @pl.when(j == 0)
    def _():
        _moba_select(q_ref, k_ref, qaug_ref, st_ref, m_ref, nb)

    j0 = pl.multiple_of(j * MOBA_ROWS, MOBA_ROWS)
    kj = k_ref[pl.ds(j0, MOBA_ROWS), :]
    vj = v_ref[pl.ds(j0, MOBA_ROWS), :]
    lane = lax.broadcasted_iota(jnp.int32, (MOBA_ROWS, LANES), 1)
    key_blk = j * (MOBA_ROWS // C_BLOCK) + lax.broadcasted_iota(jnp.int32, (MOBA_ROWS, LANES), 0) // C_BLOCK
    in_head = [lane < HEAD_DIM, lane >= HEAD_DIM]
    flag_lane = [HEAD_DIM + key_blk, key_blk]
    k_aug = [jnp.where(in_head[h], kj, jnp.where(lane == flag_lane[h], SCORE_OFF, 0.0).astype(BF16))
             for h in range(2)]
    v_aug = [jnp.where(in_head[h], vj, jnp.ones_like(vj)) for h in range(2)]

    def update(h, rows, causal):
        qa = qaug_ref[h, pl.ds(rows, MOBA_ROWS), :]
        s = lax.dot_general(qa, k_aug[h], CONTRACT_LAST, preferred_element_type=F32)
        if causal:
            qi = lax.broadcasted_iota(jnp.int32, s.shape, 0)
            ki = lax.broadcasted_iota(jnp.int32, s.shape, 1)
            s = jnp.where(ki <= qi, s, SCORE_OFF)
        m_old = m_ref[h, pl.ds(rows, MOBA_ROWS), :]
        m_new = jnp.maximum(m_old, jnp.max(s, axis=-1, keepdims=True))
        p = jnp.exp(s - jnp.concatenate([m_new] * (MOBA_ROWS // LANES), axis=1))
        pv = jnp.dot(p.astype(BF16), v_aug[h], preferred_element_type=F32)
        st = jnp.exp(m_old - m_new) * st_ref[h, pl.ds(rows, MOBA_ROWS), :] + pv
        return st, m_new

    done = []
    for h in range(2):
        st, _ = update(h, j0, True)
        done.append(st / pltpu.roll(st, HEAD_DIM, 1))
    o_ref[pl.ds(j0, MOBA_ROWS), :] = jnp.where(in_head[0], done[0], done[1]).astype(BF16)

    def chunk(t, carry):
        rows = pl.multiple_of(t * MOBA_ROWS, MOBA_ROWS)
        for h in range(2):
            st, m_new = update(h, rows, False)
            st_ref[h, pl.ds(rows, MOBA_ROWS), :] = st
            m_ref[h, pl.ds(rows, MOBA_ROWS), :] = m_new
        return carry

    first = j + 1
    n_later = seq // MOBA_ROWS - first
    lax.fori_loop(0, n_later // 2, lambda i, c: chunk(first + 2 * i + 1, chunk(first + 2 * i, c)), 0)

    @pl.when(n_later % 2 == 1)
    def _():
        chunk(seq // MOBA_ROWS - 1, 0)


def _moba(q, k, v):
    b, hp, s, _ = q.shape
    whole = pl.BlockSpec((None, None, s, LANES), lambda bi, h, j: (bi, h, 0, 0))
    whole_in = pl.BlockSpec((None, None, s, LANES), lambda bi, h, j: (bi, h, 0, 0), pipeline_mode=pl.Buffered(1))
    return pl.pallas_call(
        _moba_kernel,
        grid=(b, hp, s // MOBA_ROWS),
        in_specs=[whole_in, whole_in, whole_in],
        out_specs=whole,
        out_shape=jax.ShapeDtypeStruct((b, hp, s, LANES), BF16),
        scratch_shapes=[pltpu.VMEM((2, s, LANES), BF16),
                        pltpu.VMEM((2, s, LANES), F32),
                        pltpu.VMEM((2, s, LANES), F32)],
        compiler_params=_params("parallel", "parallel", "arbitrary"),
        name="moba",
    )(q, k, v)


def _c_out_kernel(h_ref, o_ref, w_ref, out_ref, cat_ref):
    for hp in range(C_WIDTH // LANES):
        cat_ref[:, hp * LANES:(hp + 1) * LANES] = o_ref[hp]
    out_ref[...] = h_ref[...] + jnp.dot(cat_ref[...], w_ref[...], preferred_element_type=F32)


def _c_out(h, o, w_out):
    b, s, d = h.shape
    hp = C_WIDTH // LANES
    row = pl.BlockSpec((None, PROJ_ROWS, d), lambda bi, i: (bi, i, 0))
    return pl.pallas_call(
        _c_out_kernel,
        grid=(b, s // PROJ_ROWS),
        in_specs=[row, pl.BlockSpec((None, hp, PROJ_ROWS, LANES), lambda bi, i: (bi, 0, i, 0)),
                  _resident((C_WIDTH, d))],
        out_specs=row,
        out_shape=jax.ShapeDtypeStruct((b, s, d), F32),
        scratch_shapes=[pltpu.VMEM((PROJ_ROWS, C_WIDTH), BF16)],
        compiler_params=_params("parallel", "parallel"),
        name="c_out_proj",
    )(h, o, w_out)


def kernel(x, ffn1_norm, ffn1_w_gate, ffn1_w_up, ffn1_w_down, mix_norm, ffn2_norm, ffn2_w_gate, ffn2_w_up,
           ffn2_w_down, ab_w_in, ab_v_norm, ab_w_spatial, ab_b_spatial, ab_w_out, c_w_in, c_w_out, final_norm):
    b, s, d = x.shape
    assert DILATIONS == (1, 4, 16)
    assert d == D_MODEL and s % SWA_TOKENS == 0 and s % MOBA_ROWS == 0 and (b * s) % FFN_ROWS == 0
    assert s // C_BLOCK <= HEAD_DIM
    bf = lambda w: w.astype(BF16)

    def ffn(h, layer, norm, wg, wu, wd, final_g=None):
        return _ffn(h.reshape(b * s, d), norm[layer], wg, wu, wd, layer, final_g).reshape(b, s, d)

    h = x
    h = ffn(h, 0, ffn1_norm, ffn1_w_gate, ffn1_w_up, ffn1_w_down)
    qkvs, b_out = _ab_in(h, mix_norm[0], bf(ab_w_in[0]), ab_v_norm[0], ab_w_spatial[0], ab_b_spatial[0])
    outs, lses = zip(*[_swa(*qkv, dil) for qkv, dil in zip(qkvs, DILATIONS)])
    h = _ab_out(h, outs, lses, b_out, bf(ab_w_out[0]))
    h = ffn(h, 0, ffn2_norm, ffn2_w_gate, ffn2_w_up, ffn2_w_down)
    h = ffn(h, 1, ffn1_norm, ffn1_w_gate, ffn1_w_up, ffn1_w_down)
    qc, kc, vc = _c_in(h, mix_norm[1], bf(c_w_in[0]))
    h = _c_out(h, _moba(qc, kc, vc), bf(c_w_out[0]))
    h = ffn(h, 1, ffn2_norm, ffn2_w_gate, ffn2_w_up, ffn2_w_down, final_g=final_norm)
    return h
```

```python
import functools

import jax
import jax.numpy as jnp
from jax import lax
from jax.experimental import pallas as pl
from jax.experimental.pallas import tpu as pltpu

F32 = jnp.float32
BF16 = jnp.bfloat16

D_MODEL = 1024
D_FF = 2816
EPS = 1e-6
HEAD_DIM = 64
LANES = 128
A_HEADS = 8
A_PATTERNS = ((128, 1), (512, 4), (2048, 16))
A_WIDTH = A_HEADS * HEAD_DIM
DILATIONS = tuple(d for _, d in A_PATTERNS)
B_GROUPS = 4
B_CHUNK = 128
B_WIDTH = B_GROUPS * LANES
C_HEADS = 16
C_WIDTH = C_HEADS * HEAD_DIM
C_BLOCK = 256
C_TOPK = 3
SCALE = HEAD_DIM ** -0.5
NEG_BIG = -1e30

M_INIT = -(2.0 ** 60)
SCORE_OFF = -(2.0 ** 100)

VMEM_LIMIT_BYTES = 56 * 1024 * 1024

FFN_ROWS = 1024
FFN_CHUNK = 256
PROJ_ROWS = 1024
SWA_TOKENS = 2048
SWA_WIN = 128
MOBA_ROWS = 1024

CONTRACT_LAST = (((1,), (1,)), ((), ()))


def _params(*sem):
    return pltpu.CompilerParams(dimension_semantics=sem, vmem_limit_bytes=VMEM_LIMIT_BYTES)


def _rms_norm(x, g):
    return x * lax.rsqrt(jnp.mean(x * x, axis=-1, keepdims=True) + EPS) * g


def _resident(shape):
    nd = len(shape)
    return pl.BlockSpec(shape, lambda *_: (0,) * nd, pipeline_mode=pl.Buffered(1))


FFN_WEIGHT_STEPS = D_FF // FFN_CHUNK


def _ffn_kernel(x_ref, g_ref, wg_ref, wu_ref, wd_ref, *rest, final):
    if final:
        gf_ref, o_ref, wg_bf, wu_bf, wd_bf, xn_ref, acc_ref = rest
    else:
        o_ref, wg_bf, wu_bf, wd_bf, xn_ref, acc_ref = rest
    i = pl.program_id(0)

    @pl.when(i < FFN_WEIGHT_STEPS)
    def _():
        off = pl.multiple_of(i * FFN_CHUNK, FFN_CHUNK)
        wg_bf[:, pl.ds(off, FFN_CHUNK)] = wg_ref[...].astype(BF16)
        wu_bf[:, pl.ds(off, FFN_CHUNK)] = wu_ref[...].astype(BF16)
        wd_bf[pl.ds(off, FFN_CHUNK), :] = wd_ref[...].astype(BF16)

    @pl.when(i >= FFN_WEIGHT_STEPS)
    def _():
        xn_ref[...] = _rms_norm(x_ref[...], g_ref[...]).astype(BF16)
        acc_ref[...] = jnp.zeros_like(acc_ref)

        def body(c, carry):
            off = pl.multiple_of(c * FFN_CHUNK, FFN_CHUNK)
            xn = xn_ref[...]
            gate = jnp.dot(xn, wg_bf[:, pl.ds(off, FFN_CHUNK)], preferred_element_type=F32)
            up = jnp.dot(xn, wu_bf[:, pl.ds(off, FFN_CHUNK)], preferred_element_type=F32)
            act = (jax.nn.silu(gate) * up).astype(BF16)
            acc_ref[...] += jnp.dot(act, wd_bf[pl.ds(off, FFN_CHUNK), :], preferred_element_type=F32)
            return carry

        lax.fori_loop(0, FFN_WEIGHT_STEPS, body, 0, unroll=True)
        y = x_ref[...] + 0.5 * acc_ref[...]
        if final:
            y = _rms_norm(y, gf_ref[...])
        o_ref[...] = y


def _ffn(x2, g, wg, wu, wd, layer, final_g=None):
    t, d = x2.shape
    final = final_g is not None
    row_spec = pl.BlockSpec((FFN_ROWS, d), lambda i: (jnp.maximum(i - FFN_WEIGHT_STEPS, 0), 0))
    col_chunk = pl.BlockSpec((None, d, FFN_CHUNK), lambda i: (layer, 0, jnp.minimum(i, FFN_WEIGHT_STEPS - 1)))
    row_chunk = pl.BlockSpec((None, FFN_CHUNK, d), lambda i: (layer, jnp.minimum(i, FFN_WEIGHT_STEPS - 1), 0))
    in_specs = [row_spec, _resident((1, d)), col_chunk, col_chunk, row_chunk]
    args = [x2, g.reshape(1, d), wg, wu, wd]
    if final:
        in_specs.append(_resident((1, d)))
        args.append(final_g.reshape(1, d))
    return pl.pallas_call(
        functools.partial(_ffn_kernel, final=final),
        grid=(FFN_WEIGHT_STEPS + t // FFN_ROWS,),
        in_specs=in_specs,
        out_specs=row_spec,
        out_shape=jax.ShapeDtypeStruct((t, d), F32),
        scratch_shapes=[pltpu.VMEM((d, D_FF), BF16), pltpu.VMEM((d, D_FF), BF16), pltpu.VMEM((D_FF, d), BF16),
                        pltpu.VMEM((FFN_ROWS, d), BF16), pltpu.VMEM((FFN_ROWS, d), F32)],
        compiler_params=_params("arbitrary"),
        name="ffn_final" if final else "ffn",
    )(*args)


def _ab_in_kernel(h_ref, g_ref, w_ref, vn_ref, ws_ref, bs_ref, *rest):
    qkv_refs, b_ref, z1_ref, z4_ref = rest[:3 * len(DILATIONS)], rest[-3], rest[-2], rest[-1]
    n_hp = A_WIDTH // LANES
    hn = _rms_norm(h_ref[...], g_ref[...]).astype(BF16)

    def project(col, width):
        return jnp.dot(hn, w_ref[:, col:col + width], preferred_element_type=F32)

    for t in range(3):
        z = project(t * A_WIDTH, A_WIDTH)
        if t == 0:
            z = z * SCALE
        for hp in range(n_hp):
            x1 = z[:, hp * LANES:(hp + 1) * LANES]
            qkv_refs[t][hp] = x1.astype(BF16)
            z1_ref[t * n_hp + hp] = x1
        for hp in range(n_hp):
            for r1 in range(4):
                x4 = z1_ref[t * n_hp + hp, pl.ds(r1, PROJ_ROWS // 4, stride=4), :]
                qkv_refs[3 + t][hp, :, r1 * LANES:(r1 + 1) * LANES] = x4.astype(BF16)
                z4_ref[(t * n_hp + hp) * 4 + r1] = x4
        for hp in range(n_hp):
            for r1 in range(4):
                for r2 in range(4):
                    x16 = z4_ref[(t * n_hp + hp) * 4 + r1, pl.ds(r2, PROJ_ROWS // 16, stride=4), :]
                    r = 4 * r2 + r1
                    qkv_refs[6 + t][hp, :, r * LANES:(r + 1) * LANES] = x16.astype(BF16)
    row = lax.broadcasted_iota(jnp.int32, (B_CHUNK, B_CHUNK), 0)
    col = lax.broadcasted_iota(jnp.int32, (B_CHUNK, B_CHUNK), 1)
    causal = row >= col
    zu = project(3 * A_WIDTH, B_WIDTH)
    zv = project(3 * A_WIDTH + B_WIDTH, B_WIDTH)
    for g in range(B_GROUPS):
        lo = g * LANES
        u = jax.nn.gelu(zu[:, lo:lo + LANES])
        v = jax.nn.gelu(zv[:, lo:lo + LANES])
        vn = _rms_norm(v, vn_ref[:, lo:lo + LANES]).astype(BF16)
        ws = jnp.where(causal, ws_ref[g], 0.0).astype(BF16)
        bias = bs_ref[:, g:g + 1]
        for c in range(PROJ_ROWS // B_CHUNK):
            r0 = c * B_CHUNK
            mixed = jnp.dot(ws, vn[r0:r0 + B_CHUNK], preferred_element_type=F32) + bias
            b_ref[r0:r0 + B_CHUNK, lo:lo + LANES] = (u[r0:r0 + B_CHUNK] * mixed).astype(BF16)


def _dilated_spec(hp, d):
    return pl.BlockSpec((None, hp, PROJ_ROWS // d, d * LANES), lambda bi, i: (bi, 0, i, 0))


def _ab_in(h, g, w_in, v_norm, w_s, b_s):
    b, s, d = h.shape
    hp = A_WIDTH // LANES
    width = w_in.shape[1]
    qkv_specs = [_dilated_spec(hp, dil) for dil in DILATIONS for _ in range(3)]
    qkv_shapes = [jax.ShapeDtypeStruct((b, hp, s // dil, dil * LANES), BF16) for dil in DILATIONS for _ in range(3)]
    outs = pl.pallas_call(
        _ab_in_kernel,
        grid=(b, s // PROJ_ROWS),
        in_specs=[
            pl.BlockSpec((None, PROJ_ROWS, d), lambda bi, i: (bi, i, 0)),
            _resident((1, d)),
            _resident((d, width)),
            _resident((1, B_WIDTH)),
            _resident((B_GROUPS, B_CHUNK, B_CHUNK)),
            _resident((B_CHUNK, B_GROUPS)),
        ],
        out_specs=qkv_specs + [pl.BlockSpec((None, PROJ_ROWS, B_WIDTH), lambda bi, i: (bi, i, 0))],
        out_shape=qkv_shapes + [jax.ShapeDtypeStruct((b, s, B_WIDTH), BF16)],
        scratch_shapes=[pltpu.VMEM((3 * hp, PROJ_ROWS, LANES), F32),
                        pltpu.VMEM((3 * hp * 4, PROJ_ROWS // 4, LANES), F32)],
        compiler_params=_params("parallel", "parallel"),
        name="ab_in_proj",
    )(h, g.reshape(1, d), w_in, v_norm.reshape(1, B_WIDTH), w_s, b_s.T)
    return [outs[3 * i:3 * i + 3] for i in range(len(DILATIONS))], outs[-1]


def _swa_kernel(q_ref, kp_ref, kc_ref, vp_ref, vc_ref, o_ref, lse_ref, kk_ref, vv_ref, *, rows, res):
    n = pl.program_id(3)
    nsub = rows // SWA_WIN
    kk_ref[0:SWA_WIN] = kp_ref[...]
    kk_ref[SWA_WIN:] = kc_ref[...]
    vv_ref[0:SWA_WIN] = vp_ref[...]
    vv_ref[SWA_WIN:] = vc_ref[...]
    cols = [slice(r * LANES, (r + 1) * LANES) for r in range(res)]
    q = jnp.concatenate([q_ref[:, c].reshape(nsub, SWA_WIN, LANES) for c in cols], axis=0)
    kwin = jnp.stack([kk_ref[j * SWA_WIN:(j + 2) * SWA_WIN, c] for c in cols for j in range(nsub)])
    vwin = jnp.stack([vv_ref[j * SWA_WIN:(j + 2) * SWA_WIN, c] for c in cols for j in range(nsub)])
    qi = lax.broadcasted_iota(jnp.int32, (SWA_WIN, 2 * SWA_WIN), 0)
    kc = lax.broadcasted_iota(jnp.int32, (SWA_WIN, 2 * SWA_WIN), 1)
    band = (kc >= qi) & (kc <= qi + SWA_WIN)
    bias = jnp.where(band, 0.0, NEG_BIG)
    bias_first = jnp.where(n == 0, jnp.where(band & (kc >= SWA_WIN), 0.0, NEG_BIG), bias)
    bias = jnp.stack([bias_first if j == 0 else bias for _ in cols for j in range(nsub)])
    q_head0 = lax.broadcasted_iota(jnp.int32, q.shape, 2) < HEAD_DIM
    zero = jnp.zeros_like(q)
    v_ext = jnp.concatenate([vwin, jnp.ones_like(vwin)], axis=-1)
    ms, pvs, dens = [], [], []
    for qm in (jnp.where(q_head0, q, zero), jnp.where(q_head0, zero, q)):
        s = jnp.einsum("bqd,bkd->bqk", qm, kwin, preferred_element_type=F32) + bias
        m = jnp.max(s, axis=-1, keepdims=True)
        p = jnp.exp(s - m).astype(BF16)
        pv = jnp.einsum("bqk,bkd->bqd", p, v_ext, preferred_element_type=F32)
        ms.append(m)
        pvs.append(pv[..., :LANES])
        dens.append(pv[..., LANES:])
    den = jnp.where(q_head0, dens[0], dens[1])
    out = jnp.where(q_head0, pvs[0], pvs[1]) / den
    lse = jnp.where(q_head0, ms[0], ms[1]) + jnp.log(den)
    for r, c in enumerate(cols):
        o_ref[:, c] = out[r * nsub:(r + 1) * nsub].reshape(rows, LANES).astype(BF16)
        lse_ref[:, c] = lse[r * nsub:(r + 1) * nsub].reshape(rows, LANES)


def _swa(q, k, v, dilation):
    b, hp, sub, _ = q.shape
    res = min(dilation, SWA_TOKENS // SWA_WIN)
    rows = min(sub, SWA_TOKENS // res)
    per_step = rows // SWA_WIN
    cur = pl.BlockSpec((None, None, rows, res * LANES), lambda bi, h, r, n: (bi, h, n, r))
    prev = pl.BlockSpec((None, None, SWA_WIN, res * LANES),
                        lambda bi, h, r, n: (bi, h, jnp.maximum(n * per_step - 1, 0), r))
    return pl.pallas_call(
        functools.partial(_swa_kernel, rows=rows, res=res),
        grid=(b, hp, dilation // res, sub // rows),
        in_specs=[cur, prev, cur, prev, cur],
        out_specs=[cur, cur],
        out_shape=[jax.ShapeDtypeStruct(q.shape, BF16), jax.ShapeDtypeStruct(q.shape, F32)],
        scratch_shapes=[pltpu.VMEM((rows + SWA_WIN, res * LANES), BF16),
                        pltpu.VMEM((rows + SWA_WIN, res * LANES), BF16)],
        compiler_params=_params("parallel", "parallel", "parallel", "parallel"),
        name=f"swa_d{dilation}",
    )(q, k, k, v, v)


def _ab_out_kernel(h_ref, *rest):
    n_pat = len(DILATIONS)
    o_refs, l_refs = rest[:n_pat], rest[n_pat:2 * n_pat]
    b_ref, w_ref, out_ref, cat_ref, tok_ref = rest[2 * n_pat:]
    for hp in range(A_WIDTH // LANES):
        outs, lses = [], []
        for i, d in enumerate(DILATIONS):
            vals = []
            for kind, ref in enumerate((o_refs[i], l_refs[i])):
                if d == 1:
                    vals.append(ref[hp].astype(F32))
                    continue
                slot = (hp * n_pat + i) * 2 + kind
                for r in range(d):
                    tok_ref[slot, pl.ds(r, PROJ_ROWS // d, stride=d), :] = (
                        ref[hp, :, r * LANES:(r + 1) * LANES].astype(F32))
                vals.append(tok_ref[slot])
            outs.append(vals[0])
            lses.append(vals[1])
        m = functools.reduce(jnp.maximum, lses)
        es = [jnp.exp(l - m) for l in lses]
        mix = sum(e * o for e, o in zip(es, outs)) / sum(es)
        cat_ref[:, hp * LANES:(hp + 1) * LANES] = mix.astype(BF16)
    cat_ref[:, A_WIDTH:] = b_ref[...]
    out_ref[...] = h_ref[...] + jnp.dot(cat_ref[...], w_ref[...], preferred_element_type=F32)


def _ab_out(h, outs, lses, b_out, w_out):
    b, s, d = h.shape
    hp = A_WIDTH // LANES
    row = pl.BlockSpec((None, PROJ_ROWS, d), lambda bi, i: (bi, i, 0))
    heads = [_dilated_spec(hp, dil) for dil in DILATIONS]
    return pl.pallas_call(
        _ab_out_kernel,
        grid=(b, s // PROJ_ROWS),
        in_specs=[row] + heads + heads + [pl.BlockSpec((None, PROJ_ROWS, B_WIDTH), lambda bi, i: (bi, i, 0)),
                                          _resident((A_WIDTH + B_WIDTH, d))],
        out_specs=row,
        out_shape=jax.ShapeDtypeStruct((b, s, d), F32),
        scratch_shapes=[pltpu.VMEM((PROJ_ROWS, A_WIDTH + B_WIDTH), BF16),
                        pltpu.VMEM((hp * len(DILATIONS) * 2, PROJ_ROWS, LANES), F32)],
        compiler_params=_params("parallel", "parallel"),
        name="ab_out_proj",
    )(h, *outs, *lses, b_out, w_out)


def _c_in_kernel(h_ref, g_ref, w_ref, q_ref, k_ref, v_ref):
    hn = _rms_norm(h_ref[...], g_ref[...]).astype(BF16)
    z = jnp.dot(hn, w_ref[...], preferred_element_type=F32)
    for hp in range(C_WIDTH // LANES):
        lo = hp * LANES
        q_ref[hp] = (z[:, lo:lo + LANES] * SCALE).astype(BF16)
        k_ref[hp] = z[:, C_WIDTH + lo:C_WIDTH + lo + LANES].astype(BF16)
        v_ref[hp] = z[:, 2 * C_WIDTH + lo:2 * C_WIDTH + lo + LANES].astype(BF16)


def _c_in(h, g, w_in):
    b, s, d = h.shape
    hp = C_WIDTH // LANES
    spec = pl.BlockSpec((None, hp, PROJ_ROWS, LANES), lambda bi, i: (bi, 0, i, 0))
    shape = jax.ShapeDtypeStruct((b, hp, s, LANES), BF16)
    return pl.pallas_call(
        _c_in_kernel,
        grid=(b, s // PROJ_ROWS),
        in_specs=[pl.BlockSpec((None, PROJ_ROWS, d), lambda bi, i: (bi, i, 0)),
                  _resident((1, d)), _resident((d, 3 * C_WIDTH))],
        out_specs=[spec, spec, spec],
        out_shape=[shape, shape, shape],
        compiler_params=_params("parallel", "parallel"),
        name="c_in_proj",
    )(h, g.reshape(1, d), w_in)


def _moba_select(q_ref, k_ref, qaug_ref, st_ref, m_ref, nb):
    seq = q_ref.shape[0]
    kmean = jnp.concatenate(
        [jnp.mean(k_ref[j * C_BLOCK:(j + 1) * C_BLOCK].astype(F32), axis=0, keepdims=True) for j in range(nb)],
        axis=0).astype(BF16)
    lane = lax.broadcasted_iota(jnp.int32, (MOBA_ROWS, LANES), 1)
    blk = lax.broadcasted_iota(jnp.int32, (nb, MOBA_ROWS), 0)
    qpos = lax.broadcasted_iota(jnp.int32, (nb, MOBA_ROWS), 1)
    zeros = lambda n: jnp.zeros((n, MOBA_ROWS), F32)

    def chunk(t, carry):
        r0 = pl.multiple_of(t * MOBA_ROWS, MOBA_ROWS)
        q = q_ref[pl.ds(r0, MOBA_ROWS), :]
        own = (qpos + r0) // C_BLOCK
        past = blk < own
        for h in range(2):
            qm = jnp.where(lane < HEAD_DIM if h == 0 else lane >= HEAD_DIM, q, jnp.zeros_like(q))
            gate = lax.dot_general(kmean, qm, CONTRACT_LAST, preferred_element_type=F32)
            gate = jnp.where(past, gate, -jnp.inf)
            chosen = jnp.zeros(gate.shape, jnp.bool_)
            for _ in range(C_TOPK):
                best = jnp.max(gate, axis=0, keepdims=True)
                first = jnp.min(jnp.where(gate == best, blk, nb), axis=0, keepdims=True)
                hit = blk == first
                chosen = chosen | hit
                gate = jnp.where(hit, -jnp.inf, gate)
            unselected = 1.0 - ((chosen & past) | (blk == own)).astype(F32)
            if h == 0:
                extra_t = jnp.concatenate([zeros(HEAD_DIM), unselected, zeros(HEAD_DIM - nb)], axis=0)
            else:
                extra_t = jnp.concatenate([unselected, zeros(LANES - nb)], axis=0)
            qaug_ref[h, pl.ds(r0, MOBA_ROWS), :] = (qm.astype(F32) + extra_t.T).astype(BF16)
        return carry

    lax.fori_loop(0, seq // MOBA_ROWS, chunk, 0)
    st_ref[...] = jnp.zeros_like(st_ref)
    m_ref[...] = jnp.full(m_ref.shape, M_INIT, F32)


def _moba_kernel(q_ref, k_ref, v_ref, o_ref, qaug_ref, st_ref, m_ref):
    j = pl.program_id(2)
    seq = q_ref.shape[0]
    nb = seq // C_BLOCK

    @pl.when(j == 0)
    def _():
        _moba_select(q_ref, k_ref, qaug_ref, st_ref, m_ref, nb)

    j0 = pl.multiple_of(j * MOBA_ROWS, MOBA_ROWS)
    kj = k_ref[pl.ds(j0, MOBA_ROWS), :]
    vj = v_ref[pl.ds(j0, MOBA_ROWS), :]
    lane = lax.broadcasted_iota(jnp.int32, (MOBA_ROWS, LANES), 1)
    key_blk = j * (MOBA_ROWS // C_BLOCK) + lax.broadcasted_iota(jnp.int32, (MOBA_ROWS, LANES), 0) // C_BLOCK
    in_head = [lane < HEAD_DIM, lane >= HEAD_DIM]
    flag_lane = [HEAD_DIM + key_blk, key_blk]
    k_aug = [jnp.where(in_head[h], kj, jnp.where(lane == flag_lane[h], SCORE_OFF, 0.0).astype(BF16))
             for h in range(2)]
    v_aug = [jnp.where(in_head[h], vj, jnp.ones_like(vj)) for h in range(2)]

    def update(h, rows, causal):
        qa = qaug_ref[h, pl.ds(rows, MOBA_ROWS), :]
        s = lax.dot_general(qa, k_aug[h], CONTRACT_LAST, preferred_element_type=F32)
        if causal:
            qi = lax.broadcasted_iota(jnp.int32, s.shape, 0)
            ki = lax.broadcasted_iota(jnp.int32, s.shape, 1)
            s = jnp.where(ki <= qi, s, SCORE_OFF)
        m_old = m_ref[h, pl.ds(rows, MOBA_ROWS), :]
        m_new = jnp.maximum(m_old, jnp.max(s, axis=-1, keepdims=True))
        p = jnp.exp(s - jnp.concatenate([m_new] * (MOBA_ROWS // LANES), axis=1))
        pv = jnp.dot(p.astype(BF16), v_aug[h], preferred_element_type=F32)
        st = jnp.exp(m_old - m_new) * st_ref[h, pl.ds(rows, MOBA_ROWS), :] + pv
        return st, m_new

    done = []
    for h in range(2):
        st, _ = update(h, j0, True)
        done.append(st / pltpu.roll(st, HEAD_DIM, 1))
    o_ref[pl.ds(j0, MOBA_ROWS), :] = jnp.where(in_head[0], done[0], done[1]).astype(BF16)

    def chunk(t, carry):
        rows = pl.multiple_of(t * MOBA_ROWS, MOBA_ROWS)
        for h in range(2):
            st, m_new = update(h, rows, False)
            st_ref[h, pl.ds(rows, MOBA_ROWS), :] = st
            m_ref[h, pl.ds(rows, MOBA_ROWS), :] = m_new
        return carry

    first = j + 1
    n_later = seq // MOBA_ROWS - first
    lax.fori_loop(0, n_later // 2, lambda i, c: chunk(first + 2 * i + 1, chunk(first + 2 * i, c)), 0)

    @pl.when(n_later % 2 == 1)
    def _():
        chunk(seq // MOBA_ROWS - 1, 0)


def _moba(q, k, v):
    b, hp, s, _ = q.shape
    whole = pl.BlockSpec((None, None, s, LANES), lambda bi, h, j: (bi, h, 0, 0))
    whole_in = pl.BlockSpec((None, None, s, LANES), lambda bi, h, j: (bi, h, 0, 0), pipeline_mode=pl.Buffered(1))
    return pl.pallas_call(
        _moba_kernel,
        grid=(b, hp, s // MOBA_ROWS),
        in_specs=[whole_in, whole_in, whole_in],
        out_specs=whole,
        out_shape=jax.ShapeDtypeStruct((b, hp, s, LANES), BF16),
        scratch_shapes=[pltpu.VMEM((2, s, LANES), BF16),
                        pltpu.VMEM((2, s, LANES), F32),
                        pltpu.VMEM((2, s, LANES), F32)],
        compiler_params=_params("parallel", "parallel", "arbitrary"),
        name="moba",
    )(q, k, v)


def _c_out_kernel(h_ref, o_ref, w_ref, out_ref, cat_ref):
    for hp in range(C_WIDTH // LANES):
        cat_ref[:, hp * LANES:(hp + 1) * LANES] = o_ref[hp]
    out_ref[...] = h_ref[...] + jnp.dot(cat_ref[...], w_ref[...], preferred_element_type=F32)


def _c_out(h, o, w_out):
    b, s, d = h.shape
    hp = C_WIDTH // LANES
    row = pl.BlockSpec((None, PROJ_ROWS, d), lambda bi, i: (bi, i, 0))
    return pl.pallas_call(
        _c_out_kernel,
        grid=(b, s // PROJ_ROWS),
        in_specs=[row, pl.BlockSpec((None, hp, PROJ_ROWS, LANES), lambda bi, i: (bi, 0, i, 0)),
                  _resident((C_WIDTH, d))],
        out_specs=row,
        out_shape=jax.ShapeDtypeStruct((b, s, d), F32),
        scratch_shapes=[pltpu.VMEM((PROJ_ROWS, C_WIDTH), BF16)],
        compiler_params=_params("parallel", "parallel"),
        name="c_out_proj",
    )(h, o, w_out)


def kernel(x, ffn1_norm, ffn1_w_gate, ffn1_w_up, ffn1_w_down, mix_norm, ffn2_norm, ffn2_w_gate, ffn2_w_up,
           ffn2_w_down, ab_w_in, ab_v_norm, ab_w_spatial, ab_b_spatial, ab_w_out, c_w_in, c_w_out, final_norm):
    b, s, d = x.shape
    assert DILATIONS == (1, 4, 16)
    assert d == D_MODEL and s % SWA_TOKENS == 0 and s % MOBA_ROWS == 0 and (b * s) % FFN_ROWS == 0
    assert s // C_BLOCK <= HEAD_DIM
    bf = lambda w: w.astype(BF16)

    def ffn(h, layer, norm, wg, wu, wd, final_g=None):
        return _ffn(h.reshape(b * s, d), norm[layer], wg, wu, wd, layer, final_g).reshape(b, s, d)

    h = x
    h = ffn(h, 0, ffn1_norm, ffn1_w_gate, ffn1_w_up, ffn1_w_down)
    qkvs, b_out = _ab_in(h, mix_norm[0], bf(ab_w_in[0]), ab_v_norm[0], ab_w_spatial[0], ab_b_spatial[0])
    outs, lses = zip(*[_swa(*qkv, dil) for qkv, dil in zip(qkvs, DILATIONS)])
    h = _ab_out(h, outs, lses, b_out, bf(ab_w_out[0]))
    h = ffn(h, 0, ffn2_norm, ffn2_w_gate, ffn2_w_up, ffn2_w_down)
    h = ffn(h, 1, ffn1_norm, ffn1_w_gate, ffn1_w_up, ffn1_w_down)
    qc, kc, vc = _c_in(h, mix_norm[1], bf(c_w_in[0]))
    h = _c_out(h, _moba(qc, kc, vc), bf(c_w_out[0]))
    h = ffn(h, 1, ffn2_norm, ffn2_w_gate, ffn2_w_up, ffn2_w_down, final_g=final_norm)
    return h
```

```python
import functools

import jax
import jax.numpy as jnp
from jax import lax
from jax.experimental import pallas as pl
from jax.experimental.pallas import tpu as pltpu

F32 = jnp.float32
BF16 = jnp.bfloat16

D_MODEL = 1024
D_FF = 2816
EPS = 1e-6
HEAD_DIM = 64
LANES = 128
A_HEADS = 8
A_PATTERNS = ((128, 1), (512, 4), (2048, 16))
A_WIDTH = A_HEADS * HEAD_DIM
DILATIONS = tuple(d for _, d in A_PATTERNS)
B_GROUPS = 4
B_CHUNK = 128
B_WIDTH = B_GROUPS * LANES
C_HEADS = 16
C_WIDTH = C_HEADS * HEAD_DIM
C_BLOCK = 256
C_TOPK = 3
SCALE = HEAD_DIM ** -0.5
LOG2E = 1.4426950408889634
LN2 = 0.6931471805599453
NEG_BIG = -1e30

M_INIT = -(2.0 ** 60)
SCORE_OFF = -(2.0 ** 100)

VMEM_LIMIT_BYTES = 56 * 1024 * 1024

FFN_ROWS = 1024
FFN_CHUNK = 256
PROJ_ROWS = 1024
SWA_TOKENS = 2048
SWA_WIN = 128
MOBA_ROWS = 1024

CONTRACT_LAST = (((1,), (1,)), ((), ()))


def _params(*sem):
    return pltpu.CompilerParams(dimension_semantics=sem, vmem_limit_bytes=VMEM_LIMIT_BYTES)


def _rms_norm(x, g):
    return x * lax.rsqrt(jnp.mean(x * x, axis=-1, keepdims=True) + EPS) * g


def _resident(shape):
    nd = len(shape)
    return pl.BlockSpec(shape, lambda *_: (0,) * nd, pipeline_mode=pl.Buffered(1))


FFN_WEIGHT_STEPS = D_FF // FFN_CHUNK


def _ffn_kernel(x_ref, g_ref, wg_ref, wu_ref, wd_ref, *rest, final):
    if final:
        gf_ref, o_ref, wg_bf, wu_bf, wd_bf, xn_ref, acc_ref = rest
    else:
        o_ref, wg_bf, wu_bf, wd_bf, xn_ref, acc_ref = rest
    i = pl.program_id(0)

    @pl.when(i < FFN_WEIGHT_STEPS)
    def _():
        off = pl.multiple_of(i * FFN_CHUNK, FFN_CHUNK)
        wg_bf[:, pl.ds(off, FFN_CHUNK)] = wg_ref[...].astype(BF16)
        wu_bf[:, pl.ds(off, FFN_CHUNK)] = wu_ref[...].astype(BF16)
        wd_bf[pl.ds(off, FFN_CHUNK), :] = wd_ref[...].astype(BF16)

    @pl.when(i >= FFN_WEIGHT_STEPS)
    def _():
        xn_ref[...] = _rms_norm(x_ref[...], g_ref[...]).astype(BF16)
        acc_ref[...] = jnp.zeros_like(acc_ref)

        def body(c, carry):
            off = pl.multiple_of(c * FFN_CHUNK, FFN_CHUNK)
            xn = xn_ref[...]
            gate = jnp.dot(xn, wg_bf[:, pl.ds(off, FFN_CHUNK)], preferred_element_type=F32)
            up = jnp.dot(xn, wu_bf[:, pl.ds(off, FFN_CHUNK)], preferred_element_type=F32)
            act = (jax.nn.silu(gate) * up).astype(BF16)
            acc_ref[...] += jnp.dot(act, wd_bf[pl.ds(off, FFN_CHUNK), :], preferred_element_type=F32)
            return carry

        lax.fori_loop(0, FFN_WEIGHT_STEPS, body, 0, unroll=True)
        y = x_ref[...] + 0.5 * acc_ref[...]
        if final:
            y = _rms_norm(y, gf_ref[...])
        o_ref[...] = y


def _ffn(x2, g, wg, wu, wd, layer, final_g=None):
    t, d = x2.shape
    final = final_g is not None
    row_spec = pl.BlockSpec((FFN_ROWS, d), lambda i: (jnp.maximum(i - FFN_WEIGHT_STEPS, 0), 0))
    col_chunk = pl.BlockSpec((None, d, FFN_CHUNK), lambda i: (layer, 0, jnp.minimum(i, FFN_WEIGHT_STEPS - 1)))
    row_chunk = pl.BlockSpec((None, FFN_CHUNK, d), lambda i: (layer, jnp.minimum(i, FFN_WEIGHT_STEPS - 1), 0))
    in_specs = [row_spec, _resident((1, d)), col_chunk, col_chunk, row_chunk]
    args = [x2, g.reshape(1, d), wg, wu, wd]
    if final:
        in_specs.append(_resident((1, d)))
        args.append(final_g.reshape(1, d))
    return pl.pallas_call(
        functools.partial(_ffn_kernel, final=final),
        grid=(FFN_WEIGHT_STEPS + t // FFN_ROWS,),
        in_specs=in_specs,
        out_specs=row_spec,
        out_shape=jax.ShapeDtypeStruct((t, d), F32),
        scratch_shapes=[pltpu.VMEM((d, D_FF), BF16), pltpu.VMEM((d, D_FF), BF16), pltpu.VMEM((D_FF, d), BF16),
                        pltpu.VMEM((FFN_ROWS, d), BF16), pltpu.VMEM((FFN_ROWS, d), F32)],
        compiler_params=_params("arbitrary"),
        name="ffn_final" if final else "ffn",
    )(*args)


def _ab_in_kernel(h_ref, g_ref, w_ref, vn_ref, ws_ref, bs_ref, *rest):
    qkv_refs, b_ref, z1_ref, z4_ref = rest[:3 * len(DILATIONS)], rest[-3], rest[-2], rest[-1]
    n_hp = A_WIDTH // LANES
    hn = _rms_norm(h_ref[...], g_ref[...]).astype(BF16)

    def project(col, width):
        return jnp.dot(hn, w_ref[:, col:col + width], preferred_element_type=F32)

    for t in range(3):
        z = project(t * A_WIDTH, A_WIDTH)
        if t == 0:
            z = z * (SCALE * LOG2E)
        for hp in range(n_hp):
            x1 = z[:, hp * LANES:(hp + 1) * LANES]
            qkv_refs[t][hp] = x1.astype(BF16)
            z1_ref[t * n_hp + hp] = x1
        for hp in range(n_hp):
            for r1 in range(4):
                x4 = z1_ref[t * n_hp + hp, pl.ds(r1, PROJ_ROWS // 4, stride=4), :]
                qkv_refs[3 + t][hp, :, r1 * LANES:(r1 + 1) * LANES] = x4.astype(BF16)
                z4_ref[(t * n_hp + hp) * 4 + r1] = x4
        for hp in range(n_hp):
            for r1 in range(4):
                for r2 in range(4):
                    x16 = z4_ref[(t * n_hp + hp) * 4 + r1, pl.ds(r2, PROJ_ROWS // 16, stride=4), :]
                    r = 4 * r2 + r1
                    qkv_refs[6 + t][hp, :, r * LANES:(r + 1) * LANES] = x16.astype(BF16)
    row = lax.broadcasted_iota(jnp.int32, (B_CHUNK, B_CHUNK), 0)
    col = lax.broadcasted_iota(jnp.int32, (B_CHUNK, B_CHUNK), 1)
    causal = row >= col
    zu = project(3 * A_WIDTH, B_WIDTH)
    zv = project(3 * A_WIDTH + B_WIDTH, B_WIDTH)
    for g in range(B_GROUPS):
        lo = g * LANES
        u = jax.nn.gelu(zu[:, lo:lo + LANES])
        v = jax.nn.gelu(zv[:, lo:lo + LANES])
        vn = _rms_norm(v, vn_ref[:, lo:lo + LANES]).astype(BF16)
        ws = jnp.where(causal, ws_ref[g], 0.0).astype(BF16)
        bias = bs_ref[:, g:g + 1]
        for c in range(PROJ_ROWS // B_CHUNK):
            r0 = c * B_CHUNK
            mixed = jnp.dot(ws, vn[r0:r0 + B_CHUNK], preferred_element_type=F32) + bias
            b_ref[r0:r0 + B_CHUNK, lo:lo + LANES] = (u[r0:r0 + B_CHUNK] * mixed).astype(BF16)


def _dilated_spec(hp, d):
    return pl.BlockSpec((None, hp, PROJ_ROWS // d, d * LANES), lambda bi, i: (bi, 0, i, 0))


def _ab_in(h, g, w_in, v_norm, w_s, b_s):
    b, s, d = h.shape
    hp = A_WIDTH // LANES
    width = w_in.shape[1]
    qkv_specs = [_dilated_spec(hp, dil) for dil in DILATIONS for _ in range(3)]
    qkv_shapes = [jax.ShapeDtypeStruct((b, hp, s // dil, dil * LANES), BF16) for dil in DILATIONS for _ in range(3)]
    outs = pl.pallas_call(
        _ab_in_kernel,
        grid=(b, s // PROJ_ROWS),
        in_specs=[
            pl.BlockSpec((None, PROJ_ROWS, d), lambda bi, i: (bi, i, 0)),
            _resident((1, d)),
            _resident((d, width)),
            _resident((1, B_WIDTH)),
            _resident((B_GROUPS, B_CHUNK, B_CHUNK)),
            _resident((B_CHUNK, B_GROUPS)),
        ],
        out_specs=qkv_specs + [pl.BlockSpec((None, PROJ_ROWS, B_WIDTH), lambda bi, i: (bi, i, 0))],
        out_shape=qkv_shapes + [jax.ShapeDtypeStruct((b, s, B_WIDTH), BF16)],
        scratch_shapes=[pltpu.VMEM((3 * hp, PROJ_ROWS, LANES), F32),
                        pltpu.VMEM((3 * hp * 4, PROJ_ROWS // 4, LANES), F32)],
        compiler_params=_params("parallel", "parallel"),
        name="ab_in_proj",
    )(h, g.reshape(1, d), w_in, v_norm.reshape(1, B_WIDTH), w_s, b_s.T)
    return [outs[3 * i:3 * i + 3] for i in range(len(DILATIONS))], outs[-1]


def _swa_kernel(q_ref, kp_ref, kc_ref, vp_ref, vc_ref, o_ref, lse_ref, kk_ref, vv_ref, *, rows, res):
    n = pl.program_id(3)
    nsub = rows // SWA_WIN
    kk_ref[0:SWA_WIN] = kp_ref[...]
    kk_ref[SWA_WIN:] = kc_ref[...]
    vv_ref[0:SWA_WIN] = vp_ref[...]
    vv_ref[SWA_WIN:] = vc_ref[...]
    cols = [slice(r * LANES, (r + 1) * LANES) for r in range(res)]
    q = jnp.concatenate([q_ref[:, c].reshape(nsub, SWA_WIN, LANES) for c in cols], axis=0)
    kwin = jnp.stack([kk_ref[j * SWA_WIN:(j + 2) * SWA_WIN, c] for c in cols for j in range(nsub)])
    vwin = jnp.stack([vv_ref[j * SWA_WIN:(j + 2) * SWA_WIN, c] for c in cols for j in range(nsub)])
    qi = lax.broadcasted_iota(jnp.int32, (SWA_WIN, 2 * SWA_WIN), 0)
    kc = lax.broadcasted_iota(jnp.int32, (SWA_WIN, 2 * SWA_WIN), 1)
    band = (kc >= qi) & (kc <= qi + SWA_WIN)
    bias = jnp.where(band, 0.0, NEG_BIG)
    bias_first = jnp.where(n == 0, jnp.where(band & (kc >= SWA_WIN), 0.0, NEG_BIG), bias)
    bias = jnp.stack([bias_first if j == 0 else bias for _ in cols for j in range(nsub)])
    q_head0 = lax.broadcasted_iota(jnp.int32, q.shape, 2) < HEAD_DIM
    zero = jnp.zeros_like(q)
    v_ext = jnp.concatenate([vwin, jnp.ones_like(vwin)], axis=-1)
    ms, pvs, dens = [], [], []
    for qm in (jnp.where(q_head0, q, zero), jnp.where(q_head0, zero, q)):
        s = jnp.einsum("bqd,bkd->bqk", qm, kwin, preferred_element_type=F32) + bias
        m = jnp.max(s, axis=-1, keepdims=True)
        p = jnp.exp2(s - m).astype(BF16)
        pv = jnp.einsum("bqk,bkd->bqd", p, v_ext, preferred_element_type=F32)
        ms.append(m)
        pvs.append(pv[..., :LANES])
        dens.append(pv[..., LANES:])
    den = jnp.where(q_head0, dens[0], dens[1])
    out = jnp.where(q_head0, pvs[0], pvs[1]) / den
    lse = jnp.where(q_head0, ms[0], ms[1]) * LN2 + jnp.log(den)
    for r, c in enumerate(cols):
        o_ref[:, c] = out[r * nsub:(r + 1) * nsub].reshape(rows, LANES).astype(BF16)
        lse_ref[:, c] = lse[r * nsub:(r + 1) * nsub].reshape(rows, LANES)


def _swa(q, k, v, dilation):
    b, hp, sub, _ = q.shape
    res = min(dilation, SWA_TOKENS // SWA_WIN)
    rows = min(sub, SWA_TOKENS // res)
    per_step = rows // SWA_WIN
    cur = pl.BlockSpec((None, None, rows, res * LANES), lambda bi, h, r, n: (bi, h, n, r))
    prev = pl.BlockSpec((None, None, SWA_WIN, res * LANES),
                        lambda bi, h, r, n: (bi, h, jnp.maximum(n * per_step - 1, 0), r))
    return pl.pallas_call(
        functools.partial(_swa_kernel, rows=rows, res=res),
        grid=(b, hp, dilation // res, sub // rows),
        in_specs=[cur, prev, cur, prev, cur],
        out_specs=[cur, cur],
        out_shape=[jax.ShapeDtypeStruct(q.shape, BF16), jax.ShapeDtypeStruct(q.shape, F32)],
        scratch_shapes=[pltpu.VMEM((rows + SWA_WIN, res * LANES), BF16),
                        pltpu.VMEM((rows + SWA_WIN, res * LANES), BF16)],
        compiler_params=_params("parallel", "parallel", "parallel", "parallel"),
        name=f"swa_d{dilation}",
    )(q, k, k, v, v)


def _ab_out_kernel(h_ref, *rest):
    n_pat = len(DILATIONS)
    o_refs, l_refs = rest[:n_pat], rest[n_pat:2 * n_pat]
    b_ref, w_ref, out_ref, cat_ref, tok_ref = rest[2 * n_pat:]
    for hp in range(A_WIDTH // LANES):
        outs, lses = [], []
        for i, d in enumerate(DILATIONS):
            vals = []
            for kind, ref in enumerate((o_refs[i], l_refs[i])):
                if d == 1:
                    vals.append(ref[hp].astype(F32))
                    continue
                slot = (hp * n_pat + i) * 2 + kind
                for r in range(d):
                    tok_ref[slot, pl.ds(r, PROJ_ROWS // d, stride=d), :] = (
                        ref[hp, :, r * LANES:(r + 1) * LANES].astype(F32))
                vals.append(tok_ref[slot])
            outs.append(vals[0])
            lses.append(vals[1])
        m = functools.reduce(jnp.maximum, lses)
        es = [jnp.exp(l - m) for l in lses]
        mix = sum(e * o for e, o in zip(es, outs)) / sum(es)
        cat_ref[:, hp * LANES:(hp + 1) * LANES] = mix.astype(BF16)
    cat_ref[:, A_WIDTH:] = b_ref[...]
    out_ref[...] = h_ref[...] + jnp.dot(cat_ref[...], w_ref[...], preferred_element_type=F32)


def _ab_out(h, outs, lses, b_out, w_out):
    b, s, d = h.shape
    hp = A_WIDTH // LANES
    row = pl.BlockSpec((None, PROJ_ROWS, d), lambda bi, i: (bi, i, 0))
    heads = [_dilated_spec(hp, dil) for dil in DILATIONS]
    return pl.pallas_call(
        _ab_out_kernel,
        grid=(b, s // PROJ_ROWS),
        in_specs=[row] + heads + heads + [pl.BlockSpec((None, PROJ_ROWS, B_WIDTH), lambda bi, i: (bi, i, 0)),
                                          _resident((A_WIDTH + B_WIDTH, d))],
        out_specs=row,
        out_shape=jax.ShapeDtypeStruct((b, s, d), F32),
        scratch_shapes=[pltpu.VMEM((PROJ_ROWS, A_WIDTH + B_WIDTH), BF16),
                        pltpu.VMEM((hp * len(DILATIONS) * 2, PROJ_ROWS, LANES), F32)],
        compiler_params=_params("parallel", "parallel"),
        name="ab_out_proj",
    )(h, *outs, *lses, b_out, w_out)


def _c_in_kernel(h_ref, g_ref, w_ref, q_ref, k_ref, v_ref):
    hn = _rms_norm(h_ref[...], g_ref[...]).astype(BF16)
    z = jnp.dot(hn, w_ref[...], preferred_element_type=F32)
    for hp in range(C_WIDTH // LANES):
        lo = hp * LANES
        q_ref[hp] = (z[:, lo:lo + LANES] * (SCALE * LOG2E)).astype(BF16)
        k_ref[hp] = z[:, C_WIDTH + lo:C_WIDTH + lo + LANES].astype(BF16)
        v_ref[hp] = z[:, 2 * C_WIDTH + lo:2 * C_WIDTH + lo + LANES].astype(BF16)


def _c_in(h, g, w_in):
    b, s, d = h.shape
    hp = C_WIDTH // LANES
    spec = pl.BlockSpec((None, hp, PROJ_ROWS, LANES), lambda bi, i: (bi, 0, i, 0))
    shape = jax.ShapeDtypeStruct((b, hp, s, LANES), BF16)
    return pl.pallas_call(
        _c_in_kernel,
        grid=(b, s // PROJ_ROWS),
        in_specs=[pl.BlockSpec((None, PROJ_ROWS, d), lambda bi, i: (bi, i, 0)),
                  _resident((1, d)), _resident((d, 3 * C_WIDTH))],
        out_specs=[spec, spec, spec],
        out_shape=[shape, shape, shape],
        compiler_params=_params("parallel", "parallel"),
        name="c_in_proj",
    )(h, g.reshape(1, d), w_in)


def _moba_select(q_ref, k_ref, qaug_ref, st_ref, m_ref, nb):
    seq = q_ref.shape[0]
    kmean = jnp.concatenate(
        [jnp.mean(k_ref[j * C_BLOCK:(j + 1) * C_BLOCK].astype(F32), axis=0, keepdims=True) for j in range(nb)],
        axis=0).astype(BF16)
    lane = lax.broadcasted_iota(jnp.int32, (MOBA_ROWS, LANES), 1)
    blk = lax.broadcasted_iota(jnp.int32, (nb, MOBA_ROWS), 0)
    qpos = lax.broadcasted_iota(jnp.int32, (nb, MOBA_ROWS), 1)
    zeros = lambda n: jnp.zeros((n, MOBA_ROWS), F32)

    def chunk(t, carry):
        r0 = pl.multiple_of(t * MOBA_ROWS, MOBA_ROWS)
        q = q_ref[pl.ds(r0, MOBA_ROWS), :]
        own = (qpos + r0) // C_BLOCK
        past = blk < own
        for h in range(2):
            qm = jnp.where(lane < HEAD_DIM if h == 0 else lane >= HEAD_DIM, q, jnp.zeros_like(q))
            gate = lax.dot_general(kmean, qm, CONTRACT_LAST, preferred_element_type=F32)
            gate = jnp.where(past, gate, -jnp.inf)
            chosen = jnp.zeros(gate.shape, jnp.bool_)
            for _ in range(C_TOPK):
                best = jnp.max(gate, axis=0, keepdims=True)
                first = jnp.min(jnp.where(gate == best, blk, nb), axis=0, keepdims=True)
                hit = blk == first
                chosen = chosen | hit
                gate = jnp.where(hit, -jnp.inf, gate)
            unselected = 1.0 - ((chosen & past) | (blk == own)).astype(F32)
            if h == 0:
                extra_t = jnp.concatenate([zeros(HEAD_DIM), unselected, zeros(HEAD_DIM - nb)], axis=0)
            else:
                extra_t = jnp.concatenate([unselected, zeros(LANES - nb)], axis=0)
            qaug_ref[h, pl.ds(r0, MOBA_ROWS), :] = (qm.astype(F32) + extra_t.T).astype(BF16)
        return carry

    lax.fori_loop(0, seq // MOBA_ROWS, chunk, 0)
    st_ref[...] = jnp.zeros_like(st_ref)
    m_ref[...] = jnp.full(m_ref.shape, M_INIT, F32)


def _moba_kernel(q_ref, k_ref, v_ref, o_ref, qaug_ref, st_ref, m_ref):
    j = pl.program_id(2)
    seq = q_ref.shape[0]
    nb = seq // C_BLOCK

    @pl.when(j == 0)
    def _():
        _moba_select(q_ref, k_ref, qaug_ref, st_ref, m_ref, nb)

    j0 = pl.multiple_of(j * MOBA_ROWS, MOBA_ROWS)
    kj = k_ref[pl.ds(j0, MOBA_ROWS), :]
    vj = v_ref[pl.ds(j0, MOBA_ROWS), :]
    lane = lax.broadcasted_iota(jnp.int32, (MOBA_ROWS, LANES), 1)
    key_blk = j * (MOBA_ROWS // C_BLOCK) + lax.broadcasted_iota(jnp.int32, (MOBA_ROWS, LANES), 0) // C_BLOCK
    in_head = [lane < HEAD_DIM, lane >= HEAD_DIM]
    flag_lane = [HEAD_DIM + key_blk, key_blk]
    k_aug = [jnp.where(in_head[h], kj, jnp.where(lane == flag_lane[h], SCORE_OFF, 0.0).astype(BF16))
             for h in range(2)]
    v_aug = [jnp.where(in_head[h], vj, jnp.ones_like(vj)) for h in range(2)]

    def update(h, rows, causal):
        qa = qaug_ref[h, pl.ds(rows, MOBA_ROWS), :]
        s = lax.dot_general(qa, k_aug[h], CONTRACT_LAST, preferred_element_type=F32)
        if causal:
            qi = lax.broadcasted_iota(jnp.int32, s.shape, 0)
            ki = lax.broadcasted_iota(jnp.int32, s.shape, 1)
            s = jnp.where(ki <= qi, s, SCORE_OFF)
        m_old = m_ref[h, pl.ds(rows, MOBA_ROWS), :]
        m_new = jnp.maximum(m_old, jnp.max(s, axis=-1, keepdims=True))
        p = jnp.exp2(s - jnp.concatenate([m_new] * (MOBA_ROWS // LANES), axis=1))
        pv = jnp.dot(p.astype(BF16), v_aug[h], preferred_element_type=F32)
        st = jnp.exp2(m_old - m_new) * st_ref[h, pl.ds(rows, MOBA_ROWS), :] + pv
        return st, m_new

    done = []
    for h in range(2):
        st, _ = update(h, j0, True)
        done.append(st / pltpu.roll(st, HEAD_DIM, 1))
    o_ref[pl.ds(j0, MOBA_ROWS), :] = jnp.where(in_head[0], done[0], done[1]).astype(BF16)

    def chunk(t, carry):
        rows = pl.multiple_of(t * MOBA_ROWS, MOBA_ROWS)
        for h in range(2):
            st, m_new = update(h, rows, False)
            st_ref[h, pl.ds(rows, MOBA_ROWS), :] = st
            m_ref[h, pl.ds(rows, MOBA_ROWS), :] = m_new
        return carry

    first = j + 1
    n_later = seq // MOBA_ROWS - first
    lax.fori_loop(0, n_later // 2, lambda i, c: chunk(first + 2 * i + 1, chunk(first + 2 * i, c)), 0)

    @pl.when(n_later % 2 == 1)
    def _():
        chunk(seq // MOBA_ROWS - 1, 0)


def _moba(q, k, v):
    b, hp, s, _ = q.shape
    whole = pl.BlockSpec((None, None, s, LANES), lambda bi, h, j: (bi, h, 0, 0))
    whole_in = pl.BlockSpec((None, None, s, LANES), lambda bi, h, j: (bi, h, 0, 0), pipeline_mode=pl.Buffered(1))
    return pl.pallas_call(
        _moba_kernel,
        grid=(b, hp, s // MOBA_ROWS),
        in_specs=[whole_in, whole_in, whole_in],
        out_specs=whole,
        out_shape=jax.ShapeDtypeStruct((b, hp, s, LANES), BF16),
        scratch_shapes=[pltpu.VMEM((2, s, LANES), BF16),
                        pltpu.VMEM((2, s, LANES), F32),
                        pltpu.VMEM((2, s, LANES), F32)],
        compiler_params=_params("parallel", "parallel", "arbitrary"),
        name="moba",
    )(q, k, v)


def _c_out_kernel(h_ref, o_ref, w_ref, out_ref, cat_ref):
    for hp in range(C_WIDTH // LANES):
        cat_ref[:, hp * LANES:(hp + 1) * LANES] = o_ref[hp]
    out_ref[...] = h_ref[...] + jnp.dot(cat_ref[...], w_ref[...], preferred_element_type=F32)


def _c_out(h, o, w_out):
    b, s, d = h.shape
    hp = C_WIDTH // LANES
    row = pl.BlockSpec((None, PROJ_ROWS, d), lambda bi, i: (bi, i, 0))
    return pl.pallas_call(
        _c_out_kernel,
        grid=(b, s // PROJ_ROWS),
        in_specs=[row, pl.BlockSpec((None, hp, PROJ_ROWS, LANES), lambda bi, i: (bi, 0, i, 0)),
                  _resident((C_WIDTH, d))],
        out_specs=row,
        out_shape=jax.ShapeDtypeStruct((b, s, d), F32),
        scratch_shapes=[pltpu.VMEM((PROJ_ROWS, C_WIDTH), BF16)],
        compiler_params=_params("parallel", "parallel"),
        name="c_out_proj",
    )(h, o, w_out)


def kernel(x, ffn1_norm, ffn1_w_gate, ffn1_w_up, ffn1_w_down, mix_norm, ffn2_norm, ffn2_w_gate, ffn2_w_up,
           ffn2_w_down, ab_w_in, ab_v_norm, ab_w_spatial, ab_b_spatial, ab_w_out, c_w_in, c_w_out, final_norm):
    b, s, d = x.shape
    assert DILATIONS == (1, 4, 16)
    assert d == D_MODEL and s % SWA_TOKENS == 0 and s % MOBA_ROWS == 0 and (b * s) % FFN_ROWS == 0
    assert s // C_BLOCK <= HEAD_DIM
    bf = lambda w: w.astype(BF16)

    def ffn(h, layer, norm, wg, wu, wd, final_g=None):
        return _ffn(h.reshape(b * s, d), norm[layer], wg, wu, wd, layer, final_g).reshape(b, s, d)

    h = x
    h = ffn(h, 0, ffn1_norm, ffn1_w_gate, ffn1_w_up, ffn1_w_down)
    qkvs, b_out = _ab_in(h, mix_norm[0], bf(ab_w_in[0]), ab_v_norm[0], ab_w_spatial[0], ab_b_spatial[0])
    outs, lses = zip(*[_swa(*qkv, dil) for qkv, dil in zip(qkvs, DILATIONS)])
    h = _ab_out(h, outs, lses, b_out, bf(ab_w_out[0]))
    h = ffn(h, 0, ffn2_norm, ffn2_w_gate, ffn2_w_up, ffn2_w_down)
    h = ffn(h, 1, ffn1_norm, ffn1_w_gate, ffn1_w_up, ffn1_w_down)
    qc, kc, vc = _c_in(h, mix_norm[1], bf(c_w_in[0]))
    h = _c_out(h, _moba(qc, kc, vc), bf(c_w_out[0]))
    h = ffn(h, 1, ffn2_norm, ffn2_w_gate, ffn2_w_up, ffn2_w_down, final_g=final_norm)
    return h
```

```python
import functools

import jax
import jax.numpy as jnp
from jax import lax
from jax.experimental import pallas as pl
from jax.experimental.pallas import tpu as pltpu

F32 = jnp.float32
BF16 = jnp.bfloat16

D_MODEL = 1024
D_FF = 2816
EPS = 1e-6
HEAD_DIM = 64
LANES = 128
A_HEADS = 8
A_PATTERNS = ((128, 1), (512, 4), (2048, 16))
A_WIDTH = A_HEADS * HEAD_DIM
DILATIONS = tuple(d for _, d in A_PATTERNS)
B_GROUPS = 4
B_CHUNK = 128
B_WIDTH = B_GROUPS * LANES
C_HEADS = 16
C_WIDTH = C_HEADS * HEAD_DIM
C_BLOCK = 256
C_TOPK = 3
SCALE = HEAD_DIM ** -0.5
LOG2E = 1.4426950408889634
LN2 = 0.6931471805599453
NEG_BIG = -1e30

M_INIT = -(2.0 ** 60)
SCORE_OFF = -(2.0 ** 100)

VMEM_LIMIT_BYTES = 56 * 1024 * 1024

FFN_ROWS = 1024
FFN_CHUNK = 256
PROJ_ROWS = 1024
SWA_TOKENS = 4096
SWA_WIN = 128
MOBA_ROWS = 1024

CONTRACT_LAST = (((1,), (1,)), ((), ()))


def _params(*sem):
    return pltpu.CompilerParams(dimension_semantics=sem, vmem_limit_bytes=VMEM_LIMIT_BYTES)


def _rms_norm(x, g):
    return x * lax.rsqrt(jnp.mean(x * x, axis=-1, keepdims=True) + EPS) * g


def _resident(shape):
    nd = len(shape)
    return pl.BlockSpec(shape, lambda *_: (0,) * nd, pipeline_mode=pl.Buffered(1))


FFN_WEIGHT_STEPS = D_FF // FFN_CHUNK


def _ffn_kernel(x_ref, g_ref, wg_ref, wu_ref, wd_ref, *rest, final):
    if final:
        gf_ref, o_ref, wg_bf, wu_bf, wd_bf, xn_ref, acc_ref = rest
    else:
        o_ref, wg_bf, wu_bf, wd_bf, xn_ref, acc_ref = rest
    i = pl.program_id(0)

    def start():
        xn_ref[...] = _rms_norm(x_ref[...], g_ref[...]).astype(BF16)
        acc_ref[...] = jnp.zeros_like(acc_ref)

    def apply_chunk(c, carry=0):
        off = pl.multiple_of(c * FFN_CHUNK, FFN_CHUNK)
        xn = xn_ref[...]
        gate = jnp.dot(xn, wg_bf[:, pl.ds(off, FFN_CHUNK)], preferred_element_type=F32)
        up = jnp.dot(xn, wu_bf[:, pl.ds(off, FFN_CHUNK)], preferred_element_type=F32)
        act = (jax.nn.silu(gate) * up).astype(BF16)
        acc_ref[...] += jnp.dot(act, wd_bf[pl.ds(off, FFN_CHUNK), :], preferred_element_type=F32)
        return carry

    def finish():
        y = x_ref[...] + 0.5 * acc_ref[...]
        if final:
            y = _rms_norm(y, gf_ref[...])
        o_ref[...] = y

    @pl.when(i < FFN_WEIGHT_STEPS)
    def _():
        off = pl.multiple_of(i * FFN_CHUNK, FFN_CHUNK)
        wg_bf[:, pl.ds(off, FFN_CHUNK)] = wg_ref[...].astype(BF16)
        wu_bf[:, pl.ds(off, FFN_CHUNK)] = wu_ref[...].astype(BF16)
        wd_bf[pl.ds(off, FFN_CHUNK), :] = wd_ref[...].astype(BF16)
        pl.when(i == 0)(start)
        apply_chunk(i)
        pl.when(i == FFN_WEIGHT_STEPS - 1)(finish)

    @pl.when(i >= FFN_WEIGHT_STEPS)
    def _():
        start()
        lax.fori_loop(0, FFN_WEIGHT_STEPS, apply_chunk, 0, unroll=True)
        finish()


def _ffn(x2, g, wg, wu, wd, layer, final_g=None):
    t, d = x2.shape
    final = final_g is not None
    row_spec = pl.BlockSpec((FFN_ROWS, d), lambda i: (jnp.maximum(i - (FFN_WEIGHT_STEPS - 1), 0), 0))
    col_chunk = pl.BlockSpec((None, d, FFN_CHUNK), lambda i: (layer, 0, jnp.minimum(i, FFN_WEIGHT_STEPS - 1)))
    row_chunk = pl.BlockSpec((None, FFN_CHUNK, d), lambda i: (layer, jnp.minimum(i, FFN_WEIGHT_STEPS - 1), 0))
    in_specs = [row_spec, _resident((1, d)), col_chunk, col_chunk, row_chunk]
    args = [x2, g.reshape(1, d), wg, wu, wd]
    if final:
        in_specs.append(_resident((1, d)))
        args.append(final_g.reshape(1, d))
    return pl.pallas_call(
        functools.partial(_ffn_kernel, final=final),
        grid=(FFN_WEIGHT_STEPS - 1 + t // FFN_ROWS,),
        in_specs=in_specs,
        out_specs=row_spec,
        out_shape=jax.ShapeDtypeStruct((t, d), F32),
        scratch_shapes=[pltpu.VMEM((d, D_FF), BF16), pltpu.VMEM((d, D_FF), BF16), pltpu.VMEM((D_FF, d), BF16),
                        pltpu.VMEM((FFN_ROWS, d), BF16), pltpu.VMEM((FFN_ROWS, d), F32)],
        compiler_params=_params("arbitrary"),
        name="ffn_final" if final else "ffn",
    )(*args)


def _ab_in_kernel(h_ref, g_ref, w_ref, vn_ref, ws_ref, bs_ref, *rest):
    qkv_refs, b_ref, z1_ref, z4_ref = rest[:3 * len(DILATIONS)], rest[-3], rest[-2], rest[-1]
    n_hp = A_WIDTH // LANES
    hn = _rms_norm(h_ref[...], g_ref[...]).astype(BF16)

    def project(col, width):
        return jnp.dot(hn, w_ref[:, col:col + width], preferred_element_type=F32)

    for t in range(3):
        z = project(t * A_WIDTH, A_WIDTH)
        if t == 0:
            z = z * (SCALE * LOG2E)
        for hp in range(n_hp):
            x1 = z[:, hp * LANES:(hp + 1) * LANES]
            qkv_refs[t][hp] = x1.astype(BF16)
            z1_ref[t * n_hp + hp] = x1
        for hp in range(n_hp):
            for r1 in range(4):
                x4 = z1_ref[t * n_hp + hp, pl.ds(r1, PROJ_ROWS // 4, stride=4), :]
                qkv_refs[3 + t][hp, :, r1 * LANES:(r1 + 1) * LANES] = x4.astype(BF16)
                z4_ref[(t * n_hp + hp) * 4 + r1] = x4
        for hp in range(n_hp):
            for r1 in range(4):
                for r2 in range(4):
                    x16 = z4_ref[(t * n_hp + hp) * 4 + r1, pl.ds(r2, PROJ_ROWS // 16, stride=4), :]
                    r = 4 * r2 + r1
                    qkv_refs[6 + t][hp, :, r * LANES:(r + 1) * LANES] = x16.astype(BF16)
    row = lax.broadcasted_iota(jnp.int32, (B_CHUNK, B_CHUNK), 0)
    col = lax.broadcasted_iota(jnp.int32, (B_CHUNK, B_CHUNK), 1)
    causal = row >= col
    zu = project(3 * A_WIDTH, B_WIDTH)
    zv = project(3 * A_WIDTH + B_WIDTH, B_WIDTH)
    for g in range(B_GROUPS):
        lo = g * LANES
        u = jax.nn.gelu(zu[:, lo:lo + LANES])
        v = jax.nn.gelu(zv[:, lo:lo + LANES])
        vn = _rms_norm(v, vn_ref[:, lo:lo + LANES]).astype(BF16)
        ws = jnp.where(causal, ws_ref[g], 0.0).astype(BF16)
        bias = bs_ref[:, g:g + 1]
        for c in range(PROJ_ROWS // B_CHUNK):
            r0 = c * B_CHUNK
            mixed = jnp.dot(ws, vn[r0:r0 + B_CHUNK], preferred_element_type=F32) + bias
            b_ref[r0:r0 + B_CHUNK, lo:lo + LANES] = (u[r0:r0 + B_CHUNK] * mixed).astype(BF16)


def _dilated_spec(hp, d):
    return pl.BlockSpec((None, hp, PROJ_ROWS // d, d * LANES), lambda bi, i: (bi, 0, i, 0))


def _ab_in(h, g, w_in, v_norm, w_s, b_s):
    b, s, d = h.shape
    hp = A_WIDTH // LANES
    width = w_in.shape[1]
    qkv_specs = [_dilated_spec(hp, dil) for dil in DILATIONS for _ in range(3)]
    qkv_shapes = [jax.ShapeDtypeStruct((b, hp, s // dil, dil * LANES), BF16) for dil in DILATIONS for _ in range(3)]
    outs = pl.pallas_call(
        _ab_in_kernel,
        grid=(b, s // PROJ_ROWS),
        in_specs=[
            pl.BlockSpec((None, PROJ_ROWS, d), lambda bi, i: (bi, i, 0)),
            _resident((1, d)),
            _resident((d, width)),
            _resident((1, B_WIDTH)),
            _resident((B_GROUPS, B_CHUNK, B_CHUNK)),
            _resident((B_CHUNK, B_GROUPS)),
        ],
        out_specs=qkv_specs + [pl.BlockSpec((None, PROJ_ROWS, B_WIDTH), lambda bi, i: (bi, i, 0))],
        out_shape=qkv_shapes + [jax.ShapeDtypeStruct((b, s, B_WIDTH), BF16)],
        scratch_shapes=[pltpu.VMEM((3 * hp, PROJ_ROWS, LANES), F32),
                        pltpu.VMEM((3 * hp * 4, PROJ_ROWS // 4, LANES), F32)],
        compiler_params=_params("parallel", "parallel"),
        name="ab_in_proj",
    )(h, g.reshape(1, d), w_in, v_norm.reshape(1, B_WIDTH), w_s, b_s.T)
    return [outs[3 * i:3 * i + 3] for i in range(len(DILATIONS))], outs[-1]


def _swa_kernel(q_ref, kp_ref, kc_ref, vp_ref, vc_ref, o_ref, lse_ref, kk_ref, vv_ref, *, rows, res):
    n = pl.program_id(3)
    nsub = rows // SWA_WIN
    kk_ref[0:SWA_WIN] = kp_ref[...]
    kk_ref[SWA_WIN:] = kc_ref[...]
    vv_ref[0:SWA_WIN] = vp_ref[...]
    vv_ref[SWA_WIN:] = vc_ref[...]
    cols = [slice(r * LANES, (r + 1) * LANES) for r in range(res)]
    q = jnp.concatenate([q_ref[:, c].reshape(nsub, SWA_WIN, LANES) for c in cols], axis=0)
    kwin = jnp.stack([kk_ref[j * SWA_WIN:(j + 2) * SWA_WIN, c] for c in cols for j in range(nsub)])
    vwin = jnp.stack([vv_ref[j * SWA_WIN:(j + 2) * SWA_WIN, c] for c in cols for j in range(nsub)])
    qi = lax.broadcasted_iota(jnp.int32, (SWA_WIN, 2 * SWA_WIN), 0)
    kc = lax.broadcasted_iota(jnp.int32, (SWA_WIN, 2 * SWA_WIN), 1)
    band = (kc >= qi) & (kc <= qi + SWA_WIN)
    bias = jnp.where(band, 0.0, NEG_BIG)
    bias_first = jnp.where(n == 0, jnp.where(band & (kc >= SWA_WIN), 0.0, NEG_BIG), bias)
    bias = jnp.stack([bias_first if j == 0 else bias for _ in cols for j in range(nsub)])
    q_head0 = lax.broadcasted_iota(jnp.int32, q.shape, 2) < HEAD_DIM
    zero = jnp.zeros_like(q)
    v_ext = jnp.concatenate([vwin, jnp.ones_like(vwin)], axis=-1)
    ms, pvs, dens = [], [], []
    for qm in (jnp.where(q_head0, q, zero), jnp.where(q_head0, zero, q)):
        s = jnp.einsum("bqd,bkd->bqk", qm, kwin, preferred_element_type=F32) + bias
        m = jnp.max(s, axis=-1, keepdims=True)
        p = jnp.exp2(s - m).astype(BF16)
        pv = jnp.einsum("bqk,bkd->bqd", p, v_ext, preferred_element_type=F32)
        ms.append(m)
        pvs.append(pv[..., :LANES])
        dens.append(pv[..., LANES:])
    den = jnp.where(q_head0, dens[0], dens[1])
    out = jnp.where(q_head0, pvs[0], pvs[1]) / den
    lse = jnp.where(q_head0, ms[0], ms[1]) * LN2 + jnp.log(den)
    for r, c in enumerate(cols):
        o_ref[:, c] = out[r * nsub:(r + 1) * nsub].reshape(rows, LANES).astype(BF16)
        lse_ref[:, c] = lse[r * nsub:(r + 1) * nsub].reshape(rows, LANES)


def _swa(q, k, v, dilation):
    b, hp, sub, _ = q.shape
    res = min(dilation, SWA_TOKENS // SWA_WIN)
    rows = min(sub, SWA_TOKENS // res)
    per_step = rows // SWA_WIN
    cur = pl.BlockSpec((None, None, rows, res * LANES), lambda bi, h, r, n: (bi, h, n, r))
    prev = pl.BlockSpec((None, None, SWA_WIN, res * LANES),
                        lambda bi, h, r, n: (bi, h, jnp.maximum(n * per_step - 1, 0), r))
    return pl.pallas_call(
        functools.partial(_swa_kernel, rows=rows, res=res),
        grid=(b, hp, dilation // res, sub // rows),
        in_specs=[cur, prev, cur, prev, cur],
        out_specs=[cur, cur],
        out_shape=[jax.ShapeDtypeStruct(q.shape, BF16), jax.ShapeDtypeStruct(q.shape, F32)],
        scratch_shapes=[pltpu.VMEM((rows + SWA_WIN, res * LANES), BF16),
                        pltpu.VMEM((rows + SWA_WIN, res * LANES), BF16)],
        compiler_params=_params("parallel", "parallel", "parallel", "parallel"),
        name=f"swa_d{dilation}",
    )(q, k, k, v, v)


def _ab_out_kernel(h_ref, *rest):
    n_pat = len(DILATIONS)
    o_refs, l_refs = rest[:n_pat], rest[n_pat:2 * n_pat]
    b_ref, w_ref, out_ref, cat_ref, tok_ref = rest[2 * n_pat:]
    for hp in range(A_WIDTH // LANES):
        outs, lses = [], []
        for i, d in enumerate(DILATIONS):
            vals = []
            for kind, ref in enumerate((o_refs[i], l_refs[i])):
                if d == 1:
                    vals.append(ref[hp].astype(F32))
                    continue
                slot = (hp * n_pat + i) * 2 + kind
                for r in range(d):
                    tok_ref[slot, pl.ds(r, PROJ_ROWS // d, stride=d), :] = (
                        ref[hp, :, r * LANES:(r + 1) * LANES].astype(F32))
                vals.append(tok_ref[slot])
            outs.append(vals[0])
            lses.append(vals[1])
        m = functools.reduce(jnp.maximum, lses)
        es = [jnp.exp(l - m) for l in lses]
        mix = sum(e * o for e, o in zip(es, outs)) / sum(es)
        cat_ref[:, hp * LANES:(hp + 1) * LANES] = mix.astype(BF16)
    cat_ref[:, A_WIDTH:] = b_ref[...]
    out_ref[...] = h_ref[...] + jnp.dot(cat_ref[...], w_ref[...], preferred_element_type=F32)


def _ab_out(h, outs, lses, b_out, w_out):
    b, s, d = h.shape
    hp = A_WIDTH // LANES
    row = pl.BlockSpec((None, PROJ_ROWS, d), lambda bi, i: (bi, i, 0))
    heads = [_dilated_spec(hp, dil) for dil in DILATIONS]
    return pl.pallas_call(
        _ab_out_kernel,
        grid=(b, s // PROJ_ROWS),
        in_specs=[row] + heads + heads + [pl.BlockSpec((None, PROJ_ROWS, B_WIDTH), lambda bi, i: (bi, i, 0)),
                                          _resident((A_WIDTH + B_WIDTH, d))],
        out_specs=row,
        out_shape=jax.ShapeDtypeStruct((b, s, d), F32),
        scratch_shapes=[pltpu.VMEM((PROJ_ROWS, A_WIDTH + B_WIDTH), BF16),
                        pltpu.VMEM((hp * len(DILATIONS) * 2, PROJ_ROWS, LANES), F32)],
        compiler_params=_params("parallel", "parallel"),
        name="ab_out_proj",
    )(h, *outs, *lses, b_out, w_out)


def _c_in_kernel(h_ref, g_ref, w_ref, q_ref, k_ref, v_ref):
    hn = _rms_norm(h_ref[...], g_ref[...]).astype(BF16)
    z = jnp.dot(hn, w_ref[...], preferred_element_type=F32)
    for hp in range(C_WIDTH // LANES):
        lo = hp * LANES
        q_ref[hp] = (z[:, lo:lo + LANES] * (SCALE * LOG2E)).astype(BF16)
        k_ref[hp] = z[:, C_WIDTH + lo:C_WIDTH + lo + LANES].astype(BF16)
        v_ref[hp] = z[:, 2 * C_WIDTH + lo:2 * C_WIDTH + lo + LANES].astype(BF16)


def _c_in(h, g, w_in):
    b, s, d = h.shape
    hp = C_WIDTH // LANES
    spec = pl.BlockSpec((None, hp, PROJ_ROWS, LANES), lambda bi, i: (bi, 0, i, 0))
    shape = jax.ShapeDtypeStruct((b, hp, s, LANES), BF16)
    return pl.pallas_call(
        _c_in_kernel,
        grid=(b, s // PROJ_ROWS),
        in_specs=[pl.BlockSpec((None, PROJ_ROWS, d), lambda bi, i: (bi, i, 0)),
                  _resident((1, d)), _resident((d, 3 * C_WIDTH))],
        out_specs=[spec, spec, spec],
        out_shape=[shape, shape, shape],
        compiler_params=_params("parallel", "parallel"),
        name="c_in_proj",
    )(h, g.reshape(1, d), w_in)


def _moba_select(q_ref, k_ref, qaug_ref, st_ref, m_ref, nb):
    seq = q_ref.shape[0]
    kmean = jnp.concatenate(
        [jnp.mean(k_ref[j * C_BLOCK:(j + 1) * C_BLOCK].astype(F32), axis=0, keepdims=True) for j in range(nb)],
        axis=0).astype(BF16)
    lane = lax.broadcasted_iota(jnp.int32, (MOBA_ROWS, LANES), 1)
    blk = lax.broadcasted_iota(jnp.int32, (nb, MOBA_ROWS), 0)
    qpos = lax.broadcasted_iota(jnp.int32, (nb, MOBA_ROWS), 1)
    zeros = lambda n: jnp.zeros((n, MOBA_ROWS), F32)

    def chunk(t, carry):
        r0 = pl.multiple_of(t * MOBA_ROWS, MOBA_ROWS)
        q = q_ref[pl.ds(r0, MOBA_ROWS), :]
        own = (qpos + r0) // C_BLOCK
        past = blk < own
        for h in range(2):
            qm = jnp.where(lane < HEAD_DIM if h == 0 else lane >= HEAD_DIM, q, jnp.zeros_like(q))
            gate = lax.dot_general(kmean, qm, CONTRACT_LAST, preferred_element_type=F32)
            gate = jnp.where(past, gate, -jnp.inf)
            chosen = jnp.zeros(gate.shape, jnp.bool_)
            for _ in range(C_TOPK):
                best = jnp.max(gate, axis=0, keepdims=True)
                first = jnp.min(jnp.where(gate == best, blk, nb), axis=0, keepdims=True)
                hit = blk == first
                chosen = chosen | hit
                gate = jnp.where(hit, -jnp.inf, gate)
            unselected = 1.0 - ((chosen & past) | (blk == own)).astype(F32)
            if h == 0:
                extra_t = jnp.concatenate([zeros(HEAD_DIM), unselected, zeros(HEAD_DIM - nb)], axis=0)
            else:
                extra_t = jnp.concatenate([unselected, zeros(LANES - nb)], axis=0)
            qaug_ref[h, pl.ds(r0, MOBA_ROWS), :] = (qm.astype(F32) + extra_t.T).astype(BF16)
        return carry

    lax.fori_loop(0, seq // MOBA_ROWS, chunk, 0)
    st_ref[...] = jnp.zeros_like(st_ref)
    m_ref[...] = jnp.full(m_ref.shape, M_INIT, F32)


def _moba_kernel(q_ref, k_ref, v_ref, o_ref, qaug_ref, st_ref, m_ref):
    j = pl.program_id(2)
    seq = q_ref.shape[0]
    nb = seq // C_BLOCK

    @pl.when(j == 0)
    def _():
        _moba_select(q_ref, k_ref, qaug_ref, st_ref, m_ref, nb)

    j0 = pl.multiple_of(j * MOBA_ROWS, MOBA_ROWS)
    kj = k_ref[pl.ds(j0, MOBA_ROWS), :]
    vj = v_ref[pl.ds(j0, MOBA_ROWS), :]
    lane = lax.broadcasted_iota(jnp.int32, (MOBA_ROWS, LANES), 1)
    key_blk = j * (MOBA_ROWS // C_BLOCK) + lax.broadcasted_iota(jnp.int32, (MOBA_ROWS, LANES), 0) // C_BLOCK
    in_head = [lane < HEAD_DIM, lane >= HEAD_DIM]
    flag_lane = [HEAD_DIM + key_blk, key_blk]
    k_aug = [jnp.where(in_head[h], kj, jnp.where(lane == flag_lane[h], SCORE_OFF, 0.0).astype(BF16))
             for h in range(2)]
    v_aug = [jnp.where(in_head[h], vj, jnp.ones_like(vj)) for h in range(2)]

    def update(h, rows, causal):
        qa = qaug_ref[h, pl.ds(rows, MOBA_ROWS), :]
        s = lax.dot_general(qa, k_aug[h], CONTRACT_LAST, preferred_element_type=F32)
        if causal:
            qi = lax.broadcasted_iota(jnp.int32, s.shape, 0)
            ki = lax.broadcasted_iota(jnp.int32, s.shape, 1)
            s = jnp.where(ki <= qi, s, SCORE_OFF)
        m_old = m_ref[h, pl.ds(rows, MOBA_ROWS), :]
        m_new = jnp.maximum(m_old, jnp.max(s, axis=-1, keepdims=True))
        p = jnp.exp2(s - jnp.concatenate([m_new] * (MOBA_ROWS // LANES), axis=1))
        pv = jnp.dot(p.astype(BF16), v_aug[h], preferred_element_type=F32)
        st = jnp.exp2(m_old - m_new) * st_ref[h, pl.ds(rows, MOBA_ROWS), :] + pv
        return st, m_new

    done = []
    for h in range(2):
        st, _ = update(h, j0, True)
        done.append(st / pltpu.roll(st, HEAD_DIM, 1))
    o_ref[pl.ds(j0, MOBA_ROWS), :] = jnp.where(in_head[0], done[0], done[1]).astype(BF16)

    def chunk(t, carry):
        rows = pl.multiple_of(t * MOBA_ROWS, MOBA_ROWS)
        for h in range(2):
            st, m_new = update(h, rows, False)
            st_ref[h, pl.ds(rows, MOBA_ROWS), :] = st
            m_ref[h, pl.ds(rows, MOBA_ROWS), :] = m_new
        return carry

    first = j + 1
    n_later = seq // MOBA_ROWS - first
    lax.fori_loop(0, n_later // 2, lambda i, c: chunk(first + 2 * i + 1, chunk(first + 2 * i, c)), 0)

    @pl.when(n_later % 2 == 1)
    def _():
        chunk(seq // MOBA_ROWS - 1, 0)


def _moba(q, k, v):
    b, hp, s, _ = q.shape
    whole = pl.BlockSpec((None, None, s, LANES), lambda bi, h, j: (bi, h, 0, 0))
    whole_in = pl.BlockSpec((None, None, s, LANES), lambda bi, h, j: (bi, h, 0, 0), pipeline_mode=pl.Buffered(1))
    return pl.pallas_call(
        _moba_kernel,
        grid=(b, hp, s // MOBA_ROWS),
        in_specs=[whole_in, whole_in, whole_in],
        out_specs=whole,
        out_shape=jax.ShapeDtypeStruct((b, hp, s, LANES), BF16),
        scratch_shapes=[pltpu.VMEM((2, s, LANES), BF16),
                        pltpu.VMEM((2, s, LANES), F32),
                        pltpu.VMEM((2, s, LANES), F32)],
        compiler_params=_params("parallel", "parallel", "arbitrary"),
        name="moba",
    )(q, k, v)


def _c_out_kernel(h_ref, o_ref, w_ref, out_ref, cat_ref):
    for hp in range(C_WIDTH // LANES):
        cat_ref[:, hp * LANES:(hp + 1) * LANES] = o_ref[hp]
    out_ref[...] = h_ref[...] + jnp.dot(cat_ref[...], w_ref[...], preferred_element_type=F32)


def _c_out(h, o, w_out):
    b, s, d = h.shape
    hp = C_WIDTH // LANES
    row = pl.BlockSpec((None, PROJ_ROWS, d), lambda bi, i: (bi, i, 0))
    return pl.pallas_call(
        _c_out_kernel,
        grid=(b, s // PROJ_ROWS),
        in_specs=[row, pl.BlockSpec((None, hp, PROJ_ROWS, LANES), lambda bi, i: (bi, 0, i, 0)),
                  _resident((C_WIDTH, d))],
        out_specs=row,
        out_shape=jax.ShapeDtypeStruct((b, s, d), F32),
        scratch_shapes=[pltpu.VMEM((PROJ_ROWS, C_WIDTH), BF16)],
        compiler_params=_params("parallel", "parallel"),
        name="c_out_proj",
    )(h, o, w_out)


def kernel(x, ffn1_norm, ffn1_w_gate, ffn1_w_up, ffn1_w_down, mix_norm, ffn2_norm, ffn2_w_gate, ffn2_w_up,
           ffn2_w_down, ab_w_in, ab_v_norm, ab_w_spatial, ab_b_spatial, ab_w_out, c_w_in, c_w_out, final_norm):
    b, s, d = x.shape
    assert DILATIONS == (1, 4, 16)
    assert d == D_MODEL and s % SWA_TOKENS == 0 and s % MOBA_ROWS == 0 and (b * s) % FFN_ROWS == 0
    assert s // C_BLOCK <= HEAD_DIM
    bf = lambda w: w.astype(BF16)

    def ffn(h, layer, norm, wg, wu, wd, final_g=None):
        return _ffn(h.reshape(b * s, d), norm[layer], wg, wu, wd, layer, final_g).reshape(b, s, d)

    h = x
    h = ffn(h, 0, ffn1_norm, ffn1_w_gate, ffn1_w_up, ffn1_w_down)
    qkvs, b_out = _ab_in(h, mix_norm[0], bf(ab_w_in[0]), ab_v_norm[0], ab_w_spatial[0], ab_b_spatial[0])
    outs, lses = zip(*[_swa(*qkv, dil) for qkv, dil in zip(qkvs, DILATIONS)])
    h = _ab_out(h, outs, lses, b_out, bf(ab_w_out[0]))
    h = ffn(h, 0, ffn2_norm, ffn2_w_gate, ffn2_w_up, ffn2_w_down)
    h = ffn(h, 1, ffn1_norm, ffn1_w_gate, ffn1_w_up, ffn1_w_down)
    qc, kc, vc = _c_in(h, mix_norm[1], bf(c_w_in[0]))
    h = _c_out(h, _moba(qc, kc, vc), bf(c_w_out[0]))
    h = ffn(h, 1, ffn2_norm, ffn2_w_gate, ffn2_w_up, ffn2_w_down, final_g=final_norm)
    return h
```

```python
import functools

import jax
import jax.numpy as jnp
from jax import lax
from jax.experimental import pallas as pl
from jax.experimental.pallas import tpu as pltpu

F32 = jnp.float32
BF16 = jnp.bfloat16

D_MODEL = 1024
D_FF = 2816
EPS = 1e-6
HEAD_DIM = 64
LANES = 128
A_HEADS = 8
A_PATTERNS = ((128, 1), (512, 4), (2048, 16))
A_WIDTH = A_HEADS * HEAD_DIM
DILATIONS = tuple(d for _, d in A_PATTERNS)
B_GROUPS = 4
B_CHUNK = 128
B_WIDTH = B_GROUPS * LANES
C_HEADS = 16
C_WIDTH = C_HEADS * HEAD_DIM
C_BLOCK = 256
C_TOPK = 3
SCALE = HEAD_DIM ** -0.5
LOG2E = 1.4426950408889634
LN2 = 0.6931471805599453
NEG_BIG = -1e30

M_INIT = -(2.0 ** 60)
SCORE_OFF = -(2.0 ** 100)

VMEM_LIMIT_BYTES = 56 * 1024 * 1024

FFN_ROWS = 1024
FFN_CHUNK = 256
PROJ_ROWS = 1024
SWA_TOKENS = 4096
SWA_WIN = 128
MOBA_ROWS = 1024

CONTRACT_LAST = (((1,), (1,)), ((), ()))


def _params(*sem):
    return pltpu.CompilerParams(dimension_semantics=sem, vmem_limit_bytes=VMEM_LIMIT_BYTES)


def _rms_norm(x, g):
    return x * lax.rsqrt(jnp.mean(x * x, axis=-1, keepdims=True) + EPS) * g


def _resident(shape):
    nd = len(shape)
    return pl.BlockSpec(shape, lambda *_: (0,) * nd, pipeline_mode=pl.Buffered(1))


FFN_WEIGHT_STEPS = D_FF // FFN_CHUNK


def _ffn_kernel(x_ref, g_ref, wg_ref, wu_ref, wd_ref, *rest, final):
    if final:
        gf_ref, o_ref, wg_bf, wu_bf, wd_bf, xn_ref, acc_ref = rest
    else:
        o_ref, wg_bf, wu_bf, wd_bf, xn_ref, acc_ref = rest
    i = pl.program_id(0)

    def start():
        xn_ref[...] = _rms_norm(x_ref[...], g_ref[...]).astype(BF16)
        acc_ref[...] = jnp.zeros_like(acc_ref)

    def apply_chunk(c, carry=0):
        off = pl.multiple_of(c * FFN_CHUNK, FFN_CHUNK)
        xn = xn_ref[...]
        gate = jnp.dot(xn, wg_bf[:, pl.ds(off, FFN_CHUNK)], preferred_element_type=F32)
        up = jnp.dot(xn, wu_bf[:, pl.ds(off, FFN_CHUNK)], preferred_element_type=F32)
        act = (jax.nn.silu(gate) * up).astype(BF16)
        acc_ref[...] += jnp.dot(act, wd_bf[pl.ds(off, FFN_CHUNK), :], preferred_element_type=F32)
        return carry

    def finish():
        y = x_ref[...] + 0.5 * acc_ref[...]
        if final:
            y = _rms_norm(y, gf_ref[...])
        o_ref[...] = y

    @pl.when(i < FFN_WEIGHT_STEPS)
    def _():
        off = pl.multiple_of(i * FFN_CHUNK, FFN_CHUNK)
        wg_bf[:, pl.ds(off, FFN_CHUNK)] = wg_ref[...].astype(BF16)
        wu_bf[:, pl.ds(off, FFN_CHUNK)] = wu_ref[...].astype(BF16)
        wd_bf[pl.ds(off, FFN_CHUNK), :] = wd_ref[...].astype(BF16)
        pl.when(i == 0)(start)
        apply_chunk(i)
        pl.when(i == FFN_WEIGHT_STEPS - 1)(finish)

    @pl.when(i >= FFN_WEIGHT_STEPS)
    def _():
        start()
        lax.fori_loop(0, FFN_WEIGHT_STEPS, apply_chunk, 0, unroll=True)
        finish()


def _ffn(x2, g, wg, wu, wd, layer, final_g=None):
    t, d = x2.shape
    final = final_g is not None
    row_spec = pl.BlockSpec((FFN_ROWS, d), lambda i: (jnp.maximum(i - (FFN_WEIGHT_STEPS - 1), 0), 0))
    col_chunk = pl.BlockSpec((None, d, FFN_CHUNK), lambda i: (layer, 0, jnp.minimum(i, FFN_WEIGHT_STEPS - 1)))
    row_chunk = pl.BlockSpec((None, FFN_CHUNK, d), lambda i: (layer, jnp.minimum(i, FFN_WEIGHT_STEPS - 1), 0))
    in_specs = [row_spec, _resident((1, d)), col_chunk, col_chunk, row_chunk]
    args = [x2, g.reshape(1, d), wg, wu, wd]
    if final:
        in_specs.append(_resident((1, d)))
        args.append(final_g.reshape(1, d))
    return pl.pallas_call(
        functools.partial(_ffn_kernel, final=final),
        grid=(FFN_WEIGHT_STEPS - 1 + t // FFN_ROWS,),
        in_specs=in_specs,
        out_specs=row_spec,
        out_shape=jax.ShapeDtypeStruct((t, d), F32),
        scratch_shapes=[pltpu.VMEM((d, D_FF), BF16), pltpu.VMEM((d, D_FF), BF16), pltpu.VMEM((D_FF, d), BF16),
                        pltpu.VMEM((FFN_ROWS, d), BF16), pltpu.VMEM((FFN_ROWS, d), F32)],
        compiler_params=_params("arbitrary"),
        name="ffn_final" if final else "ffn",
    )(*args)


def _ab_in_kernel(h_ref, g_ref, w_ref, vn_ref, ws_ref, bs_ref, *rest):
    qkv_refs, b_ref, z1_ref, z4_ref = rest[:3 * len(DILATIONS)], rest[-3], rest[-2], rest[-1]
    n_hp = A_WIDTH // LANES
    hn = _rms_norm(h_ref[...], g_ref[...]).astype(BF16)

    def project(col, width):
        return jnp.dot(hn, w_ref[:, col:col + width], preferred_element_type=F32)

    row = lax.broadcasted_iota(jnp.int32, (B_CHUNK, B_CHUNK), 0)
    col = lax.broadcasted_iota(jnp.int32, (B_CHUNK, B_CHUNK), 1)
    causal = row >= col
    zu = project(3 * A_WIDTH, B_WIDTH)
    zv = project(3 * A_WIDTH + B_WIDTH, B_WIDTH)
    for g in range(B_GROUPS):
        lo = g * LANES
        u = jax.nn.gelu(zu[:, lo:lo + LANES])
        v = jax.nn.gelu(zv[:, lo:lo + LANES])
        vn = _rms_norm(v, vn_ref[:, lo:lo + LANES]).astype(BF16)
        ws = jnp.where(causal, ws_ref[g], 0.0).astype(BF16)
        bias = bs_ref[:, g:g + 1]
        for c in range(PROJ_ROWS // B_CHUNK):
            r0 = c * B_CHUNK
            mixed = jnp.dot(ws, vn[r0:r0 + B_CHUNK], preferred_element_type=F32) + bias
            b_ref[r0:r0 + B_CHUNK, lo:lo + LANES] = (u[r0:r0 + B_CHUNK] * mixed).astype(BF16)
    for t in range(3):
        z = project(t * A_WIDTH, A_WIDTH)
        if t == 0:
            z = z * (SCALE * LOG2E)
        for hp in range(n_hp):
            x1 = z[:, hp * LANES:(hp + 1) * LANES]
            qkv_refs[t][hp] = x1.astype(BF16)
            z1_ref[t * n_hp + hp] = x1
        for hp in range(n_hp):
            for r1 in range(4):
                x4 = z1_ref[t * n_hp + hp, pl.ds(r1, PROJ_ROWS // 4, stride=4), :]
                qkv_refs[3 + t][hp, :, r1 * LANES:(r1 + 1) * LANES] = x4.astype(BF16)
                z4_ref[(t * n_hp + hp) * 4 + r1] = x4
        for hp in range(n_hp):
            for r1 in range(4):
                for r2 in range(4):
                    x16 = z4_ref[(t * n_hp + hp) * 4 + r1, pl.ds(r2, PROJ_ROWS // 16, stride=4), :]
                    r = 4 * r2 + r1
                    qkv_refs[6 + t][hp, :, r * LANES:(r + 1) * LANES] = x16.astype(BF16)


def _dilated_spec(hp, d):
    return pl.BlockSpec((None, hp, PROJ_ROWS // d, d * LANES), lambda bi, i: (bi, 0, i, 0))


def _ab_in(h, g, w_in, v_norm, w_s, b_s):
    b, s, d = h.shape
    hp = A_WIDTH // LANES
    width = w_in.shape[1]
    qkv_specs = [_dilated_spec(hp, dil) for dil in DILATIONS for _ in range(3)]
    qkv_shapes = [jax.ShapeDtypeStruct((b, hp, s // dil, dil * LANES), BF16) for dil in DILATIONS for _ in range(3)]
    outs = pl.pallas_call(
        _ab_in_kernel,
        grid=(b, s // PROJ_ROWS),
        in_specs=[
            pl.BlockSpec((None, PROJ_ROWS, d), lambda bi, i: (bi, i, 0)),
            _resident((1, d)),
            _resident((d, width)),
            _resident((1, B_WIDTH)),
            _resident((B_GROUPS, B_CHUNK, B_CHUNK)),
            _resident((B_CHUNK, B_GROUPS)),
        ],
        out_specs=qkv_specs + [pl.BlockSpec((None, PROJ_ROWS, B_WIDTH), lambda bi, i: (bi, i, 0))],
        out_shape=qkv_shapes + [jax.ShapeDtypeStruct((b, s, B_WIDTH), BF16)],
        scratch_shapes=[pltpu.VMEM((3 * hp, PROJ_ROWS, LANES), F32),
                        pltpu.VMEM((3 * hp * 4, PROJ_ROWS // 4, LANES), F32)],
        compiler_params=_params("parallel", "parallel"),
        name="ab_in_proj",
    )(h, g.reshape(1, d), w_in, v_norm.reshape(1, B_WIDTH), w_s, b_s.T)
    return [outs[3 * i:3 * i + 3] for i in range(len(DILATIONS))], outs[-1]


def _swa_kernel(q_ref, kp_ref, kc_ref, vp_ref, vc_ref, o_ref, lse_ref, kk_ref, vv_ref, *, rows, res):
    n = pl.program_id(3)
    nsub = rows // SWA_WIN
    kk_ref[0:SWA_WIN] = kp_ref[...]
    kk_ref[SWA_WIN:] = kc_ref[...]
    vv_ref[0:SWA_WIN] = vp_ref[...]
    vv_ref[SWA_WIN:] = vc_ref[...]
    cols = [slice(r * LANES, (r + 1) * LANES) for r in range(res)]
    q = jnp.concatenate([q_ref[:, c].reshape(nsub, SWA_WIN, LANES) for c in cols], axis=0)
    kwin = jnp.stack([kk_ref[j * SWA_WIN:(j + 2) * SWA_WIN, c] for c in cols for j in range(nsub)])
    vwin = jnp.stack([vv_ref[j * SWA_WIN:(j + 2) * SWA_WIN, c] for c in cols for j in range(nsub)])
    qi = lax.broadcasted_iota(jnp.int32, (SWA_WIN, 2 * SWA_WIN), 0)
    kc = lax.broadcasted_iota(jnp.int32, (SWA_WIN, 2 * SWA_WIN), 1)
    band = (kc >= qi) & (kc <= qi + SWA_WIN)
    bias = jnp.where(band, 0.0, NEG_BIG)
    bias_first = jnp.where(n == 0, jnp.where(band & (kc >= SWA_WIN), 0.0, NEG_BIG), bias)
    bias = jnp.stack([bias_first if j == 0 else bias for _ in cols for j in range(nsub)])
    q_head0 = lax.broadcasted_iota(jnp.int32, q.shape, 2) < HEAD_DIM
    zero = jnp.zeros_like(q)
    v_ext = jnp.concatenate([vwin, jnp.ones_like(vwin)], axis=-1)
    ms, pvs, dens = [], [], []
    for qm in (jnp.where(q_head0, q, zero), jnp.where(q_head0, zero, q)):
        s = jnp.einsum("bqd,bkd->bqk", qm, kwin, preferred_element_type=F32) + bias
        m = jnp.max(s, axis=-1, keepdims=True)
        p = jnp.exp2(s - m).astype(BF16)
        pv = jnp.einsum("bqk,bkd->bqd", p, v_ext, preferred_element_type=F32)
        ms.append(m)
        pvs.append(pv[..., :LANES])
        dens.append(pv[..., LANES:])
    den = jnp.where(q_head0, dens[0], dens[1])
    out = jnp.where(q_head0, pvs[0], pvs[1]) / den
    lse = jnp.where(q_head0, ms[0], ms[1]) * LN2 + jnp.log(den)
    for r, c in enumerate(cols):
        o_ref[:, c] = out[r * nsub:(r + 1) * nsub].reshape(rows, LANES).astype(BF16)
        lse_ref[:, c] = lse[r * nsub:(r + 1) * nsub].reshape(rows, LANES)


def _swa(q, k, v, dilation):
    b, hp, sub, _ = q.shape
    res = min(dilation, SWA_TOKENS // SWA_WIN)
    rows = min(sub, SWA_TOKENS // res)
    per_step = rows // SWA_WIN
    cur = pl.BlockSpec((None, None, rows, res * LANES), lambda bi, h, r, n: (bi, h, n, r))
    prev = pl.BlockSpec((None, None, SWA_WIN, res * LANES),
                        lambda bi, h, r, n: (bi, h, jnp.maximum(n * per_step - 1, 0), r))
    return pl.pallas_call(
        functools.partial(_swa_kernel, rows=rows, res=res),
        grid=(b, hp, dilation // res, sub // rows),
        in_specs=[cur, prev, cur, prev, cur],
        out_specs=[cur, cur],
        out_shape=[jax.ShapeDtypeStruct(q.shape, BF16), jax.ShapeDtypeStruct(q.shape, F32)],
        scratch_shapes=[pltpu.VMEM((rows + SWA_WIN, res * LANES), BF16),
                        pltpu.VMEM((rows + SWA_WIN, res * LANES), BF16)],
        compiler_params=_params("parallel", "parallel", "parallel", "parallel"),
        name=f"swa_d{dilation}",
    )(q, k, k, v, v)


def _ab_out_kernel(h_ref, *rest):
    n_pat = len(DILATIONS)
    o_refs, l_refs = rest[:n_pat], rest[n_pat:2 * n_pat]
    b_ref, w_ref, out_ref, cat_ref, tok_ref = rest[2 * n_pat:]
    for hp in range(A_WIDTH // LANES):
        outs, lses = [], []
        for i, d in enumerate(DILATIONS):
            vals = []
            for kind, ref in enumerate((o_refs[i], l_refs[i])):
                if d == 1:
                    vals.append(ref[hp].astype(F32))
                    continue
                slot = (hp * n_pat + i) * 2 + kind
                for r in range(d):
                    tok_ref[slot, pl.ds(r, PROJ_ROWS // d, stride=d), :] = (
                        ref[hp, :, r * LANES:(r + 1) * LANES].astype(F32))
                vals.append(tok_ref[slot])
            outs.append(vals[0])
            lses.append(vals[1])
        m = functools.reduce(jnp.maximum, lses)
        es = [jnp.exp(l - m) for l in lses]
        mix = sum(e * o for e, o in zip(es, outs)) / sum(es)
        cat_ref[:, hp * LANES:(hp + 1) * LANES] = mix.astype(BF16)
    cat_ref[:, A_WIDTH:] = b_ref[...]
    out_ref[...] = h_ref[...] + jnp.dot(cat_ref[...], w_ref[...], preferred_element_type=F32)


def _ab_out(h, outs, lses, b_out, w_out):
    b, s, d = h.shape
    hp = A_WIDTH // LANES
    row = pl.BlockSpec((None, PROJ_ROWS, d), lambda bi, i: (bi, i, 0))
    heads = [_dilated_spec(hp, dil) for dil in DILATIONS]
    return pl.pallas_call(
        _ab_out_kernel,
        grid=(b, s // PROJ_ROWS),
        in_specs=[row] + heads + heads + [pl.BlockSpec((None, PROJ_ROWS, B_WIDTH), lambda bi, i: (bi, i, 0)),
                                          _resident((A_WIDTH + B_WIDTH, d))],
        out_specs=row,
        out_shape=jax.ShapeDtypeStruct((b, s, d), F32),
        scratch_shapes=[pltpu.VMEM((PROJ_ROWS, A_WIDTH + B_WIDTH), BF16),
                        pltpu.VMEM((hp * len(DILATIONS) * 2, PROJ_ROWS, LANES), F32)],
        compiler_params=_params("parallel", "parallel"),
        name="ab_out_proj",
    )(h, *outs, *lses, b_out, w_out)


def _c_in_kernel(h_ref, g_ref, w_ref, q_ref, k_ref, v_ref):
    hn = _rms_norm(h_ref[...], g_ref[...]).astype(BF16)
    z = jnp.dot(hn, w_ref[...], preferred_element_type=F32)
    for hp in range(C_WIDTH // LANES):
        lo = hp * LANES
        q_ref[hp] = (z[:, lo:lo + LANES] * (SCALE * LOG2E)).astype(BF16)
        k_ref[hp] = z[:, C_WIDTH + lo:C_WIDTH + lo + LANES].astype(BF16)
        v_ref[hp] = z[:, 2 * C_WIDTH + lo:2 * C_WIDTH + lo + LANES].astype(BF16)


def _c_in(h, g, w_in):
    b, s, d = h.shape
    hp = C_WIDTH // LANES
    spec = pl.BlockSpec((None, hp, PROJ_ROWS, LANES), lambda bi, i: (bi, 0, i, 0))
    shape = jax.ShapeDtypeStruct((b, hp, s, LANES), BF16)
    return pl.pallas_call(
        _c_in_kernel,
        grid=(b, s // PROJ_ROWS),
        in_specs=[pl.BlockSpec((None, PROJ_ROWS, d), lambda bi, i: (bi, i, 0)),
                  _resident((1, d)), _resident((d, 3 * C_WIDTH))],
        out_specs=[spec, spec, spec],
        out_shape=[shape, shape, shape],
        compiler_params=_params("parallel", "parallel"),
        name="c_in_proj",
    )(h, g.reshape(1, d), w_in)


def _moba_select(q_ref, k_ref, qaug_ref, st_ref, m_ref, nb):
    seq = q_ref.shape[0]
    kmean = jnp.concatenate(
        [jnp.mean(k_ref[j * C_BLOCK:(j + 1) * C_BLOCK].astype(F32), axis=0, keepdims=True) for j in range(nb)],
        axis=0).astype(BF16)
    lane = lax.broadcasted_iota(jnp.int32, (MOBA_ROWS, LANES), 1)
    blk = lax.broadcasted_iota(jnp.int32, (nb, MOBA_ROWS), 0)
    qpos = lax.broadcasted_iota(jnp.int32, (nb, MOBA_ROWS), 1)
    zeros = lambda n: jnp.zeros((n, MOBA_ROWS), F32)

    def chunk(t, carry):
        r0 = pl.multiple_of(t * MOBA_ROWS, MOBA_ROWS)
        q = q_ref[pl.ds(r0, MOBA_ROWS), :]
        own = (qpos + r0) // C_BLOCK
        past = blk < own
        for h in range(2):
            qm = jnp.where(lane < HEAD_DIM if h == 0 else lane >= HEAD_DIM, q, jnp.zeros_like(q))
            gate = lax.dot_general(kmean, qm, CONTRACT_LAST, preferred_element_type=F32)
            gate = jnp.where(past, gate, -jnp.inf)
            chosen = jnp.zeros(gate.shape, jnp.bool_)
            for _ in range(C_TOPK):
                best = jnp.max(gate, axis=0, keepdims=True)
                first = jnp.min(jnp.where(gate == best, blk, nb), axis=0, keepdims=True)
                hit = blk == first
                chosen = chosen | hit
                gate = jnp.where(hit, -jnp.inf, gate)
            unselected = 1.0 - ((chosen & past) | (blk == own)).astype(F32)
            if h == 0:
                extra_t = jnp.concatenate([zeros(HEAD_DIM), unselected, zeros(HEAD_DIM - nb)], axis=0)
            else:
                extra_t = jnp.concatenate([unselected, zeros(LANES - nb)], axis=0)
            qaug_ref[h, pl.ds(r0, MOBA_ROWS), :] = (qm.astype(F32) + extra_t.T).astype(BF16)
        return carry

    lax.fori_loop(0, seq // MOBA_ROWS, chunk, 0, unroll=True)
    st_ref[...] = jnp.zeros_like(st_ref)
    m_ref[...] = jnp.full(m_ref.shape, M_INIT, F32)


def _moba_kernel(q_ref, k_ref, v_ref, o_ref, qaug_ref, st_ref, m_ref):
    j = pl.program_id(2)
    seq = q_ref.shape[0]
    nb = seq // C_BLOCK

    @pl.when(j == 0)
    def _():
        _moba_select(q_ref, k_ref, qaug_ref, st_ref, m_ref, nb)

    j0 = pl.multiple_of(j * MOBA_ROWS, MOBA_ROWS)
    kj = k_ref[pl.ds(j0, MOBA_ROWS), :]
    vj = v_ref[pl.ds(j0, MOBA_ROWS), :]
    lane = lax.broadcasted_iota(jnp.int32, (MOBA_ROWS, LANES), 1)
    key_blk = j * (MOBA_ROWS // C_BLOCK) + lax.broadcasted_iota(jnp.int32, (MOBA_ROWS, LANES), 0) // C_BLOCK
    in_head = [lane < HEAD_DIM, lane >= HEAD_DIM]
    flag_lane = [HEAD_DIM + key_blk, key_blk]
    k_aug = [jnp.where(in_head[h], kj, jnp.where(lane == flag_lane[h], SCORE_OFF, 0.0).astype(BF16))
             for h in range(2)]
    v_aug = [jnp.where(in_head[h], vj, jnp.ones_like(vj)) for h in range(2)]

    def update(h, rows, causal):
        qa = qaug_ref[h, pl.ds(rows, MOBA_ROWS), :]
        s = lax.dot_general(qa, k_aug[h], CONTRACT_LAST, preferred_element_type=F32)
        if causal:
            qi = lax.broadcasted_iota(jnp.int32, s.shape, 0)
            ki = lax.broadcasted_iota(jnp.int32, s.shape, 1)
            s = jnp.where(ki <= qi, s, SCORE_OFF)
        m_old = m_ref[h, pl.ds(rows, MOBA_ROWS), :]
        m_new = jnp.maximum(m_old, jnp.max(s, axis=-1, keepdims=True))
        p = jnp.exp2(s - jnp.concatenate([m_new] * (MOBA_ROWS // LANES), axis=1))
        pv = jnp.dot(p.astype(BF16), v_aug[h], preferred_element_type=F32)
        st = jnp.exp2(m_old - m_new) * st_ref[h, pl.ds(rows, MOBA_ROWS), :] + pv
        return st, m_new

    done = []
    for h in range(2):
        st, _ = update(h, j0, True)
        done.append(st / pltpu.roll(st, HEAD_DIM, 1))
    o_ref[pl.ds(j0, MOBA_ROWS), :] = jnp.where(in_head[0], done[0], done[1]).astype(BF16)

    def chunk(t, carry):
        rows = pl.multiple_of(t * MOBA_ROWS, MOBA_ROWS)
        for h in range(2):
            st, m_new = update(h, rows, False)
            st_ref[h, pl.ds(rows, MOBA_ROWS), :] = st
            m_ref[h, pl.ds(rows, MOBA_ROWS), :] = m_new
        return carry

    first = j + 1
    n_later = seq // MOBA_ROWS - first
    lax.fori_loop(0, n_later // 2, lambda i, c: chunk(first + 2 * i + 1, chunk(first + 2 * i, c)), 0)

    @pl.when(n_later % 2 == 1)
    def _():
        chunk(seq // MOBA_ROWS - 1, 0)


def _moba(q, k, v):
    b, hp, s, _ = q.shape
    whole = pl.BlockSpec((None, None, s, LANES), lambda bi, h, j: (bi, h, 0, 0))
    whole_in = pl.BlockSpec((None, None, s, LANES), lambda bi, h, j: (bi, h, 0, 0), pipeline_mode=pl.Buffered(1))
    return pl.pallas_call(
        _moba_kernel,
        grid=(b, hp, s // MOBA_ROWS),
        in_specs=[whole_in, whole_in, whole_in],
        out_specs=whole,
        out_shape=jax.ShapeDtypeStruct((b, hp, s, LANES), BF16),
        scratch_shapes=[pltpu.VMEM((2, s, LANES), BF16),
                        pltpu.VMEM((2, s, LANES), F32),
                        pltpu.VMEM((2, s, LANES), F32)],
        compiler_params=_params("parallel", "parallel", "arbitrary"),
        name="moba",
    )(q, k, v)


def _c_out_kernel(h_ref, o_ref, w_ref, out_ref, cat_ref):
    for hp in range(C_WIDTH // LANES):
        cat_ref[:, hp * LANES:(hp + 1) * LANES] = o_ref[hp]
    out_ref[...] = h_ref[...] + jnp.dot(cat_ref[...], w_ref[...], preferred_element_type=F32)


def _c_out(h, o, w_out):
    b, s, d = h.shape
    hp = C_WIDTH // LANES
    row = pl.BlockSpec((None, PROJ_ROWS, d), lambda bi, i: (bi, i, 0))
    return pl.pallas_call(
        _c_out_kernel,
        grid=(b, s // PROJ_ROWS),
        in_specs=[row, pl.BlockSpec((None, hp, PROJ_ROWS, LANES), lambda bi, i: (bi, 0, i, 0)),
                  _resident((C_WIDTH, d))],
        out_specs=row,
        out_shape=jax.ShapeDtypeStruct((b, s, d), F32),
        scratch_shapes=[pltpu.VMEM((PROJ_ROWS, C_WIDTH), BF16)],
        compiler_params=_params("parallel", "parallel"),
        name="c_out_proj",
    )(h, o, w_out)


def kernel(x, ffn1_norm, ffn1_w_gate, ffn1_w_up, ffn1_w_down, mix_norm, ffn2_norm, ffn2_w_gate, ffn2_w_up,
           ffn2_w_down, ab_w_in, ab_v_norm, ab_w_spatial, ab_b_spatial, ab_w_out, c_w_in, c_w_out, final_norm):
    b, s, d = x.shape
    assert DILATIONS == (1, 4, 16)
    assert d == D_MODEL and s % SWA_TOKENS == 0 and s % MOBA_ROWS == 0 and (b * s) % FFN_ROWS == 0
    assert s // C_BLOCK <= HEAD_DIM
    bf = lambda w: w.astype(BF16)

    def ffn(h, layer, norm, wg, wu, wd, final_g=None):
        return _ffn(h.reshape(b * s, d), norm[layer], wg, wu, wd, layer, final_g).reshape(b, s, d)

    h = x
    h = ffn(h, 0, ffn1_norm, ffn1_w_gate, ffn1_w_up, ffn1_w_down)
    qkvs, b_out = _ab_in(h, mix_norm[0], bf(ab_w_in[0]), ab_v_norm[0], ab_w_spatial[0], ab_b_spatial[0])
    outs, lses = zip(*[_swa(*qkv, dil) for qkv, dil in zip(qkvs, DILATIONS)])
    h = _ab_out(h, outs, lses, b_out, bf(ab_w_out[0]))
    h = ffn(h, 0, ffn2_norm, ffn2_w_gate, ffn2_w_up, ffn2_w_down)
    h = ffn(h, 1, ffn1_norm, ffn1_w_gate, ffn1_w_up, ffn1_w_down)
    qc, kc, vc = _c_in(h, mix_norm[1], bf(c_w_in[0]))
    h = _c_out(h, _moba(qc, kc, vc), bf(c_w_out[0]))
    h = ffn(h, 1, ffn2_norm, ffn2_w_gate, ffn2_w_up, ffn2_w_down, final_g=final_norm)
    return h
```

```python
import functools

import jax
import jax.numpy as jnp
from jax import lax
from jax.experimental import pallas as pl
from jax.experimental.pallas import tpu as pltpu

F32 = jnp.float32
BF16 = jnp.bfloat16

D_MODEL = 1024
D_FF = 2816
EPS = 1e-6
HEAD_DIM = 64
LANES = 128
A_HEADS = 8
A_PATTERNS = ((128, 1), (512, 4), (2048, 16))
A_WIDTH = A_HEADS * HEAD_DIM
DILATIONS = tuple(d for _, d in A_PATTERNS)
B_GROUPS = 4
B_CHUNK = 128
B_WIDTH = B_GROUPS * LANES
C_HEADS = 16
C_WIDTH = C_HEADS * HEAD_DIM
C_BLOCK = 256
C_TOPK = 3
SCALE = HEAD_DIM ** -0.5
LOG2E = 1.4426950408889634
LN2 = 0.6931471805599453
NEG_BIG = -1e30

M_INIT = -(2.0 ** 60)
SCORE_OFF = -(2.0 ** 100)

VMEM_LIMIT_BYTES = 56 * 1024 * 1024

FFN_ROWS = 1024
FFN_CHUNK = 256
PROJ_ROWS = 1024
SWA_TOKENS = 4096
SWA_WIN = 128
MOBA_ROWS = 1024

CONTRACT_LAST = (((1,), (1,)), ((), ()))


def _params(*sem):
    return pltpu.CompilerParams(dimension_semantics=sem, vmem_limit_bytes=VMEM_LIMIT_BYTES)


def _rms_norm(x, g):
    return x * lax.rsqrt(jnp.mean(x * x, axis=-1, keepdims=True) + EPS) * g


def _resident(shape):
    nd = len(shape)
    return pl.BlockSpec(shape, lambda *_: (0,) * nd, pipeline_mode=pl.Buffered(1))


FFN_WEIGHT_STEPS = D_FF // FFN_CHUNK


def _ffn_kernel(x_ref, g_ref, wg_ref, wu_ref, wd_ref, *rest, mixed, final):
    rest = list(rest)
    mix_ref = rest.pop(0) if mixed else None
    gf_ref = rest.pop(0) if final else None
    o_ref, wg_bf, wu_bf, wd_bf, xn_ref, acc_ref = rest
    i = pl.program_id(0)

    def residual():
        return x_ref[...] + mix_ref[...].astype(F32) if mixed else x_ref[...]

    def start():
        xn_ref[...] = _rms_norm(residual(), g_ref[...]).astype(BF16)
        acc_ref[...] = jnp.zeros_like(acc_ref)

    def apply_chunk(c, carry=0):
        off = pl.multiple_of(c * FFN_CHUNK, FFN_CHUNK)
        xn = xn_ref[...]
        gate = jnp.dot(xn, wg_bf[:, pl.ds(off, FFN_CHUNK)], preferred_element_type=F32)
        up = jnp.dot(xn, wu_bf[:, pl.ds(off, FFN_CHUNK)], preferred_element_type=F32)
        act = (jax.nn.silu(gate) * up).astype(BF16)
        acc_ref[...] += jnp.dot(act, wd_bf[pl.ds(off, FFN_CHUNK), :], preferred_element_type=F32)
        return carry

    def finish():
        y = residual() + 0.5 * acc_ref[...]
        if final:
            y = _rms_norm(y, gf_ref[...])
        o_ref[...] = y

    @pl.when(i < FFN_WEIGHT_STEPS)
    def _():
        off = pl.multiple_of(i * FFN_CHUNK, FFN_CHUNK)
        wg_bf[:, pl.ds(off, FFN_CHUNK)] = wg_ref[...].astype(BF16)
        wu_bf[:, pl.ds(off, FFN_CHUNK)] = wu_ref[...].astype(BF16)
        wd_bf[pl.ds(off, FFN_CHUNK), :] = wd_ref[...].astype(BF16)
        pl.when(i == 0)(start)
        apply_chunk(i)
        pl.when(i == FFN_WEIGHT_STEPS - 1)(finish)

    @pl.when(i >= FFN_WEIGHT_STEPS)
    def _():
        start()
        lax.fori_loop(0, FFN_WEIGHT_STEPS, apply_chunk, 0, unroll=True)
        finish()


def _ffn(x2, g, wg, wu, wd, layer, mix=None, final_g=None):
    t, d = x2.shape
    mixed, final = mix is not None, final_g is not None
    row_spec = pl.BlockSpec((FFN_ROWS, d), lambda i: (jnp.maximum(i - (FFN_WEIGHT_STEPS - 1), 0), 0))
    col_chunk = pl.BlockSpec((None, d, FFN_CHUNK), lambda i: (layer, 0, jnp.minimum(i, FFN_WEIGHT_STEPS - 1)))
    row_chunk = pl.BlockSpec((None, FFN_CHUNK, d), lambda i: (layer, jnp.minimum(i, FFN_WEIGHT_STEPS - 1), 0))
    in_specs = [row_spec, _resident((1, d)), col_chunk, col_chunk, row_chunk]
    args = [x2, g.reshape(1, d), wg, wu, wd]
    if mixed:
        in_specs.append(row_spec)
        args.append(mix)
    if final:
        in_specs.append(_resident((1, d)))
        args.append(final_g.reshape(1, d))
    return pl.pallas_call(
        functools.partial(_ffn_kernel, mixed=mixed, final=final),
        grid=(FFN_WEIGHT_STEPS - 1 + t // FFN_ROWS,),
        in_specs=in_specs,
        out_specs=row_spec,
        out_shape=jax.ShapeDtypeStruct((t, d), F32),
        scratch_shapes=[pltpu.VMEM((d, D_FF), BF16), pltpu.VMEM((d, D_FF), BF16), pltpu.VMEM((D_FF, d), BF16),
                        pltpu.VMEM((FFN_ROWS, d), BF16), pltpu.VMEM((FFN_ROWS, d), F32)],
        compiler_params=_params("arbitrary"),
        name="ffn_final" if final else "ffn",
    )(*args)


def _ab_in_kernel(h_ref, g_ref, w_ref, vn_ref, ws_ref, bs_ref, *rest):
    qkv_refs, b_ref, z1_ref, z4_ref = rest[:3 * len(DILATIONS)], rest[-3], rest[-2], rest[-1]
    n_hp = A_WIDTH // LANES
    hn = _rms_norm(h_ref[...], g_ref[...]).astype(BF16)

    def project(col, width):
        return jnp.dot(hn, w_ref[:, col:col + width], preferred_element_type=F32)

    row = lax.broadcasted_iota(jnp.int32, (B_CHUNK, B_CHUNK), 0)
    col = lax.broadcasted_iota(jnp.int32, (B_CHUNK, B_CHUNK), 1)
    causal = row >= col
    zu = project(3 * A_WIDTH, B_WIDTH)
    zv = project(3 * A_WIDTH + B_WIDTH, B_WIDTH)
    for g in range(B_GROUPS):
        lo = g * LANES
        u = jax.nn.gelu(zu[:, lo:lo + LANES])
        v = jax.nn.gelu(zv[:, lo:lo + LANES])
        vn = _rms_norm(v, vn_ref[:, lo:lo + LANES]).astype(BF16)
        ws = jnp.where(causal, ws_ref[g], 0.0).astype(BF16)
        bias = bs_ref[:, g:g + 1]
        for c in range(PROJ_ROWS // B_CHUNK):
            r0 = c * B_CHUNK
            mixed = jnp.dot(ws, vn[r0:r0 + B_CHUNK], preferred_element_type=F32) + bias
            b_ref[r0:r0 + B_CHUNK, lo:lo + LANES] = (u[r0:r0 + B_CHUNK] * mixed).astype(BF16)
    for t in range(3):
        z = project(t * A_WIDTH, A_WIDTH)
        if t == 0:
            z = z * (SCALE * LOG2E)
        for hp in range(n_hp):
            x1 = z[:, hp * LANES:(hp + 1) * LANES]
            qkv_refs[t][hp] = x1.astype(BF16)
            z1_ref[t * n_hp + hp] = x1
        for hp in range(n_hp):
            for r1 in range(4):
                x4 = z1_ref[t * n_hp + hp, pl.ds(r1, PROJ_ROWS // 4, stride=4), :]
                qkv_refs[3 + t][hp, :, r1 * LANES:(r1 + 1) * LANES] = x4.astype(BF16)
                z4_ref[(t * n_hp + hp) * 4 + r1] = x4
        for hp in range(n_hp):
            for r1 in range(4):
                for r2 in range(4):
                    x16 = z4_ref[(t * n_hp + hp) * 4 + r1, pl.ds(r2, PROJ_ROWS // 16, stride=4), :]
                    r = 4 * r2 + r1
                    qkv_refs[6 + t][hp, :, r * LANES:(r + 1) * LANES] = x16.astype(BF16)


def _dilated_spec(hp, d):
    return pl.BlockSpec((None, hp, PROJ_ROWS // d, d * LANES), lambda bi, i: (bi, 0, i, 0))


def _ab_in(h, g, w_in, v_norm, w_s, b_s):
    b, s, d = h.shape
    hp = A_WIDTH // LANES
    width = w_in.shape[1]
    qkv_specs = [_dilated_spec(hp, dil) for dil in DILATIONS for _ in range(3)]
    qkv_shapes = [jax.ShapeDtypeStruct((b, hp, s // dil, dil * LANES), BF16) for dil in DILATIONS for _ in range(3)]
    outs = pl.pallas_call(
        _ab_in_kernel,
        grid=(b, s // PROJ_ROWS),
        in_specs=[
            pl.BlockSpec((None, PROJ_ROWS, d), lambda bi, i: (bi, i, 0)),
            _resident((1, d)),
            _resident((d, width)),
            _resident((1, B_WIDTH)),
            _resident((B_GROUPS, B_CHUNK, B_CHUNK)),
            _resident((B_CHUNK, B_GROUPS)),
        ],
        out_specs=qkv_specs + [pl.BlockSpec((None, PROJ_ROWS, B_WIDTH), lambda bi, i: (bi, i, 0))],
        out_shape=qkv_shapes + [jax.ShapeDtypeStruct((b, s, B_WIDTH), BF16)],
        scratch_shapes=[pltpu.VMEM((3 * hp, PROJ_ROWS, LANES), F32),
                        pltpu.VMEM((3 * hp * 4, PROJ_ROWS // 4, LANES), F32)],
        compiler_params=_params("parallel", "parallel"),
        name="ab_in_proj",
    )(h, g.reshape(1, d), w_in, v_norm.reshape(1, B_WIDTH), w_s, b_s.T)
    return [outs[3 * i:3 * i + 3] for i in range(len(DILATIONS))], outs[-1]


def _swa_kernel(q_ref, kp_ref, kc_ref, vp_ref, vc_ref, o_ref, lse_ref, kk_ref, vv_ref, *, rows, res):
    n = pl.program_id(3)
    nsub = rows // SWA_WIN
    kk_ref[0:SWA_WIN] = kp_ref[...]
    kk_ref[SWA_WIN:] = kc_ref[...]
    vv_ref[0:SWA_WIN] = vp_ref[...]
    vv_ref[SWA_WIN:] = vc_ref[...]
    cols = [slice(r * LANES, (r + 1) * LANES) for r in range(res)]
    q = jnp.concatenate([q_ref[:, c].reshape(nsub, SWA_WIN, LANES) for c in cols], axis=0)
    kwin = jnp.stack([kk_ref[j * SWA_WIN:(j + 2) * SWA_WIN, c] for c in cols for j in range(nsub)])
    vwin = jnp.stack([vv_ref[j * SWA_WIN:(j + 2) * SWA_WIN, c] for c in cols for j in range(nsub)])
    qi = lax.broadcasted_iota(jnp.int32, (SWA_WIN, 2 * SWA_WIN), 0)
    kc = lax.broadcasted_iota(jnp.int32, (SWA_WIN, 2 * SWA_WIN), 1)
    band = (kc >= qi) & (kc <= qi + SWA_WIN)
    bias = jnp.where(band, 0.0, NEG_BIG)
    bias_first = jnp.where(n == 0, jnp.where(band & (kc >= SWA_WIN), 0.0, NEG_BIG), bias)
    bias = jnp.stack([bias_first if j == 0 else bias for _ in cols for j in range(nsub)])
    q_head0 = lax.broadcasted_iota(jnp.int32, q.shape, 2) < HEAD_DIM
    zero = jnp.zeros_like(q)
    v_ext = jnp.concatenate([vwin, jnp.ones_like(vwin)], axis=-1)
    ms, pvs, dens = [], [], []
    for qm in (jnp.where(q_head0, q, zero), jnp.where(q_head0, zero, q)):
        s = jnp.einsum("bqd,bkd->bqk", qm, kwin, preferred_element_type=F32) + bias
        m = jnp.max(s, axis=-1, keepdims=True)
        p = jnp.exp2(s - m).astype(BF16)
        pv = jnp.einsum("bqk,bkd->bqd", p, v_ext, preferred_element_type=F32)
        ms.append(m)
        pvs.append(pv[..., :LANES])
        dens.append(pv[..., LANES:])
    den = jnp.where(q_head0, dens[0], dens[1])
    out = jnp.where(q_head0, pvs[0], pvs[1]) / den
    lse = jnp.where(q_head0, ms[0], ms[1]) * LN2 + jnp.log(den)
    for r, c in enumerate(cols):
        o_ref[:, c] = out[r * nsub:(r + 1) * nsub].reshape(rows, LANES).astype(BF16)
        lse_ref[:, c] = lse[r * nsub:(r + 1) * nsub].reshape(rows, LANES)


def _swa(q, k, v, dilation):
    b, hp, sub, _ = q.shape
    res = min(dilation, SWA_TOKENS // SWA_WIN)
    rows = min(sub, SWA_TOKENS // res)
    per_step = rows // SWA_WIN
    cur = pl.BlockSpec((None, None, rows, res * LANES), lambda bi, h, r, n: (bi, h, n, r))
    prev = pl.BlockSpec((None, None, SWA_WIN, res * LANES),
                        lambda bi, h, r, n: (bi, h, jnp.maximum(n * per_step - 1, 0), r))
    return pl.pallas_call(
        functools.partial(_swa_kernel, rows=rows, res=res),
        grid=(b, hp, dilation // res, sub // rows),
        in_specs=[cur, prev, cur, prev, cur],
        out_specs=[cur, cur],
        out_shape=[jax.ShapeDtypeStruct(q.shape, BF16), jax.ShapeDtypeStruct(q.shape, F32)],
        scratch_shapes=[pltpu.VMEM((rows + SWA_WIN, res * LANES), BF16),
                        pltpu.VMEM((rows + SWA_WIN, res * LANES), BF16)],
        compiler_params=_params("parallel", "parallel", "parallel", "parallel"),
        name=f"swa_d{dilation}",
    )(q, k, k, v, v)


def _ab_out_kernel(*rest):
    n_pat = len(DILATIONS)
    o_refs, l_refs = rest[:n_pat], rest[n_pat:2 * n_pat]
    b_ref, w_ref, out_ref, cat_ref, tok_ref = rest[2 * n_pat:]
    for hp in range(A_WIDTH // LANES):
        outs, lses = [], []
        for i, d in enumerate(DILATIONS):
            vals = []
            for kind, ref in enumerate((o_refs[i], l_refs[i])):
                if d == 1:
                    vals.append(ref[hp].astype(F32))
                    continue
                slot = (hp * n_pat + i) * 2 + kind
                for r in range(d):
                    tok_ref[slot, pl.ds(r, PROJ_ROWS // d, stride=d), :] = (
                        ref[hp, :, r * LANES:(r + 1) * LANES].astype(F32))
                vals.append(tok_ref[slot])
            outs.append(vals[0])
            lses.append(vals[1])
        m = functools.reduce(jnp.maximum, lses)
        es = [jnp.exp(l - m) for l in lses]
        mix = sum(e * o for e, o in zip(es, outs)) / sum(es)
        cat_ref[:, hp * LANES:(hp + 1) * LANES] = mix.astype(BF16)
    cat_ref[:, A_WIDTH:] = b_ref[...]
    out_ref[...] = jnp.dot(cat_ref[...], w_ref[...], preferred_element_type=F32).astype(BF16)


def _ab_out(outs, lses, b_out, w_out):
    b, s, _ = b_out.shape
    d = w_out.shape[1]
    hp = A_WIDTH // LANES
    heads = [_dilated_spec(hp, dil) for dil in DILATIONS]
    return pl.pallas_call(
        _ab_out_kernel,
        grid=(b, s // PROJ_ROWS),
        in_specs=heads + heads + [pl.BlockSpec((None, PROJ_ROWS, B_WIDTH), lambda bi, i: (bi, i, 0)),
                                  _resident((A_WIDTH + B_WIDTH, d))],
        out_specs=pl.BlockSpec((None, PROJ_ROWS, d), lambda bi, i: (bi, i, 0)),
        out_shape=jax.ShapeDtypeStruct((b, s, d), BF16),
        scratch_shapes=[pltpu.VMEM((PROJ_ROWS, A_WIDTH + B_WIDTH), BF16),
                        pltpu.VMEM((hp * len(DILATIONS) * 2, PROJ_ROWS, LANES), F32)],
        compiler_params=_params("parallel", "parallel"),
        name="ab_out_proj",
    )(*outs, *lses, b_out, w_out)


def _c_in_kernel(h_ref, g_ref, w_ref, q_ref, k_ref, v_ref):
    hn = _rms_norm(h_ref[...], g_ref[...]).astype(BF16)
    z = jnp.dot(hn, w_ref[...], preferred_element_type=F32)
    for hp in range(C_WIDTH // LANES):
        lo = hp * LANES
        q_ref[hp] = (z[:, lo:lo + LANES] * (SCALE * LOG2E)).astype(BF16)
        k_ref[hp] = z[:, C_WIDTH + lo:C_WIDTH + lo + LANES].astype(BF16)
        v_ref[hp] = z[:, 2 * C_WIDTH + lo:2 * C_WIDTH + lo + LANES].astype(BF16)


def _c_in(h, g, w_in):
    b, s, d = h.shape
    hp = C_WIDTH // LANES
    spec = pl.BlockSpec((None, hp, PROJ_ROWS, LANES), lambda bi, i: (bi, 0, i, 0))
    shape = jax.ShapeDtypeStruct((b, hp, s, LANES), BF16)
    return pl.pallas_call(
        _c_in_kernel,
        grid=(b, s // PROJ_ROWS),
        in_specs=[pl.BlockSpec((None, PROJ_ROWS, d), lambda bi, i: (bi, i, 0)),
                  _resident((1, d)), _resident((d, 3 * C_WIDTH))],
        out_specs=[spec, spec, spec],
        out_shape=[shape, shape, shape],
        compiler_params=_params("parallel", "parallel"),
        name="c_in_proj",
    )(h, g.reshape(1, d), w_in)


def _moba_select(q_ref, k_ref, qaug_ref, st_ref, m_ref, nb):
    seq = q_ref.shape[0]
    kmean = jnp.concatenate(
        [jnp.mean(k_ref[j * C_BLOCK:(j + 1) * C_BLOCK].astype(F32), axis=0, keepdims=True) for j in range(nb)],
        axis=0).astype(BF16)
    lane = lax.broadcasted_iota(jnp.int32, (MOBA_ROWS, LANES), 1)
    blk = lax.broadcasted_iota(jnp.int32, (nb, MOBA_ROWS), 0)
    qpos = lax.broadcasted_iota(jnp.int32, (nb, MOBA_ROWS), 1)
    zeros = lambda n: jnp.zeros((n, MOBA_ROWS), F32)

    def chunk(t, carry):
        r0 = pl.multiple_of(t * MOBA_ROWS, MOBA_ROWS)
        q = q_ref[pl.ds(r0, MOBA_ROWS), :]
        own = (qpos + r0) // C_BLOCK
        past = blk < own
        for h in range(2):
            qm = jnp.where(lane < HEAD_DIM if h == 0 else lane >= HEAD_DIM, q, jnp.zeros_like(q))
            gate = lax.dot_general(kmean, qm, CONTRACT_LAST, preferred_element_type=F32)
            gate = jnp.where(past, gate, -jnp.inf)
            chosen = jnp.zeros(gate.shape, jnp.bool_)
            for _ in range(C_TOPK):
                best = jnp.max(gate, axis=0, keepdims=True)
                first = jnp.min(jnp.where(gate == best, blk, nb), axis=0, keepdims=True)
                hit = blk == first
                chosen = chosen | hit
                gate = jnp.where(hit, -jnp.inf, gate)
            unselected = 1.0 - ((chosen & past) | (blk == own)).astype(F32)
            if h == 0:
                extra_t = jnp.concatenate([zeros(HEAD_DIM), unselected, zeros(HEAD_DIM - nb)], axis=0)
            else:
                extra_t = jnp.concatenate([unselected, zeros(LANES - nb)], axis=0)
            qaug_ref[h, pl.ds(r0, MOBA_ROWS), :] = (qm.astype(F32) + extra_t.T).astype(BF16)
        return carry

    lax.fori_loop(0, seq // MOBA_ROWS, chunk, 0, unroll=True)
    st_ref[...] = jnp.zeros_like(st_ref)
    m_ref[...] = jnp.full(m_ref.shape, M_INIT, F32)


def _moba_kernel(q_ref, k_ref, v_ref, o_ref, qaug_ref, st_ref, m_ref):
    j = pl.program_id(2)
    seq = q_ref.shape[0]
    nb = seq // C_BLOCK

    @pl.when(j == 0)
    def _():
        _moba_select(q_ref, k_ref, qaug_ref, st_ref, m_ref, nb)

    j0 = pl.multiple_of(j * MOBA_ROWS, MOBA_ROWS)
    kj = k_ref[pl.ds(j0, MOBA_ROWS), :]
    vj = v_ref[pl.ds(j0, MOBA_ROWS), :]
    lane = lax.broadcasted_iota(jnp.int32, (MOBA_ROWS, LANES), 1)
    key_blk = j * (MOBA_ROWS // C_BLOCK) + lax.broadcasted_iota(jnp.int32, (MOBA_ROWS, LANES), 0) // C_BLOCK
    in_head = [lane < HEAD_DIM, lane >= HEAD_DIM]
    flag_lane = [HEAD_DIM + key_blk, key_blk]
    k_aug = [jnp.where(in_head[h], kj, jnp.where(lane == flag_lane[h], SCORE_OFF, 0.0).astype(BF16))
             for h in range(2)]
    v_aug = [jnp.where(in_head[h], vj, jnp.ones_like(vj)) for h in range(2)]

    def update(h, rows, causal):
        qa = qaug_ref[h, pl.ds(rows, MOBA_ROWS), :]
        s = lax.dot_general(qa, k_aug[h], CONTRACT_LAST, preferred_element_type=F32)
        if causal:
            qi = lax.broadcasted_iota(jnp.int32, s.shape, 0)
            ki = lax.broadcasted_iota(jnp.int32, s.shape, 1)
            s = jnp.where(ki <= qi, s, SCORE_OFF)
        m_old = m_ref[h, pl.ds(rows, MOBA_ROWS), :]
        m_new = jnp.maximum(m_old, jnp.max(s, axis=-1, keepdims=True))
        p = jnp.exp2(s - jnp.concatenate([m_new] * (MOBA_ROWS // LANES), axis=1))
        pv = jnp.dot(p.astype(BF16), v_aug[h], preferred_element_type=F32)
        st = jnp.exp2(m_old - m_new) * st_ref[h, pl.ds(rows, MOBA_ROWS), :] + pv
        return st, m_new

    done = []
    for h in range(2):
        st, _ = update(h, j0, True)
        done.append(st / pltpu.roll(st, HEAD_DIM, 1))
    o_ref[pl.ds(j0, MOBA_ROWS), :] = jnp.where(in_head[0], done[0], done[1]).astype(BF16)

    def chunk(t, carry):
        rows = pl.multiple_of(t * MOBA_ROWS, MOBA_ROWS)
        for h in range(2):
            st, m_new = update(h, rows, False)
            st_ref[h, pl.ds(rows, MOBA_ROWS), :] = st
            m_ref[h, pl.ds(rows, MOBA_ROWS), :] = m_new
        return carry

    first = j + 1
    n_later = seq // MOBA_ROWS - first
    lax.fori_loop(0, n_later // 2, lambda i, c: chunk(first + 2 * i + 1, chunk(first + 2 * i, c)), 0)

    @pl.when(n_later % 2 == 1)
    def _():
        chunk(seq // MOBA_ROWS - 1, 0)


def _moba(q, k, v):
    b, hp, s, _ = q.shape
    whole = pl.BlockSpec((None, None, s, LANES), lambda bi, h, j: (bi, h, 0, 0))
    whole_in = pl.BlockSpec((None, None, s, LANES), lambda bi, h, j: (bi, h, 0, 0), pipeline_mode=pl.Buffered(1))
    return pl.pallas_call(
        _moba_kernel,
        grid=(b, hp, s // MOBA_ROWS),
        in_specs=[whole_in, whole_in, whole_in],
        out_specs=whole,
        out_shape=jax.ShapeDtypeStruct((b, hp, s, LANES), BF16),
        scratch_shapes=[pltpu.VMEM((2, s, LANES), BF16),
                        pltpu.VMEM((2, s, LANES), F32),
                        pltpu.VMEM((2, s, LANES), F32)],
        compiler_params=_params("parallel", "parallel", "arbitrary"),
        name="moba",
    )(q, k, v)


def _c_out_kernel(o_ref, w_ref, out_ref, cat_ref):
    for hp in range(C_WIDTH // LANES):
        cat_ref[:, hp * LANES:(hp + 1) * LANES] = o_ref[hp]
    out_ref[...] = jnp.dot(cat_ref[...], w_ref[...], preferred_element_type=F32).astype(BF16)


def _c_out(o, w_out):
    b, hp, s, _ = o.shape
    d = w_out.shape[1]
    return pl.pallas_call(
        _c_out_kernel,
        grid=(b, s // PROJ_ROWS),
        in_specs=[pl.BlockSpec((None, hp, PROJ_ROWS, LANES), lambda bi, i: (bi, 0, i, 0)), _resident((C_WIDTH, d))],
        out_specs=pl.BlockSpec((None, PROJ_ROWS, d), lambda bi, i: (bi, i, 0)),
        out_shape=jax.ShapeDtypeStruct((b, s, d), BF16),
        scratch_shapes=[pltpu.VMEM((PROJ_ROWS, C_WIDTH), BF16)],
        compiler_params=_params("parallel", "parallel"),
        name="c_out_proj",
    )(o, w_out)


def kernel(x, ffn1_norm, ffn1_w_gate, ffn1_w_up, ffn1_w_down, mix_norm, ffn2_norm, ffn2_w_gate, ffn2_w_up,
           ffn2_w_down, ab_w_in, ab_v_norm, ab_w_spatial, ab_b_spatial, ab_w_out, c_w_in, c_w_out, final_norm):
    b, s, d = x.shape
    assert DILATIONS == (1, 4, 16)
    assert d == D_MODEL and s % SWA_TOKENS == 0 and s % MOBA_ROWS == 0 and (b * s) % FFN_ROWS == 0
    assert s // C_BLOCK <= HEAD_DIM
    bf = lambda w: w.astype(BF16)

    def ffn(h, layer, norm, wg, wu, wd, mix=None, final_g=None):
        mix = None if mix is None else mix.reshape(b * s, d)
        return _ffn(h.reshape(b * s, d), norm[layer], wg, wu, wd, layer, mix, final_g).reshape(b, s, d)

    h = x
    h = ffn(h, 0, ffn1_norm, ffn1_w_gate, ffn1_w_up, ffn1_w_down)
    qkvs, b_out = _ab_in(h, mix_norm[0], bf(ab_w_in[0]), ab_v_norm[0], ab_w_spatial[0], ab_b_spatial[0])
    outs, lses = zip(*[_swa(*qkv, dil) for qkv, dil in zip(qkvs, DILATIONS)])
    mix = _ab_out(outs, lses, b_out, bf(ab_w_out[0]))
    h = ffn(h, 0, ffn2_norm, ffn2_w_gate, ffn2_w_up, ffn2_w_down, mix=mix)
    h = ffn(h, 1, ffn1_norm, ffn1_w_gate, ffn1_w_up, ffn1_w_down)
    qc, kc, vc = _c_in(h, mix_norm[1], bf(c_w_in[0]))
    mix = _c_out(_moba(qc, kc, vc), bf(c_w_out[0]))
    h = ffn(h, 1, ffn2_norm, ffn2_w_gate, ffn2_w_up, ffn2_w_down, mix=mix, final_g=final_norm)
    return h
```

```python
import functools

import jax
import jax.numpy as jnp
from jax import lax
from jax.experimental import pallas as pl
from jax.experimental.pallas import tpu as pltpu

F32 = jnp.float32
BF16 = jnp.bfloat16

D_MODEL = 1024
D_FF = 2816
EPS = 1e-6
HEAD_DIM = 64
LANES = 128
A_HEADS = 8
A_PATTERNS = ((128, 1), (512, 4), (2048, 16))
A_WIDTH = A_HEADS * HEAD_DIM
DILATIONS = tuple(d for _, d in A_PATTERNS)
B_GROUPS = 4
B_CHUNK = 128
B_WIDTH = B_GROUPS * LANES
C_HEADS = 16
C_WIDTH = C_HEADS * HEAD_DIM
C_BLOCK = 256
C_TOPK = 3
SCALE = HEAD_DIM ** -0.5
LOG2E = 1.4426950408889634
LN2 = 0.6931471805599453
NEG_BIG = -1e30

M_INIT = -(2.0 ** 60)
SCORE_OFF = -(2.0 ** 100)

VMEM_LIMIT_BYTES = 56 * 1024 * 1024

FFN_ROWS = 1024
FFN_CHUNK = 256
PROJ_ROWS = 1024
SWA_TOKENS = 4096
SWA_WIN = 128
MOBA_ROWS = 1024

CONTRACT_LAST = (((1,), (1,)), ((), ()))


def _params(*sem):
    return pltpu.CompilerParams(dimension_semantics=sem, vmem_limit_bytes=VMEM_LIMIT_BYTES)


def _rms_norm(x, g):
    return x * lax.rsqrt(jnp.mean(x * x, axis=-1, keepdims=True) + EPS) * g


def _resident(shape):
    nd = len(shape)
    return pl.BlockSpec(shape, lambda *_: (0,) * nd, pipeline_mode=pl.Buffered(1))


FFN_WEIGHT_STEPS = D_FF // FFN_CHUNK


def _ffn_kernel(x_ref, g_ref, wg_ref, wu_ref, wd_ref, *rest, final):
    if final:
        gf_ref, o_ref, wg_bf, wu_bf, wd_bf, xn_ref, acc_ref = rest
    else:
        o_ref, wg_bf, wu_bf, wd_bf, xn_ref, acc_ref = rest
    i = pl.program_id(0)

    def start():
        xn_ref[...] = _rms_norm(x_ref[...], g_ref[...]).astype(BF16)
        acc_ref[...] = jnp.zeros_like(acc_ref)

    def apply_chunk(c, carry=0):
        off = pl.multiple_of(c * FFN_CHUNK, FFN_CHUNK)
        xn = xn_ref[...]
        gate = jnp.dot(xn, wg_bf[:, pl.ds(off, FFN_CHUNK)], preferred_element_type=F32)
        up = jnp.dot(xn, wu_bf[:, pl.ds(off, FFN_CHUNK)], preferred_element_type=F32)
        act = (jax.nn.silu(gate) * up).astype(BF16)
        acc_ref[...] += jnp.dot(act, wd_bf[pl.ds(off, FFN_CHUNK), :], preferred_element_type=F32)
        return carry

    def finish():
        y = x_ref[...] + 0.5 * acc_ref[...]
        if final:
            y = _rms_norm(y, gf_ref[...])
        o_ref[...] = y

    @pl.when(i < FFN_WEIGHT_STEPS)
    def _():
        off = pl.multiple_of(i * FFN_CHUNK, FFN_CHUNK)
        wg_bf[:, pl.ds(off, FFN_CHUNK)] = wg_ref[...].astype(BF16)
        wu_bf[:, pl.ds(off, FFN_CHUNK)] = wu_ref[...].astype(BF16)
        wd_bf[pl.ds(off, FFN_CHUNK), :] = wd_ref[...].astype(BF16)
        pl.when(i == 0)(start)
        apply_chunk(i)
        pl.when(i == FFN_WEIGHT_STEPS - 1)(finish)

    @pl.when(i >= FFN_WEIGHT_STEPS)
    def _():
        start()
        lax.fori_loop(0, FFN_WEIGHT_STEPS, apply_chunk, 0, unroll=True)
        finish()


def _ffn(x2, g, wg, wu, wd, layer, final_g=None):
    t, d = x2.shape
    final = final_g is not None
    row_spec = pl.BlockSpec((FFN_ROWS, d), lambda i: (jnp.maximum(i - (FFN_WEIGHT_STEPS - 1), 0), 0))
    col_chunk = pl.BlockSpec((None, d, FFN_CHUNK), lambda i: (layer, 0, jnp.minimum(i, FFN_WEIGHT_STEPS - 1)))
    row_chunk = pl.BlockSpec((None, FFN_CHUNK, d), lambda i: (layer, jnp.minimum(i, FFN_WEIGHT_STEPS - 1), 0))
    in_specs = [row_spec, _resident((1, d)), col_chunk, col_chunk, row_chunk]
    args = [x2, g.reshape(1, d), wg, wu, wd]
    if final:
        in_specs.append(_resident((1, d)))
        args.append(final_g.reshape(1, d))
    return pl.pallas_call(
        functools.partial(_ffn_kernel, final=final),
        grid=(FFN_WEIGHT_STEPS - 1 + t // FFN_ROWS,),
        in_specs=in_specs,
        out_specs=row_spec,
        out_shape=jax.ShapeDtypeStruct((t, d), F32),
        scratch_shapes=[pltpu.VMEM((d, D_FF), BF16), pltpu.VMEM((d, D_FF), BF16), pltpu.VMEM((D_FF, d), BF16),
                        pltpu.VMEM((FFN_ROWS, d), BF16), pltpu.VMEM((FFN_ROWS, d), F32)],
        compiler_params=_params("arbitrary"),
        name="ffn_final" if final else "ffn",
    )(*args)


def _ab_in_kernel(h_ref, g_ref, w_ref, vn_ref, ws_ref, bs_ref, *rest):
    qkv_refs, b_ref, z1_ref, z4_ref = rest[:3 * len(DILATIONS)], rest[-3], rest[-2], rest[-1]
    n_hp = A_WIDTH // LANES
    hn = _rms_norm(h_ref[...], g_ref[...]).astype(BF16)

    def project(col, width):
        return jnp.dot(hn, w_ref[:, col:col + width], preferred_element_type=F32)

    row = lax.broadcasted_iota(jnp.int32, (B_CHUNK, B_CHUNK), 0)
    col = lax.broadcasted_iota(jnp.int32, (B_CHUNK, B_CHUNK), 1)
    causal = row >= col
    zu = project(3 * A_WIDTH, B_WIDTH)
    zv = project(3 * A_WIDTH + B_WIDTH, B_WIDTH)
    for g in range(B_GROUPS):
        lo = g * LANES
        u = jax.nn.gelu(zu[:, lo:lo + LANES])
        v = jax.nn.gelu(zv[:, lo:lo + LANES])
        vn = _rms_norm(v, vn_ref[:, lo:lo + LANES]).astype(BF16)
        ws = jnp.where(causal, ws_ref[g], 0.0).astype(BF16)
        bias = bs_ref[:, g:g + 1]
        for c in range(PROJ_ROWS // B_CHUNK):
            r0 = c * B_CHUNK
            mixed = jnp.dot(ws, vn[r0:r0 + B_CHUNK], preferred_element_type=F32) + bias
            b_ref[r0:r0 + B_CHUNK, lo:lo + LANES] = (u[r0:r0 + B_CHUNK] * mixed).astype(BF16)
    for t in range(3):
        z = project(t * A_WIDTH, A_WIDTH)
        if t == 0:
            z = z * (SCALE * LOG2E)
        for hp in range(n_hp):
            x1 = z[:, hp * LANES:(hp + 1) * LANES]
            qkv_refs[t][hp] = x1.astype(BF16)
            z1_ref[t * n_hp + hp] = x1
        for hp in range(n_hp):
            for r1 in range(4):
                x4 = z1_ref[t * n_hp + hp, pl.ds(r1, PROJ_ROWS // 4, stride=4), :]
                qkv_refs[3 + t][hp, :, r1 * LANES:(r1 + 1) * LANES] = x4.astype(BF16)
                z4_ref[(t * n_hp + hp) * 4 + r1] = x4
        for hp in range(n_hp):
            for r1 in range(4):
                for r2 in range(4):
                    x16 = z4_ref[(t * n_hp + hp) * 4 + r1, pl.ds(r2, PROJ_ROWS // 16, stride=4), :]
                    r = 4 * r2 + r1
                    qkv_refs[6 + t][hp, :, r * LANES:(r + 1) * LANES] = x16.astype(BF16)


def _dilated_spec(hp, d):
    return pl.BlockSpec((None, hp, PROJ_ROWS // d, d * LANES), lambda bi, i: (bi, 0, i, 0))


def _ab_in(h, g, w_in, v_norm, w_s, b_s):
    b, s, d = h.shape
    hp = A_WIDTH // LANES
    width = w_in.shape[1]
    qkv_specs = [_dilated_spec(hp, dil) for dil in DILATIONS for _ in range(3)]
    qkv_shapes = [jax.ShapeDtypeStruct((b, hp, s // dil, dil * LANES), BF16) for dil in DILATIONS for _ in range(3)]
    outs = pl.pallas_call(
        _ab_in_kernel,
        grid=(b, s // PROJ_ROWS),
        in_specs=[
            pl.BlockSpec((None, PROJ_ROWS, d), lambda bi, i: (bi, i, 0)),
            _resident((1, d)),
            _resident((d, width)),
            _resident((1, B_WIDTH)),
            _resident((B_GROUPS, B_CHUNK, B_CHUNK)),
            _resident((B_CHUNK, B_GROUPS)),
        ],
        out_specs=qkv_specs + [pl.BlockSpec((None, PROJ_ROWS, B_WIDTH), lambda bi, i: (bi, i, 0))],
        out_shape=qkv_shapes + [jax.ShapeDtypeStruct((b, s, B_WIDTH), BF16)],
        scratch_shapes=[pltpu.VMEM((3 * hp, PROJ_ROWS, LANES), F32),
                        pltpu.VMEM((3 * hp * 4, PROJ_ROWS // 4, LANES), F32)],
        compiler_params=_params("parallel", "parallel"),
        name="ab_in_proj",
    )(h, g.reshape(1, d), w_in, v_norm.reshape(1, B_WIDTH), w_s, b_s.T)
    return [outs[3 * i:3 * i + 3] for i in range(len(DILATIONS))], outs[-1]


def _swa_kernel(q_ref, kp_ref, kc_ref, vp_ref, vc_ref, o_ref, lse_ref, kk_ref, vv_ref, *, rows, res):
    n = pl.program_id(3)
    nsub = rows // SWA_WIN
    kk_ref[0:SWA_WIN] = kp_ref[...]
    kk_ref[SWA_WIN:] = kc_ref[...]
    vv_ref[0:SWA_WIN] = vp_ref[...]
    vv_ref[SWA_WIN:] = vc_ref[...]
    cols = [slice(r * LANES, (r + 1) * LANES) for r in range(res)]
    q = jnp.concatenate([q_ref[:, c].reshape(nsub, SWA_WIN, LANES) for c in cols], axis=0)
    kwin = jnp.stack([kk_ref[j * SWA_WIN:(j + 2) * SWA_WIN, c] for c in cols for j in range(nsub)])
    vwin = jnp.stack([vv_ref[j * SWA_WIN:(j + 2) * SWA_WIN, c] for c in cols for j in range(nsub)])
    qi = lax.broadcasted_iota(jnp.int32, (SWA_WIN, 2 * SWA_WIN), 0)
    kc = lax.broadcasted_iota(jnp.int32, (SWA_WIN, 2 * SWA_WIN), 1)
    band = (kc >= qi) & (kc <= qi + SWA_WIN)
    bias = jnp.where(band, 0.0, NEG_BIG)
    bias_first = jnp.where(n == 0, jnp.where(band & (kc >= SWA_WIN), 0.0, NEG_BIG), bias)
    bias = jnp.stack([bias_first if j == 0 else bias for _ in cols for j in range(nsub)])
    q_head0 = lax.broadcasted_iota(jnp.int32, q.shape, 2) < HEAD_DIM
    zero = jnp.zeros_like(q)
    v_ext = jnp.concatenate([vwin, jnp.ones_like(vwin)], axis=-1)
    ms, pvs, dens = [], [], []
    for qm in (jnp.where(q_head0, q, zero), jnp.where(q_head0, zero, q)):
        s = jnp.einsum("bqd,bkd->bqk", qm, kwin, preferred_element_type=F32) + bias
        m = jnp.max(s, axis=-1, keepdims=True)
        p = jnp.exp2(s - m).astype(BF16)
        pv = jnp.einsum("bqk,bkd->bqd", p, v_ext, preferred_element_type=F32)
        ms.append(m)
        pvs.append(pv[..., :LANES])
        dens.append(pv[..., LANES:])
    den = jnp.where(q_head0, dens[0], dens[1])
    out = jnp.where(q_head0, pvs[0], pvs[1]) / den
    lse = jnp.where(q_head0, ms[0], ms[1]) * LN2 + jnp.log(den)
    for r, c in enumerate(cols):
        o_ref[:, c] = out[r * nsub:(r + 1) * nsub].reshape(rows, LANES).astype(BF16)
        lse_ref[:, c] = lse[r * nsub:(r + 1) * nsub].reshape(rows, LANES)


def _swa(q, k, v, dilation):
    b, hp, sub, _ = q.shape
    res = min(dilation, SWA_TOKENS // SWA_WIN)
    rows = min(sub, SWA_TOKENS // res)
    per_step = rows // SWA_WIN
    cur = pl.BlockSpec((None, None, rows, res * LANES), lambda bi, h, r, n: (bi, h, n, r))
    prev = pl.BlockSpec((None, None, SWA_WIN, res * LANES),
                        lambda bi, h, r, n: (bi, h, jnp.maximum(n * per_step - 1, 0), r))
    return pl.pallas_call(
        functools.partial(_swa_kernel, rows=rows, res=res),
        grid=(b, hp, dilation // res, sub // rows),
        in_specs=[cur, prev, cur, prev, cur],
        out_specs=[cur, cur],
        out_shape=[jax.ShapeDtypeStruct(q.shape, BF16), jax.ShapeDtypeStruct(q.shape, F32)],
        scratch_shapes=[pltpu.VMEM((rows + SWA_WIN, res * LANES), BF16),
                        pltpu.VMEM((rows + SWA_WIN, res * LANES), BF16)],
        compiler_params=_params("parallel", "parallel", "parallel", "parallel"),
        name=f"swa_d{dilation}",
    )(q, k, k, v, v)


def _ab_out_kernel(h_ref, *rest):
    n_pat = len(DILATIONS)
    o_refs, l_refs = rest[:n_pat], rest[n_pat:2 * n_pat]
    b_ref, w_ref, out_ref, cat_ref, tok_ref = rest[2 * n_pat:]
    for hp in range(A_WIDTH // LANES):
        outs, lses = [], []
        for i, d in enumerate(DILATIONS):
            vals = []
            for kind, ref in enumerate((o_refs[i], l_refs[i])):
                if d == 1:
                    vals.append(ref[hp].astype(F32))
                    continue
                slot = (hp * n_pat + i) * 2 + kind
                for r in range(d):
                    tok_ref[slot, pl.ds(r, PROJ_ROWS // d, stride=d), :] = (
                        ref[hp, :, r * LANES:(r + 1) * LANES].astype(F32))
                vals.append(tok_ref[slot])
            outs.append(vals[0])
            lses.append(vals[1])
        m = functools.reduce(jnp.maximum, lses)
        es = [jnp.exp(l - m) for l in lses]
        mix = sum(e * o for e, o in zip(es, outs)) / sum(es)
        cat_ref[:, hp * LANES:(hp + 1) * LANES] = mix.astype(BF16)
    cat_ref[:, A_WIDTH:] = b_ref[...]
    out_ref[...] = h_ref[...] + jnp.dot(cat_ref[...], w_ref[...], preferred_element_type=F32)


def _ab_out(h, outs, lses, b_out, w_out):
    b, s, d = h.shape
    hp = A_WIDTH // LANES
    row = pl.BlockSpec((None, PROJ_ROWS, d), lambda bi, i: (bi, i, 0))
    heads = [_dilated_spec(hp, dil) for dil in DILATIONS]
    return pl.pallas_call(
        _ab_out_kernel,
        grid=(b, s // PROJ_ROWS),
        in_specs=[row] + heads + heads + [pl.BlockSpec((None, PROJ_ROWS, B_WIDTH), lambda bi, i: (bi, i, 0)),
                                          _resident((A_WIDTH + B_WIDTH, d))],
        out_specs=row,
        out_shape=jax.ShapeDtypeStruct((b, s, d), F32),
        scratch_shapes=[pltpu.VMEM((PROJ_ROWS, A_WIDTH + B_WIDTH), BF16),
                        pltpu.VMEM((hp * len(DILATIONS) * 2, PROJ_ROWS, LANES), F32)],
        compiler_params=_params("parallel", "parallel"),
        name="ab_out_proj",
    )(h, *outs, *lses, b_out, w_out)


def _c_in_kernel(h_ref, g_ref, w_ref, q_ref, k_ref, v_ref):
    hn = _rms_norm(h_ref[...], g_ref[...]).astype(BF16)
    z = jnp.dot(hn, w_ref[...], preferred_element_type=F32)
    for hp in range(C_WIDTH // LANES):
        lo = hp * LANES
        q_ref[hp] = (z[:, lo:lo + LANES] * (SCALE * LOG2E)).astype(BF16)
        k_ref[hp] = z[:, C_WIDTH + lo:C_WIDTH + lo + LANES].astype(BF16)
        v_ref[hp] = z[:, 2 * C_WIDTH + lo:2 * C_WIDTH + lo + LANES].astype(BF16)


def _c_in(h, g, w_in):
    b, s, d = h.shape
    hp = C_WIDTH // LANES
    spec = pl.BlockSpec((None, hp, PROJ_ROWS, LANES), lambda bi, i: (bi, 0, i, 0))
    shape = jax.ShapeDtypeStruct((b, hp, s, LANES), BF16)
    return pl.pallas_call(
        _c_in_kernel,
        grid=(b, s // PROJ_ROWS),
        in_specs=[pl.BlockSpec((None, PROJ_ROWS, d), lambda bi, i: (bi, i, 0)),
                  _resident((1, d)), _resident((d, 3 * C_WIDTH))],
        out_specs=[spec, spec, spec],
        out_shape=[shape, shape, shape],
        compiler_params=_params("parallel", "parallel"),
        name="c_in_proj",
    )(h, g.reshape(1, d), w_in)


def _moba_select(q_ref, k_ref, qaug_ref, st_ref, m_ref, nb):
    seq = q_ref.shape[0]
    kmean = jnp.concatenate(
        [jnp.mean(k_ref[j * C_BLOCK:(j + 1) * C_BLOCK].astype(F32), axis=0, keepdims=True) for j in range(nb)],
        axis=0).astype(BF16)
    lane = lax.broadcasted_iota(jnp.int32, (MOBA_ROWS, LANES), 1)
    blk = lax.broadcasted_iota(jnp.int32, (nb, MOBA_ROWS), 0)
    qpos = lax.broadcasted_iota(jnp.int32, (nb, MOBA_ROWS), 1)
    zeros = lambda n: jnp.zeros((n, MOBA_ROWS), F32)

    def chunk(t, carry):
        r0 = pl.multiple_of(t * MOBA_ROWS, MOBA_ROWS)
        q = q_ref[pl.ds(r0, MOBA_ROWS), :]
        own = (qpos + r0) // C_BLOCK
        past = blk < own
        for h in range(2):
            qm = jnp.where(lane < HEAD_DIM if h == 0 else lane >= HEAD_DIM, q, jnp.zeros_like(q))
            gate = lax.dot_general(kmean, qm, CONTRACT_LAST, preferred_element_type=F32)
            gate = jnp.where(past, gate, -jnp.inf)
            chosen = jnp.zeros(gate.shape, jnp.bool_)
            for _ in range(C_TOPK):
                best = jnp.max(gate, axis=0, keepdims=True)
                first = jnp.min(jnp.where(gate == best, blk, nb), axis=0, keepdims=True)
                hit = blk == first
                chosen = chosen | hit
                gate = jnp.where(hit, -jnp.inf, gate)
            unselected = 1.0 - ((chosen & past) | (blk == own)).astype(F32)
            if h == 0:
                extra_t = jnp.concatenate([zeros(HEAD_DIM), unselected, zeros(HEAD_DIM - nb)], axis=0)
            else:
                extra_t = jnp.concatenate([unselected, zeros(LANES - nb)], axis=0)
            qaug_ref[h, pl.ds(r0, MOBA_ROWS), :] = (qm.astype(F32) + extra_t.T).astype(BF16)
        return carry

    lax.fori_loop(0, seq // MOBA_ROWS, chunk, 0, unroll=True)
    st_ref[...] = jnp.zeros_like(st_ref)
    m_ref[...] = jnp.full(m_ref.shape, M_INIT, F32)


def _moba_kernel(q_ref, k_ref, v_ref, o_ref, qaug_ref, st_ref, m_ref):
    j = pl.program_id(2)
    seq = q_ref.shape[0]
    nb = seq // C_BLOCK

    @pl.when(j == 0)
    def _():
        _moba_select(q_ref, k_ref, qaug_ref, st_ref, m_ref, nb)

    j0 = pl.multiple_of(j * MOBA_ROWS, MOBA_ROWS)
    kj = k_ref[pl.ds(j0, MOBA_ROWS), :]
    vj = v_ref[pl.ds(j0, MOBA_ROWS), :]
    lane = lax.broadcasted_iota(jnp.int32, (MOBA_ROWS, LANES), 1)
    key_blk = j * (MOBA_ROWS // C_BLOCK) + lax.broadcasted_iota(jnp.int32, (MOBA_ROWS, LANES), 0) // C_BLOCK
    in_head = [lane < HEAD_DIM, lane >= HEAD_DIM]
    flag_lane = [HEAD_DIM + key_blk, key_blk]
    k_aug = [jnp.where(in_head[h], kj, jnp.where(lane == flag_lane[h], SCORE_OFF, 0.0).astype(BF16))
             for h in range(2)]
    v_aug = [jnp.where(in_head[h], vj, jnp.ones_like(vj)) for h in range(2)]

    def update(h, rows, causal):
        qa = qaug_ref[h, pl.ds(rows, MOBA_ROWS), :]
        s = lax.dot_general(qa, k_aug[h], CONTRACT_LAST, preferred_element_type=F32)
        if causal:
            qi = lax.broadcasted_iota(jnp.int32, s.shape, 0)
            ki = lax.broadcasted_iota(jnp.int32, s.shape, 1)
            s = jnp.where(ki <= qi, s, SCORE_OFF)
        m_old = m_ref[h, pl.ds(rows, MOBA_ROWS), :]
        m_new = jnp.maximum(m_old, jnp.max(s, axis=-1, keepdims=True))
        p = jnp.exp2(s - jnp.concatenate([m_new] * (MOBA_ROWS // LANES), axis=1))
        pv = jnp.dot(p.astype(BF16), v_aug[h], preferred_element_type=F32)
        st = jnp.exp2(m_old - m_new) * st_ref[h, pl.ds(rows, MOBA_ROWS), :] + pv
        return st, m_new

    done = []
    for h in range(2):
        st, _ = update(h, j0, True)
        done.append(st / pltpu.roll(st, HEAD_DIM, 1))
    o_ref[pl.ds(j0, MOBA_ROWS), :] = jnp.where(in_head[0], done[0], done[1]).astype(BF16)

    def chunk(t, carry):
        rows = pl.multiple_of(t * MOBA_ROWS, MOBA_ROWS)
        for h in range(2):
            st, m_new = update(h, rows, False)
            st_ref[h, pl.ds(rows, MOBA_ROWS), :] = st
            m_ref[h, pl.ds(rows, MOBA_ROWS), :] = m_new
        return carry

    first = j + 1
    n_later = seq // MOBA_ROWS - first
    lax.fori_loop(0, n_later // 2, lambda i, c: chunk(first + 2 * i + 1, chunk(first + 2 * i, c)), 0)

    @pl.when(n_later % 2 == 1)
    def _():
        chunk(seq // MOBA_ROWS - 1, 0)


def _moba(q, k, v):
    b, hp, s, _ = q.shape
    whole = pl.BlockSpec((None, None, s, LANES), lambda bi, h, j: (bi, h, 0, 0))
    return pl.pallas_call(
        _moba_kernel,
        grid=(b, hp, s // MOBA_ROWS),
        in_specs=[whole, whole, whole],
        out_specs=whole,
        out_shape=jax.ShapeDtypeStruct((b, hp, s, LANES), BF16),
        scratch_shapes=[pltpu.VMEM((2, s, LANES), BF16),
                        pltpu.VMEM((2, s, LANES), F32),
                        pltpu.VMEM((2, s, LANES), F32)],
        compiler_params=_params("parallel", "parallel", "arbitrary"),
        name="moba",
    )(q, k, v)


def _c_out_kernel(h_ref, o_ref, w_ref, out_ref, cat_ref):
    for hp in range(C_WIDTH // LANES):
        cat_ref[:, hp * LANES:(hp + 1) * LANES] = o_ref[hp]
    out_ref[...] = h_ref[...] + jnp.dot(cat_ref[...], w_ref[...], preferred_element_type=F32)


def _c_out(h, o, w_out):
    b, s, d = h.shape
    hp = C_WIDTH // LANES
    row = pl.BlockSpec((None, PROJ_ROWS, d), lambda bi, i: (bi, i, 0))
    return pl.pallas_call(
        _c_out_kernel,
        grid=(b, s // PROJ_ROWS),
        in_specs=[row, pl.BlockSpec((None, hp, PROJ_ROWS, LANES), lambda bi, i: (bi, 0, i, 0)),
                  _resident((C_WIDTH, d))],
        out_specs=row,
        out_shape=jax.ShapeDtypeStruct((b, s, d), F32),
        scratch_shapes=[pltpu.VMEM((PROJ_ROWS, C_WIDTH), BF16)],
        compiler_params=_params("parallel", "parallel"),
        name="c_out_proj",
    )(h, o, w_out)


def kernel(x, ffn1_norm, ffn1_w_gate, ffn1_w_up, ffn1_w_down, mix_norm, ffn2_norm, ffn2_w_gate, ffn2_w_up,
           ffn2_w_down, ab_w_in, ab_v_norm, ab_w_spatial, ab_b_spatial, ab_w_out, c_w_in, c_w_out, final_norm):
    b, s, d = x.shape
    assert DILATIONS == (1, 4, 16)
    assert d == D_MODEL and s % SWA_TOKENS == 0 and s % MOBA_ROWS == 0 and (b * s) % FFN_ROWS == 0
    assert s // C_BLOCK <= HEAD_DIM
    bf = lambda w: w.astype(BF16)

    def ffn(h, layer, norm, wg, wu, wd, final_g=None):
        return _ffn(h.reshape(b * s, d), norm[layer], wg, wu, wd, layer, final_g).reshape(b, s, d)

    h = x
    h = ffn(h, 0, ffn1_norm, ffn1_w_gate, ffn1_w_up, ffn1_w_down)
    qkvs, b_out = _ab_in(h, mix_norm[0], bf(ab_w_in[0]), ab_v_norm[0], ab_w_spatial[0], ab_b_spatial[0])
    outs, lses = zip(*[_swa(*qkv, dil) for qkv, dil in zip(qkvs, DILATIONS)])
    h = _ab_out(h, outs, lses, b_out, bf(ab_w_out[0]))
    h = ffn(h, 0, ffn2_norm, ffn2_w_gate, ffn2_w_up, ffn2_w_down)
    h = ffn(h, 1, ffn1_norm, ffn1_w_gate, ffn1_w_up, ffn1_w_down)
    qc, kc, vc = _c_in(h, mix_norm[1], bf(c_w_in[0]))
    h = _c_out(h, _moba(qc, kc, vc), bf(c_w_out[0]))
    h = ffn(h, 1, ffn2_norm, ffn2_w_gate, ffn2_w_up, ffn2_w_down, final_g=final_norm)
    return h
```

```python
import functools

import jax
import jax.numpy as jnp
from jax import lax
from jax.experimental import pallas as pl
from jax.experimental.pallas import tpu as pltpu

F32 = jnp.float32
BF16 = jnp.bfloat16

D_MODEL = 1024
D_FF = 2816
EPS = 1e-6
HEAD_DIM = 64
LANES = 128
A_HEADS = 8
A_PATTERNS = ((128, 1), (512, 4), (2048, 16))
A_WIDTH = A_HEADS * HEAD_DIM
DILATIONS = tuple(d for _, d in A_PATTERNS)
B_GROUPS = 4
B_CHUNK = 128
B_WIDTH = B_GROUPS * LANES
C_HEADS = 16
C_WIDTH = C_HEADS * HEAD_DIM
C_BLOCK = 256
C_TOPK = 3
SCALE = HEAD_DIM ** -0.5
LOG2E = 1.4426950408889634
LN2 = 0.6931471805599453
NEG_BIG = -1e30

M_INIT = -(2.0 ** 60)
SCORE_OFF = -(2.0 ** 100)

VMEM_LIMIT_BYTES = 56 * 1024 * 1024

FFN_ROWS = 1024
FFN_CHUNK = 256
PROJ_ROWS = 1024
SWA_TOKENS = 4096
SWA_WIN = 128
MOBA_ROWS = 1024

CONTRACT_LAST = (((1,), (1,)), ((), ()))


def _params(*sem):
    return pltpu.CompilerParams(dimension_semantics=sem, vmem_limit_bytes=VMEM_LIMIT_BYTES)


def _rms_norm(x, g):
    return x * lax.rsqrt(jnp.mean(x * x, axis=-1, keepdims=True) + EPS) * g


def _resident(shape):
    nd = len(shape)
    return pl.BlockSpec(shape, lambda *_: (0,) * nd, pipeline_mode=pl.Buffered(1))


FFN_WEIGHT_STEPS = D_FF // FFN_CHUNK


def _ffn_kernel(x_ref, g_ref, wg_ref, wu_ref, wd_ref, *rest, final):
    if final:
        gf_ref, o_ref, wg_bf, wu_bf, wd_bf, xn_ref, acc_ref = rest
    else:
        o_ref, wg_bf, wu_bf, wd_bf, xn_ref, acc_ref = rest
    i = pl.program_id(0)

    def start():
        xn_ref[...] = _rms_norm(x_ref[...], g_ref[...]).astype(BF16)
        acc_ref[...] = jnp.zeros_like(acc_ref)

    def apply_chunk(c, carry=0):
        off = pl.multiple_of(c * FFN_CHUNK, FFN_CHUNK)
        xn = xn_ref[...]
        gate = jnp.dot(xn, wg_bf[:, pl.ds(off, FFN_CHUNK)], preferred_element_type=F32)
        up = jnp.dot(xn, wu_bf[:, pl.ds(off, FFN_CHUNK)], preferred_element_type=F32)
        act = (jax.nn.silu(gate) * up).astype(BF16)
        acc_ref[...] += jnp.dot(act, wd_bf[pl.ds(off, FFN_CHUNK), :], preferred_element_type=F32)
        return carry

    def finish():
        y = x_ref[...] + 0.5 * acc_ref[...]
        if final:
            y = _rms_norm(y, gf_ref[...])
        o_ref[...] = y

    @pl.when(i < FFN_WEIGHT_STEPS)
    def _():
        off = pl.multiple_of(i * FFN_CHUNK, FFN_CHUNK)
        wg_bf[:, pl.ds(off, FFN_CHUNK)] = wg_ref[...].astype(BF16)
        wu_bf[:, pl.ds(off, FFN_CHUNK)] = wu_ref[...].astype(BF16)
        wd_bf[pl.ds(off, FFN_CHUNK), :] = wd_ref[...].astype(BF16)
        pl.when(i == 0)(start)
        apply_chunk(i)
        pl.when(i == FFN_WEIGHT_STEPS - 1)(finish)

    @pl.when(i >= FFN_WEIGHT_STEPS)
    def _():
        start()
        lax.fori_loop(0, FFN_WEIGHT_STEPS, apply_chunk, 0, unroll=True)
        finish()


def _ffn(x2, g, wg, wu, wd, layer, final_g=None):
    t, d = x2.shape
    final = final_g is not None
    row_spec = pl.BlockSpec((FFN_ROWS, d), lambda i: (jnp.maximum(i - (FFN_WEIGHT_STEPS - 1), 0), 0))
    col_chunk = pl.BlockSpec((None, d, FFN_CHUNK), lambda i: (layer, 0, jnp.minimum(i, FFN_WEIGHT_STEPS - 1)))
    row_chunk = pl.BlockSpec((None, FFN_CHUNK, d), lambda i: (layer, jnp.minimum(i, FFN_WEIGHT_STEPS - 1), 0))
    in_specs = [row_spec, _resident((1, d)), col_chunk, col_chunk, row_chunk]
    args = [x2, g.reshape(1, d), wg, wu, wd]
    if final:
        in_specs.append(_resident((1, d)))
        args.append(final_g.reshape(1, d))
    return pl.pallas_call(
        functools.partial(_ffn_kernel, final=final),
        grid=(FFN_WEIGHT_STEPS - 1 + t // FFN_ROWS,),
        in_specs=in_specs,
        out_specs=row_spec,
        out_shape=jax.ShapeDtypeStruct((t, d), F32),
        scratch_shapes=[pltpu.VMEM((d, D_FF), BF16), pltpu.VMEM((d, D_FF), BF16), pltpu.VMEM((D_FF, d), BF16),
                        pltpu.VMEM((FFN_ROWS, d), BF16), pltpu.VMEM((FFN_ROWS, d), F32)],
        compiler_params=_params("arbitrary"),
        name="ffn_final" if final else "ffn",
    )(*args)


def _ab_in_kernel(h_ref, g_ref, w_ref, vn_ref, ws_ref, bs_ref, *rest):
    qkv_refs, b_ref, z1_ref, z4_ref = rest[:3 * len(DILATIONS)], rest[-3], rest[-2], rest[-1]
    n_hp = A_WIDTH // LANES
    hn = _rms_norm(h_ref[...], g_ref[...]).astype(BF16)

    def project(col, width):
        return jnp.dot(hn, w_ref[:, col:col + width], preferred_element_type=F32)

    row = lax.broadcasted_iota(jnp.int32, (B_CHUNK, B_CHUNK), 0)
    col = lax.broadcasted_iota(jnp.int32, (B_CHUNK, B_CHUNK), 1)
    causal = row >= col
    zu = project(3 * A_WIDTH, B_WIDTH)
    zv = project(3 * A_WIDTH + B_WIDTH, B_WIDTH)
    for g in range(B_GROUPS):
        lo = g * LANES
        u = jax.nn.gelu(zu[:, lo:lo + LANES])
        v = jax.nn.gelu(zv[:, lo:lo + LANES])
        vn = _rms_norm(v, vn_ref[:, lo:lo + LANES]).astype(BF16)
        ws = jnp.where(causal, ws_ref[g], 0.0).astype(BF16)
        bias = bs_ref[:, g:g + 1]
        for c in range(PROJ_ROWS // B_CHUNK):
            r0 = c * B_CHUNK
            mixed = jnp.dot(ws, vn[r0:r0 + B_CHUNK], preferred_element_type=F32) + bias
            b_ref[r0:r0 + B_CHUNK, lo:lo + LANES] = (u[r0:r0 + B_CHUNK] * mixed).astype(BF16)
    for t in range(3):
        z = project(t * A_WIDTH, A_WIDTH)
        if t == 0:
            z = z * (SCALE * LOG2E)
        for hp in range(n_hp):
            x1 = z[:, hp * LANES:(hp + 1) * LANES]
            qkv_refs[t][hp] = x1.astype(BF16)
            z1_ref[t * n_hp + hp] = x1
        for hp in range(n_hp):
            for r1 in range(4):
                x4 = z1_ref[t * n_hp + hp, pl.ds(r1, PROJ_ROWS // 4, stride=4), :]
                qkv_refs[3 + t][hp, :, r1 * LANES:(r1 + 1) * LANES] = x4.astype(BF16)
                z4_ref[(t * n_hp + hp) * 4 + r1] = x4
        for hp in range(n_hp):
            for r1 in range(4):
                for r2 in range(4):
                    x16 = z4_ref[(t * n_hp + hp) * 4 + r1, pl.ds(r2, PROJ_ROWS // 16, stride=4), :]
                    r = 4 * r2 + r1
                    qkv_refs[6 + t][hp, :, r * LANES:(r + 1) * LANES] = x16.astype(BF16)


def _dilated_spec(hp, d):
    return pl.BlockSpec((None, hp, PROJ_ROWS // d, d * LANES), lambda bi, i: (bi, 0, i, 0))


def _ab_in(h, g, w_in, v_norm, w_s, b_s):
    b, s, d = h.shape
    hp = A_WIDTH // LANES
    width = w_in.shape[1]
    qkv_specs = [_dilated_spec(hp, dil) for dil in DILATIONS for _ in range(3)]
    qkv_shapes = [jax.ShapeDtypeStruct((b, hp, s // dil, dil * LANES), BF16) for dil in DILATIONS for _ in range(3)]
    outs = pl.pallas_call(
        _ab_in_kernel,
        grid=(b, s // PROJ_ROWS),
        in_specs=[
            pl.BlockSpec((None, PROJ_ROWS, d), lambda bi, i: (bi, i, 0)),
            _resident((1, d)),
            _resident((d, width)),
            _resident((1, B_WIDTH)),
            _resident((B_GROUPS, B_CHUNK, B_CHUNK)),
            _resident((B_CHUNK, B_GROUPS)),
        ],
        out_specs=qkv_specs + [pl.BlockSpec((None, PROJ_ROWS, B_WIDTH), lambda bi, i: (bi, i, 0))],
        out_shape=qkv_shapes + [jax.ShapeDtypeStruct((b, s, B_WIDTH), BF16)],
        scratch_shapes=[pltpu.VMEM((3 * hp, PROJ_ROWS, LANES), F32),
                        pltpu.VMEM((3 * hp * 4, PROJ_ROWS // 4, LANES), F32)],
        compiler_params=_params("parallel", "parallel"),
        name="ab_in_proj",
    )(h, g.reshape(1, d), w_in, v_norm.reshape(1, B_WIDTH), w_s, b_s.T)
    return [outs[3 * i:3 * i + 3] for i in range(len(DILATIONS))], outs[-1]


def _swa_kernel(q_ref, kp_ref, kc_ref, vp_ref, vc_ref, o_ref, lse_ref, kk_ref, vv_ref, *, rows, res):
    n = pl.program_id(3)
    nsub = rows // SWA_WIN
    kk_ref[0:SWA_WIN] = kp_ref[...]
    kk_ref[SWA_WIN:] = kc_ref[...]
    vv_ref[0:SWA_WIN] = vp_ref[...]
    vv_ref[SWA_WIN:] = vc_ref[...]
    cols = [slice(r * LANES, (r + 1) * LANES) for r in range(res)]
    q = jnp.concatenate([q_ref[:, c].reshape(nsub, SWA_WIN, LANES) for c in cols], axis=0)
    kwin = jnp.stack([kk_ref[j * SWA_WIN:(j + 2) * SWA_WIN, c] for c in cols for j in range(nsub)])
    vwin = jnp.stack([vv_ref[j * SWA_WIN:(j + 2) * SWA_WIN, c] for c in cols for j in range(nsub)])
    qi = lax.broadcasted_iota(jnp.int32, (SWA_WIN, 2 * SWA_WIN), 0)
    kc = lax.broadcasted_iota(jnp.int32, (SWA_WIN, 2 * SWA_WIN), 1)
    band = (kc >= qi) & (kc <= qi + SWA_WIN)
    bias = jnp.where(band, 0.0, NEG_BIG)
    bias_first = jnp.where(n == 0, jnp.where(band & (kc >= SWA_WIN), 0.0, NEG_BIG), bias)
    bias = jnp.stack([bias_first if j == 0 else bias for _ in cols for j in range(nsub)])
    q_head0 = lax.broadcasted_iota(jnp.int32, q.shape, 2) < HEAD_DIM
    zero = jnp.zeros_like(q)
    v_ext = jnp.concatenate([vwin, jnp.ones_like(vwin)], axis=-1)
    ms, pvs, dens = [], [], []
    for qm in (jnp.where(q_head0, q, zero), jnp.where(q_head0, zero, q)):
        s = jnp.einsum("bqd,bkd->bqk", qm, kwin, preferred_element_type=F32) + bias
        m = jnp.max(s, axis=-1, keepdims=True)
        p = jnp.exp2(s - m).astype(BF16)
        pv = jnp.einsum("bqk,bkd->bqd", p, v_ext, preferred_element_type=F32)
        ms.append(m)
        pvs.append(pv[..., :LANES])
        dens.append(pv[..., LANES:])
    den = jnp.where(q_head0, dens[0], dens[1])
    out = jnp.where(q_head0, pvs[0], pvs[1]) / den
    lse = jnp.where(q_head0, ms[0], ms[1]) * LN2 + jnp.log(den)
    for r, c in enumerate(cols):
        o_ref[:, c] = out[r * nsub:(r + 1) * nsub].reshape(rows, LANES).astype(BF16)
        lse_ref[:, c] = lse[r * nsub:(r + 1) * nsub].reshape(rows, LANES)


def _swa(q, k, v, dilation):
    b, hp, sub, _ = q.shape
    res = min(dilation, SWA_TOKENS // SWA_WIN)
    rows = min(sub, SWA_TOKENS // res)
    per_step = rows // SWA_WIN
    cur = pl.BlockSpec((None, None, rows, res * LANES), lambda bi, h, r, n: (bi, h, n, r))
    prev = pl.BlockSpec((None, None, SWA_WIN, res * LANES),
                        lambda bi, h, r, n: (bi, h, jnp.maximum(n * per_step - 1, 0), r))
    return pl.pallas_call(
        functools.partial(_swa_kernel, rows=rows, res=res),
        grid=(b, hp, dilation // res, sub // rows),
        in_specs=[cur, prev, cur, prev, cur],
        out_specs=[cur, cur],
        out_shape=[jax.ShapeDtypeStruct(q.shape, BF16), jax.ShapeDtypeStruct(q.shape, F32)],
        scratch_shapes=[pltpu.VMEM((rows + SWA_WIN, res * LANES), BF16),
                        pltpu.VMEM((rows + SWA_WIN, res * LANES), BF16)],
        compiler_params=_params("parallel", "parallel", "parallel", "parallel"),
        name=f"swa_d{dilation}",
    )(q, k, k, v, v)


def _ab_out_kernel(h_ref, *rest):
    n_pat = len(DILATIONS)
    o_refs, l_refs = rest[:n_pat], rest[n_pat:2 * n_pat]
    b_ref, w_ref, out_ref, cat_ref, tok_ref = rest[2 * n_pat:]
    for hp in range(A_WIDTH // LANES):
        outs, lses = [], []
        for i, d in enumerate(DILATIONS):
            vals = []
            for kind, ref in enumerate((o_refs[i], l_refs[i])):
                if d == 1:
                    vals.append(ref[hp].astype(F32))
                    continue
                slot = (hp * n_pat + i) * 2 + kind
                for r in range(d):
                    tok_ref[slot, pl.ds(r, PROJ_ROWS // d, stride=d), :] = (
                        ref[hp, :, r * LANES:(r + 1) * LANES].astype(F32))
                vals.append(tok_ref[slot])
            outs.append(vals[0])
            lses.append(vals[1])
        m = functools.reduce(jnp.maximum, lses)
        es = [jnp.exp(l - m) for l in lses]
        mix = sum(e * o for e, o in zip(es, outs)) / sum(es)
        cat_ref[:, hp * LANES:(hp + 1) * LANES] = mix.astype(BF16)
    cat_ref[:, A_WIDTH:] = b_ref[...]
    out_ref[...] = h_ref[...] + jnp.dot(cat_ref[...], w_ref[...], preferred_element_type=F32)


def _ab_out(h, outs, lses, b_out, w_out):
    b, s, d = h.shape
    hp = A_WIDTH // LANES
    row = pl.BlockSpec((None, PROJ_ROWS, d), lambda bi, i: (bi, i, 0))
    heads = [_dilated_spec(hp, dil) for dil in DILATIONS]
    return pl.pallas_call(
        _ab_out_kernel,
        grid=(b, s // PROJ_ROWS),
        in_specs=[row] + heads + heads + [pl.BlockSpec((None, PROJ_ROWS, B_WIDTH), lambda bi, i: (bi, i, 0)),
                                          _resident((A_WIDTH + B_WIDTH, d))],
        out_specs=row,
        out_shape=jax.ShapeDtypeStruct((b, s, d), F32),
        scratch_shapes=[pltpu.VMEM((PROJ_ROWS, A_WIDTH + B_WIDTH), BF16),
                        pltpu.VMEM((hp * len(DILATIONS) * 2, PROJ_ROWS, LANES), F32)],
        compiler_params=_params("parallel", "parallel"),
        name="ab_out_proj",
    )(h, *outs, *lses, b_out, w_out)


def _c_in_kernel(h_ref, g_ref, w_ref, q_ref, k_ref, v_ref):
    hn = _rms_norm(h_ref[...], g_ref[...]).astype(BF16)
    z = jnp.dot(hn, w_ref[...], preferred_element_type=F32)
    for hp in range(C_WIDTH // LANES):
        lo = hp * LANES
        q_ref[hp] = (z[:, lo:lo + LANES] * (SCALE * LOG2E)).astype(BF16)
        k_ref[hp] = z[:, C_WIDTH + lo:C_WIDTH + lo + LANES].astype(BF16)
        v_ref[hp] = z[:, 2 * C_WIDTH + lo:2 * C_WIDTH + lo + LANES].astype(BF16)


def _c_in(h, g, w_in):
    b, s, d = h.shape
    hp = C_WIDTH // LANES
    spec = pl.BlockSpec((None, hp, PROJ_ROWS, LANES), lambda bi, i: (bi, 0, i, 0))
    shape = jax.ShapeDtypeStruct((b, hp, s, LANES), BF16)
    return pl.pallas_call(
        _c_in_kernel,
        grid=(b, s // PROJ_ROWS),
        in_specs=[pl.BlockSpec((None, PROJ_ROWS, d), lambda bi, i: (bi, i, 0)),
                  _resident((1, d)), _resident((d, 3 * C_WIDTH))],
        out_specs=[spec, spec, spec],
        out_shape=[shape, shape, shape],
        compiler_params=_params("parallel", "parallel"),
        name="c_in_proj",
    )(h, g.reshape(1, d), w_in)


def _moba_select(q_ref, k_ref, qaug_ref, nb):
    seq = q_ref.shape[0]
    kmean = jnp.concatenate(
        [jnp.mean(k_ref[j * C_BLOCK:(j + 1) * C_BLOCK].astype(F32), axis=0, keepdims=True) for j in range(nb)],
        axis=0).astype(BF16)
    lane = lax.broadcasted_iota(jnp.int32, (MOBA_ROWS, LANES), 1)
    blk = lax.broadcasted_iota(jnp.int32, (nb, MOBA_ROWS), 0)
    qpos = lax.broadcasted_iota(jnp.int32, (nb, MOBA_ROWS), 1)
    zeros = lambda n: jnp.zeros((n, MOBA_ROWS), F32)

    def chunk(t, carry):
        r0 = pl.multiple_of(t * MOBA_ROWS, MOBA_ROWS)
        q = q_ref[pl.ds(r0, MOBA_ROWS), :]
        own = (qpos + r0) // C_BLOCK
        past = blk < own
        for h in range(2):
            qm = jnp.where(lane < HEAD_DIM if h == 0 else lane >= HEAD_DIM, q, jnp.zeros_like(q))
            gate = lax.dot_general(kmean, qm, CONTRACT_LAST, preferred_element_type=F32)
            gate = jnp.where(past, gate, -jnp.inf)
            chosen = jnp.zeros(gate.shape, jnp.bool_)
            for _ in range(C_TOPK):
                best = jnp.max(gate, axis=0, keepdims=True)
                first = jnp.min(jnp.where(gate == best, blk, nb), axis=0, keepdims=True)
                hit = blk == first
                chosen = chosen | hit
                gate = jnp.where(hit, -jnp.inf, gate)
            unselected = 1.0 - ((chosen & past) | (blk == own)).astype(F32)
            if h == 0:
                extra_t = jnp.concatenate([zeros(HEAD_DIM), unselected, zeros(HEAD_DIM - nb)], axis=0)
            else:
                extra_t = jnp.concatenate([unselected, zeros(LANES - nb)], axis=0)
            qaug_ref[h, pl.ds(r0, MOBA_ROWS), :] = (qm.astype(F32) + extra_t.T).astype(BF16)
        return carry

    lax.fori_loop(0, seq // MOBA_ROWS, chunk, 0, unroll=True)


def _moba_kernel(q_ref, k_ref, v_ref, o_ref, qaug_ref, st_ref, m_ref):
    seq = q_ref.shape[0]
    nb = seq // C_BLOCK
    n_chunks = seq // MOBA_ROWS

    def step(j, first_group):
        j0 = j * MOBA_ROWS if first_group else pl.multiple_of(j * MOBA_ROWS, MOBA_ROWS)
        kj = k_ref[pl.ds(j0, MOBA_ROWS), :]
        vj = v_ref[pl.ds(j0, MOBA_ROWS), :]
        lane = lax.broadcasted_iota(jnp.int32, (MOBA_ROWS, LANES), 1)
        key_blk = j * (MOBA_ROWS // C_BLOCK) + lax.broadcasted_iota(jnp.int32, (MOBA_ROWS, LANES), 0) // C_BLOCK
        in_head = [lane < HEAD_DIM, lane >= HEAD_DIM]
        flag_lane = [HEAD_DIM + key_blk, key_blk]
        k_aug = [jnp.where(in_head[h], kj, jnp.where(lane == flag_lane[h], SCORE_OFF, 0.0).astype(BF16))
                 for h in range(2)]
        v_aug = [jnp.where(in_head[h], vj, jnp.ones_like(vj)) for h in range(2)]

        def update(h, rows, causal):
            qa = qaug_ref[h, pl.ds(rows, MOBA_ROWS), :]
            s = lax.dot_general(qa, k_aug[h], CONTRACT_LAST, preferred_element_type=F32)
            if causal:
                qi = lax.broadcasted_iota(jnp.int32, s.shape, 0)
                ki = lax.broadcasted_iota(jnp.int32, s.shape, 1)
                s = jnp.where(ki <= qi, s, SCORE_OFF)
            row_max = jnp.max(s, axis=-1, keepdims=True)
            if first_group:
                m_new = jnp.broadcast_to(jnp.maximum(row_max, M_INIT), (MOBA_ROWS, LANES))
            else:
                m_old = m_ref[h, pl.ds(rows, MOBA_ROWS), :]
                m_new = jnp.maximum(m_old, row_max)
            p = jnp.exp2(s - jnp.concatenate([m_new] * (MOBA_ROWS // LANES), axis=1))
            st = jnp.dot(p.astype(BF16), v_aug[h], preferred_element_type=F32)
            if not first_group:
                st = jnp.exp2(m_old - m_new) * st_ref[h, pl.ds(rows, MOBA_ROWS), :] + st
            return st, m_new

        done = []
        for h in range(2):
            st, _ = update(h, j0, True)
            done.append(st / pltpu.roll(st, HEAD_DIM, 1))
        o_ref[pl.ds(j0, MOBA_ROWS), :] = jnp.where(in_head[0], done[0], done[1]).astype(BF16)

        def chunk(t, carry=0):
            rows = t * MOBA_ROWS if first_group else pl.multiple_of(t * MOBA_ROWS, MOBA_ROWS)
            for h in range(2):
                st, m_new = update(h, rows, False)
                st_ref[h, pl.ds(rows, MOBA_ROWS), :] = st
                m_ref[h, pl.ds(rows, MOBA_ROWS), :] = m_new
            return carry

        if first_group:
            for t in range(1, n_chunks):
                chunk(t)
            return
        first = j + 1
        n_later = n_chunks - first
        lax.fori_loop(0, n_later // 2, lambda i, c: chunk(first + 2 * i + 1, chunk(first + 2 * i, c)), 0)

        @pl.when(n_later % 2 == 1)
        def _():
            chunk(n_chunks - 1)

    j = pl.program_id(2)

    @pl.when(j == 0)
    def _():
        _moba_select(q_ref, k_ref, qaug_ref, nb)
        step(0, True)

    @pl.when(j > 0)
    def _():
        step(j, False)


def _moba(q, k, v):
    b, hp, s, _ = q.shape
    whole = pl.BlockSpec((None, None, s, LANES), lambda bi, h, j: (bi, h, 0, 0))
    return pl.pallas_call(
        _moba_kernel,
        grid=(b, hp, s // MOBA_ROWS),
        in_specs=[whole, whole, whole],
        out_specs=whole,
        out_shape=jax.ShapeDtypeStruct((b, hp, s, LANES), BF16),
        scratch_shapes=[pltpu.VMEM((2, s, LANES), BF16),
                        pltpu.VMEM((2, s, LANES), F32),
                        pltpu.VMEM((2, s, LANES), F32)],
        compiler_params=_params("parallel", "parallel", "arbitrary"),
        name="moba",
    )(q, k, v)


def _c_out_kernel(h_ref, o_ref, w_ref, out_ref, cat_ref):
    for hp in range(C_WIDTH // LANES):
        cat_ref[:, hp * LANES:(hp + 1) * LANES] = o_ref[hp]
    out_ref[...] = h_ref[...] + jnp.dot(cat_ref[...], w_ref[...], preferred_element_type=F32)


def _c_out(h, o, w_out):
    b, s, d = h.shape
    hp = C_WIDTH // LANES
    row = pl.BlockSpec((None, PROJ_ROWS, d), lambda bi, i: (bi, i, 0))
    return pl.pallas_call(
        _c_out_kernel,
        grid=(b, s // PROJ_ROWS),
        in_specs=[row, pl.BlockSpec((None, hp, PROJ_ROWS, LANES), lambda bi, i: (bi, 0, i, 0)),
                  _resident((C_WIDTH, d))],
        out_specs=row,
        out_shape=jax.ShapeDtypeStruct((b, s, d), F32),
        scratch_shapes=[pltpu.VMEM((PROJ_ROWS, C_WIDTH), BF16)],
        compiler_params=_params("parallel", "parallel"),
        name="c_out_proj",
    )(h, o, w_out)


def kernel(x, ffn1_norm, ffn1_w_gate, ffn1_w_up, ffn1_w_down, mix_norm, ffn2_norm, ffn2_w_gate, ffn2_w_up,
           ffn2_w_down, ab_w_in, ab_v_norm, ab_w_spatial, ab_b_spatial, ab_w_out, c_w_in, c_w_out, final_norm):
    b, s, d = x.shape
    assert DILATIONS == (1, 4, 16)
    assert d == D_MODEL and s % SWA_TOKENS == 0 and s % MOBA_ROWS == 0 and (b * s) % FFN_ROWS == 0
    assert s // C_BLOCK <= HEAD_DIM
    bf = lambda w: w.astype(BF16)

    def ffn(h, layer, norm, wg, wu, wd, final_g=None):
        return _ffn(h.reshape(b * s, d), norm[layer], wg, wu, wd, layer, final_g).reshape(b, s, d)

    h = x
    h = ffn(h, 0, ffn1_norm, ffn1_w_gate, ffn1_w_up, ffn1_w_down)
    qkvs, b_out = _ab_in(h, mix_norm[0], bf(ab_w_in[0]), ab_v_norm[0], ab_w_spatial[0], ab_b_spatial[0])
    outs, lses = zip(*[_swa(*qkv, dil) for qkv, dil in zip(qkvs, DILATIONS)])
    h = _ab_out(h, outs, lses, b_out, bf(ab_w_out[0]))
    h = ffn(h, 0, ffn2_norm, ffn2_w_gate, ffn2_w_up, ffn2_w_down)
    h = ffn(h, 1, ffn1_norm, ffn1_w_gate, ffn1_w_up, ffn1_w_down)
    qc, kc, vc = _c_in(h, mix_norm[1], bf(c_w_in[0]))
    h = _c_out(h, _moba(qc, kc, vc), bf(c_w_out[0]))
    h = ffn(h, 1, ffn2_norm, ffn2_w_gate, ffn2_w_up, ffn2_w_down, final_g=final_norm)
    return h
```

```python
import functools

import jax
import jax.numpy as jnp
from jax import lax
from jax.experimental import pallas as pl
from jax.experimental.pallas import tpu as pltpu

F32 = jnp.float32
BF16 = jnp.bfloat16

D_MODEL = 1024
D_FF = 2816
EPS = 1e-6
HEAD_DIM = 64
LANES = 128
A_HEADS = 8
A_PATTERNS = ((128, 1), (512, 4), (2048, 16))
A_WIDTH = A_HEADS * HEAD_DIM
DILATIONS = tuple(d for _, d in A_PATTERNS)
B_GROUPS = 4
B_CHUNK = 128
B_WIDTH = B_GROUPS * LANES
C_HEADS = 16
C_WIDTH = C_HEADS * HEAD_DIM
C_BLOCK = 256
C_TOPK = 3
SCALE = HEAD_DIM ** -0.5
LOG2E = 1.4426950408889634
LN2 = 0.6931471805599453
NEG_BIG = -1e30

M_INIT = -(2.0 ** 60)
SCORE_OFF = -(2.0 ** 100)

VMEM_LIMIT_BYTES = 56 * 1024 * 1024

FFN_ROWS = 1024
FFN_CHUNK = 256
PROJ_ROWS = 1024
SWA_TOKENS = 4096
SWA_WIN = 128
MOBA_ROWS = 1024

CONTRACT_LAST = (((1,), (1,)), ((), ()))


def _params(*sem):
    return pltpu.CompilerParams(dimension_semantics=sem, vmem_limit_bytes=VMEM_LIMIT_BYTES)


def _rms_norm(x, g):
    return x * lax.rsqrt(jnp.mean(x * x, axis=-1, keepdims=True) + EPS) * g


def _resident(shape):
    nd = len(shape)
    return pl.BlockSpec(shape, lambda *_: (0,) * nd, pipeline_mode=pl.Buffered(1))


FFN_WEIGHT_STEPS = D_FF // FFN_CHUNK


def _ffn_kernel(x_ref, g_ref, wg_ref, wu_ref, wd_ref, *rest, final):
    if final:
        gf_ref, o_ref, wg_bf, wu_bf, wd_bf, xn_ref, acc_ref = rest
    else:
        o_ref, wg_bf, wu_bf, wd_bf, xn_ref, acc_ref = rest
    i = pl.program_id(0)

    def start():
        xn_ref[...] = _rms_norm(x_ref[...], g_ref[...]).astype(BF16)
        acc_ref[...] = jnp.zeros_like(acc_ref)

    def apply_chunk(c, carry=0):
        off = pl.multiple_of(c * FFN_CHUNK, FFN_CHUNK)
        xn = xn_ref[...]
        gate = jnp.dot(xn, wg_bf[:, pl.ds(off, FFN_CHUNK)], preferred_element_type=F32)
        up = jnp.dot(xn, wu_bf[:, pl.ds(off, FFN_CHUNK)], preferred_element_type=F32)
        act = (jax.nn.silu(gate) * up).astype(BF16)
        acc_ref[...] += jnp.dot(act, wd_bf[pl.ds(off, FFN_CHUNK), :], preferred_element_type=F32)
        return carry

    def finish():
        y = x_ref[...] + 0.5 * acc_ref[...]
        if final:
            y = _rms_norm(y, gf_ref[...])
        o_ref[...] = y

    @pl.when(i < FFN_WEIGHT_STEPS)
    def _():
        off = pl.multiple_of(i * FFN_CHUNK, FFN_CHUNK)
        wg_bf[:, pl.ds(off, FFN_CHUNK)] = wg_ref[...].astype(BF16)
        wu_bf[:, pl.ds(off, FFN_CHUNK)] = wu_ref[...].astype(BF16)
        wd_bf[pl.ds(off, FFN_CHUNK), :] = wd_ref[...].astype(BF16)
        pl.when(i == 0)(start)
        apply_chunk(i)
        pl.when(i == FFN_WEIGHT_STEPS - 1)(finish)

    @pl.when(i >= FFN_WEIGHT_STEPS)
    def _():
        start()
        lax.fori_loop(0, FFN_WEIGHT_STEPS, apply_chunk, 0, unroll=True)
        finish()


def _ffn(x2, g, wg, wu, wd, layer, final_g=None):
    t, d = x2.shape
    final = final_g is not None
    row_spec = pl.BlockSpec((FFN_ROWS, d), lambda i: (jnp.maximum(i - (FFN_WEIGHT_STEPS - 1), 0), 0))
    col_chunk = pl.BlockSpec((None, d, FFN_CHUNK), lambda i: (layer, 0, jnp.minimum(i, FFN_WEIGHT_STEPS - 1)))
    row_chunk = pl.BlockSpec((None, FFN_CHUNK, d), lambda i: (layer, jnp.minimum(i, FFN_WEIGHT_STEPS - 1), 0))
    in_specs = [row_spec, _resident((1, d)), col_chunk, col_chunk, row_chunk]
    args = [x2, g.reshape(1, d), wg, wu, wd]
    if final:
        in_specs.append(_resident((1, d)))
        args.append(final_g.reshape(1, d))
    return pl.pallas_call(
        functools.partial(_ffn_kernel, final=final),
        grid=(FFN_WEIGHT_STEPS - 1 + t // FFN_ROWS,),
        in_specs=in_specs,
        out_specs=row_spec,
        out_shape=jax.ShapeDtypeStruct((t, d), F32),
        scratch_shapes=[pltpu.VMEM((d, D_FF), BF16), pltpu.VMEM((d, D_FF), BF16), pltpu.VMEM((D_FF, d), BF16),
                        pltpu.VMEM((FFN_ROWS, d), BF16), pltpu.VMEM((FFN_ROWS, d), F32)],
        compiler_params=_params("arbitrary"),
        name="ffn_final" if final else "ffn",
    )(*args)


def _ab_in_kernel(h_ref, g_ref, w_ref, vn_ref, ws_ref, bs_ref, *rest):
    qkv_refs, b_ref, z1_ref, z4_ref = rest[:3 * len(DILATIONS)], rest[-3], rest[-2], rest[-1]
    n_hp = A_WIDTH // LANES
    hn = _rms_norm(h_ref[...], g_ref[...]).astype(BF16)

    def project(col, width):
        return jnp.dot(hn, w_ref[:, col:col + width], preferred_element_type=F32)

    row = lax.broadcasted_iota(jnp.int32, (B_CHUNK, B_CHUNK), 0)
    col = lax.broadcasted_iota(jnp.int32, (B_CHUNK, B_CHUNK), 1)
    causal = row >= col
    zu = project(3 * A_WIDTH, B_WIDTH)
    zv = project(3 * A_WIDTH + B_WIDTH, B_WIDTH)
    zs = [project(t * A_WIDTH, A_WIDTH) for t in range(3)]
    zs[0] = zs[0] * (SCALE * LOG2E)
    for g in range(B_GROUPS):
        lo = g * LANES
        u = jax.nn.gelu(zu[:, lo:lo + LANES])
        v = jax.nn.gelu(zv[:, lo:lo + LANES])
        vn = _rms_norm(v, vn_ref[:, lo:lo + LANES]).astype(BF16)
        ws = jnp.where(causal, ws_ref[g], 0.0).astype(BF16)
        bias = bs_ref[:, g:g + 1]
        for c in range(PROJ_ROWS // B_CHUNK):
            r0 = c * B_CHUNK
            mixed = jnp.dot(ws, vn[r0:r0 + B_CHUNK], preferred_element_type=F32) + bias
            b_ref[r0:r0 + B_CHUNK, lo:lo + LANES] = (u[r0:r0 + B_CHUNK] * mixed).astype(BF16)
    for t in range(3):
        z = zs[t]
        for hp in range(n_hp):
            x1 = z[:, hp * LANES:(hp + 1) * LANES]
            qkv_refs[t][hp] = x1.astype(BF16)
            z1_ref[t * n_hp + hp] = x1
        for hp in range(n_hp):
            for r1 in range(4):
                x4 = z1_ref[t * n_hp + hp, pl.ds(r1, PROJ_ROWS // 4, stride=4), :]
                qkv_refs[3 + t][hp, :, r1 * LANES:(r1 + 1) * LANES] = x4.astype(BF16)
                z4_ref[(t * n_hp + hp) * 4 + r1] = x4
        for hp in range(n_hp):
            for r1 in range(4):
                for r2 in range(4):
                    x16 = z4_ref[(t * n_hp + hp) * 4 + r1, pl.ds(r2, PROJ_ROWS // 16, stride=4), :]
                    r = 4 * r2 + r1
                    qkv_refs[6 + t][hp, :, r * LANES:(r + 1) * LANES] = x16.astype(BF16)


def _dilated_spec(hp, d):
    return pl.BlockSpec((None, hp, PROJ_ROWS // d, d * LANES), lambda bi, i: (bi, 0, i, 0))


def _ab_in(h, g, w_in, v_norm, w_s, b_s):
    b, s, d = h.shape
    hp = A_WIDTH // LANES
    width = w_in.shape[1]
    qkv_specs = [_dilated_spec(hp, dil) for dil in DILATIONS for _ in range(3)]
    qkv_shapes = [jax.ShapeDtypeStruct((b, hp, s // dil, dil * LANES), BF16) for dil in DILATIONS for _ in range(3)]
    outs = pl.pallas_call(
        _ab_in_kernel,
        grid=(b, s // PROJ_ROWS),
        in_specs=[
            pl.BlockSpec((None, PROJ_ROWS, d), lambda bi, i: (bi, i, 0)),
            _resident((1, d)),
            _resident((d, width)),
            _resident((1, B_WIDTH)),
            _resident((B_GROUPS, B_CHUNK, B_CHUNK)),
            _resident((B_CHUNK, B_GROUPS)),
        ],
        out_specs=qkv_specs + [pl.BlockSpec((None, PROJ_ROWS, B_WIDTH), lambda bi, i: (bi, i, 0))],
        out_shape=qkv_shapes + [jax.ShapeDtypeStruct((b, s, B_WIDTH), BF16)],
        scratch_shapes=[pltpu.VMEM((3 * hp, PROJ_ROWS, LANES), F32),
                        pltpu.VMEM((3 * hp * 4, PROJ_ROWS // 4, LANES), F32)],
        compiler_params=_params("parallel", "parallel"),
        name="ab_in_proj",
    )(h, g.reshape(1, d), w_in, v_norm.reshape(1, B_WIDTH), w_s, b_s.T)
    return [outs[3 * i:3 * i + 3] for i in range(len(DILATIONS))], outs[-1]


def _swa_kernel(q_ref, kp_ref, kc_ref, vp_ref, vc_ref, o_ref, lse_ref, kk_ref, vv_ref, *, rows, res):
    n = pl.program_id(3)
    nsub = rows // SWA_WIN
    kk_ref[0:SWA_WIN] = kp_ref[...]
    kk_ref[SWA_WIN:] = kc_ref[...]
    vv_ref[0:SWA_WIN] = vp_ref[...]
    vv_ref[SWA_WIN:] = vc_ref[...]
    cols = [slice(r * LANES, (r + 1) * LANES) for r in range(res)]
    q = jnp.concatenate([q_ref[:, c].reshape(nsub, SWA_WIN, LANES) for c in cols], axis=0)
    kwin = jnp.stack([kk_ref[j * SWA_WIN:(j + 2) * SWA_WIN, c] for c in cols for j in range(nsub)])
    vwin = jnp.stack([vv_ref[j * SWA_WIN:(j + 2) * SWA_WIN, c] for c in cols for j in range(nsub)])
    qi = lax.broadcasted_iota(jnp.int32, (SWA_WIN, 2 * SWA_WIN), 0)
    kc = lax.broadcasted_iota(jnp.int32, (SWA_WIN, 2 * SWA_WIN), 1)
    band = (kc >= qi) & (kc <= qi + SWA_WIN)
    bias = jnp.where(band, 0.0, NEG_BIG)
    bias_first = jnp.where(n == 0, jnp.where(band & (kc >= SWA_WIN), 0.0, NEG_BIG), bias)
    bias = jnp.stack([bias_first if j == 0 else bias for _ in cols for j in range(nsub)])
    q_head0 = lax.broadcasted_iota(jnp.int32, q.shape, 2) < HEAD_DIM
    zero = jnp.zeros_like(q)
    v_ext = jnp.concatenate([vwin, jnp.ones_like(vwin)], axis=-1)
    ms, pvs, dens = [], [], []
    for qm in (jnp.where(q_head0, q, zero), jnp.where(q_head0, zero, q)):
        s = jnp.einsum("bqd,bkd->bqk", qm, kwin, preferred_element_type=F32) + bias
        m = jnp.max(s, axis=-1, keepdims=True)
        p = jnp.exp2(s - m).astype(BF16)
        pv = jnp.einsum("bqk,bkd->bqd", p, v_ext, preferred_element_type=F32)
        ms.append(m)
        pvs.append(pv[..., :LANES])
        dens.append(pv[..., LANES:])
    den = jnp.where(q_head0, dens[0], dens[1])
    out = jnp.where(q_head0, pvs[0], pvs[1]) / den
    lse = jnp.where(q_head0, ms[0], ms[1]) * LN2 + jnp.log(den)
    for r, c in enumerate(cols):
        o_ref[:, c] = out[r * nsub:(r + 1) * nsub].reshape(rows, LANES).astype(BF16)
        lse_ref[:, c] = lse[r * nsub:(r + 1) * nsub].reshape(rows, LANES)


def _swa(q, k, v, dilation):
    b, hp, sub, _ = q.shape
    res = min(dilation, SWA_TOKENS // SWA_WIN)
    rows = min(sub, SWA_TOKENS // res)
    per_step = rows // SWA_WIN
    cur = pl.BlockSpec((None, None, rows, res * LANES), lambda bi, h, r, n: (bi, h, n, r))
    prev = pl.BlockSpec((None, None, SWA_WIN, res * LANES),
                        lambda bi, h, r, n: (bi, h, jnp.maximum(n * per_step - 1, 0), r))
    return pl.pallas_call(
        functools.partial(_swa_kernel, rows=rows, res=res),
        grid=(b, hp, dilation // res, sub // rows),
        in_specs=[cur, prev, cur, prev, cur],
        out_specs=[cur, cur],
        out_shape=[jax.ShapeDtypeStruct(q.shape, BF16), jax.ShapeDtypeStruct(q.shape, F32)],
        scratch_shapes=[pltpu.VMEM((rows + SWA_WIN, res * LANES), BF16),
                        pltpu.VMEM((rows + SWA_WIN, res * LANES), BF16)],
        compiler_params=_params("parallel", "parallel", "parallel", "parallel"),
        name=f"swa_d{dilation}",
    )(q, k, k, v, v)


def _ab_out_kernel(h_ref, *rest):
    n_pat = len(DILATIONS)
    o_refs, l_refs = rest[:n_pat], rest[n_pat:2 * n_pat]
    b_ref, w_ref, out_ref, cat_ref, tok_ref = rest[2 * n_pat:]
    for hp in range(A_WIDTH // LANES):
        outs, lses = [], []
        for i, d in enumerate(DILATIONS):
            vals = []
            for kind, ref in enumerate((o_refs[i], l_refs[i])):
                if d == 1:
                    vals.append(ref[hp].astype(F32))
                    continue
                slot = (hp * n_pat + i) * 2 + kind
                for r in range(d):
                    tok_ref[slot, pl.ds(r, PROJ_ROWS // d, stride=d), :] = (
                        ref[hp, :, r * LANES:(r + 1) * LANES].astype(F32))
                vals.append(tok_ref[slot])
            outs.append(vals[0])
            lses.append(vals[1])
        m = functools.reduce(jnp.maximum, lses)
        es = [jnp.exp(l - m) for l in lses]
        mix = sum(e * o for e, o in zip(es, outs)) / sum(es)
        cat_ref[:, hp * LANES:(hp + 1) * LANES] = mix.astype(BF16)
    cat_ref[:, A_WIDTH:] = b_ref[...]
    out_ref[...] = h_ref[...] + jnp.dot(cat_ref[...], w_ref[...], preferred_element_type=F32)


def _ab_out(h, outs, lses, b_out, w_out):
    b, s, d = h.shape
    hp = A_WIDTH // LANES
    row = pl.BlockSpec((None, PROJ_ROWS, d), lambda bi, i: (bi, i, 0))
    heads = [_dilated_spec(hp, dil) for dil in DILATIONS]
    return pl.pallas_call(
        _ab_out_kernel,
        grid=(b, s // PROJ_ROWS),
        in_specs=[row] + heads + heads + [pl.BlockSpec((None, PROJ_ROWS, B_WIDTH), lambda bi, i: (bi, i, 0)),
                                          _resident((A_WIDTH + B_WIDTH, d))],
        out_specs=row,
        out_shape=jax.ShapeDtypeStruct((b, s, d), F32),
        scratch_shapes=[pltpu.VMEM((PROJ_ROWS, A_WIDTH + B_WIDTH), BF16),
                        pltpu.VMEM((hp * len(DILATIONS) * 2, PROJ_ROWS, LANES), F32)],
        compiler_params=_params("parallel", "parallel"),
        name="ab_out_proj",
    )(h, *outs, *lses, b_out, w_out)


def _c_in_kernel(h_ref, g_ref, w_ref, q_ref, k_ref, v_ref):
    hn = _rms_norm(h_ref[...], g_ref[...]).astype(BF16)
    z = jnp.dot(hn, w_ref[...], preferred_element_type=F32)
    for hp in range(C_WIDTH // LANES):
        lo = hp * LANES
        q_ref[hp] = (z[:, lo:lo + LANES] * (SCALE * LOG2E)).astype(BF16)
        k_ref[hp] = z[:, C_WIDTH + lo:C_WIDTH + lo + LANES].astype(BF16)
        v_ref[hp] = z[:, 2 * C_WIDTH + lo:2 * C_WIDTH + lo + LANES].astype(BF16)


def _c_in(h, g, w_in):
    b, s, d = h.shape
    hp = C_WIDTH // LANES
    spec = pl.BlockSpec((None, hp, PROJ_ROWS, LANES), lambda bi, i: (bi, 0, i, 0))
    shape = jax.ShapeDtypeStruct((b, hp, s, LANES), BF16)
    return pl.pallas_call(
        _c_in_kernel,
        grid=(b, s // PROJ_ROWS),
        in_specs=[pl.BlockSpec((None, PROJ_ROWS, d), lambda bi, i: (bi, i, 0)),
                  _resident((1, d)), _resident((d, 3 * C_WIDTH))],
        out_specs=[spec, spec, spec],
        out_shape=[shape, shape, shape],
        compiler_params=_params("parallel", "parallel"),
        name="c_in_proj",
    )(h, g.reshape(1, d), w_in)


def _moba_select(q_ref, k_ref, qaug_ref, st_ref, m_ref, nb):
    seq = q_ref.shape[0]
    kmean = jnp.concatenate(
        [jnp.mean(k_ref[j * C_BLOCK:(j + 1) * C_BLOCK].astype(F32), axis=0, keepdims=True) for j in range(nb)],
        axis=0).astype(BF16)
    lane = lax.broadcasted_iota(jnp.int32, (MOBA_ROWS, LANES), 1)
    blk = lax.broadcasted_iota(jnp.int32, (nb, MOBA_ROWS), 0)
    qpos = lax.broadcasted_iota(jnp.int32, (nb, MOBA_ROWS), 1)
    zeros = lambda n: jnp.zeros((n, MOBA_ROWS), F32)

    def chunk(t, carry):
        r0 = pl.multiple_of(t * MOBA_ROWS, MOBA_ROWS)
        q = q_ref[pl.ds(r0, MOBA_ROWS), :]
        own = (qpos + r0) // C_BLOCK
        past = blk < own
        for h in range(2):
            qm = jnp.where(lane < HEAD_DIM if h == 0 else lane >= HEAD_DIM, q, jnp.zeros_like(q))
            gate = lax.dot_general(kmean, qm, CONTRACT_LAST, preferred_element_type=F32)
            gate = jnp.where(past, gate, -jnp.inf)
            chosen = jnp.zeros(gate.shape, jnp.bool_)
            for _ in range(C_TOPK):
                best = jnp.max(gate, axis=0, keepdims=True)
                first = jnp.min(jnp.where(gate == best, blk, nb), axis=0, keepdims=True)
                hit = blk == first
                chosen = chosen | hit
                gate = jnp.where(hit, -jnp.inf, gate)
            unselected = 1.0 - ((chosen & past) | (blk == own)).astype(F32)
            if h == 0:
                extra_t = jnp.concatenate([zeros(HEAD_DIM), unselected, zeros(HEAD_DIM - nb)], axis=0)
            else:
                extra_t = jnp.concatenate([unselected, zeros(LANES - nb)], axis=0)
            qaug_ref[h, pl.ds(r0, MOBA_ROWS), :] = (qm.astype(F32) + extra_t.T).astype(BF16)
        return carry

    lax.fori_loop(0, seq // MOBA_ROWS, chunk, 0, unroll=True)
    st_ref[...] = jnp.zeros_like(st_ref)
    m_ref[...] = jnp.full(m_ref.shape, M_INIT, F32)


def _moba_kernel(q_ref, k_ref, v_ref, o_ref, qaug_ref, st_ref, m_ref):
    j = pl.program_id(2)
    seq = q_ref.shape[0]
    nb = seq // C_BLOCK

    @pl.when(j == 0)
    def _():
        _moba_select(q_ref, k_ref, qaug_ref, st_ref, m_ref, nb)

    j0 = pl.multiple_of(j * MOBA_ROWS, MOBA_ROWS)
    kj = k_ref[pl.ds(j0, MOBA_ROWS), :]
    vj = v_ref[pl.ds(j0, MOBA_ROWS), :]
    lane = lax.broadcasted_iota(jnp.int32, (MOBA_ROWS, LANES), 1)
    key_blk = j * (MOBA_ROWS // C_BLOCK) + lax.broadcasted_iota(jnp.int32, (MOBA_ROWS, LANES), 0) // C_BLOCK
    in_head = [lane < HEAD_DIM, lane >= HEAD_DIM]
    flag_lane = [HEAD_DIM + key_blk, key_blk]
    k_aug = [jnp.where(in_head[h], kj, jnp.where(lane == flag_lane[h], SCORE_OFF, 0.0).astype(BF16))
             for h in range(2)]
    v_aug = [jnp.where(in_head[h], vj, jnp.ones_like(vj)) for h in range(2)]

    def update(h, rows, causal):
        qa = qaug_ref[h, pl.ds(rows, MOBA_ROWS), :]
        s = lax.dot_general(qa, k_aug[h], CONTRACT_LAST, preferred_element_type=F32)
        if causal:
            qi = lax.broadcasted_iota(jnp.int32, s.shape, 0)
            ki = lax.broadcasted_iota(jnp.int32, s.shape, 1)
            s = jnp.where(ki <= qi, s, SCORE_OFF)
        m_old = m_ref[h, pl.ds(rows, MOBA_ROWS), :]
        m_new = jnp.maximum(m_old, jnp.max(s, axis=-1, keepdims=True))
        p = jnp.exp2(s - jnp.concatenate([m_new] * (MOBA_ROWS // LANES), axis=1))
        pv = jnp.dot(p.astype(BF16), v_aug[h], preferred_element_type=F32)
        st = jnp.exp2(m_old - m_new) * st_ref[h, pl.ds(rows, MOBA_ROWS), :] + pv
        return st, m_new

    done = []
    for h in range(2):
        st, _ = update(h, j0, True)
        done.append(st / pltpu.roll(st, HEAD_DIM, 1))
    o_ref[pl.ds(j0, MOBA_ROWS), :] = jnp.where(in_head[0], done[0], done[1]).astype(BF16)

    def chunk(t, carry):
        rows = pl.multiple_of(t * MOBA_ROWS, MOBA_ROWS)
        for h in range(2):
            st, m_new = update(h, rows, False)
            st_ref[h, pl.ds(rows, MOBA_ROWS), :] = st
            m_ref[h, pl.ds(rows, MOBA_ROWS), :] = m_new
        return carry

    first = j + 1
    n_later = seq // MOBA_ROWS - first
    lax.fori_loop(0, n_later // 2, lambda i, c: chunk(first + 2 * i + 1, chunk(first + 2 * i, c)), 0)

    @pl.when(n_later % 2 == 1)
    def _():
        chunk(seq // MOBA_ROWS - 1, 0)


def _moba(q, k, v):
    b, hp, s, _ = q.shape
    whole = pl.BlockSpec((None, None, s, LANES), lambda bi, h, j: (bi, h, 0, 0))
    return pl.pallas_call(
        _moba_kernel,
        grid=(b, hp, s // MOBA_ROWS),
        in_specs=[whole, whole, whole],
        out_specs=whole,
        out_shape=jax.ShapeDtypeStruct((b, hp, s, LANES), BF16),
        scratch_shapes=[pltpu.VMEM((2, s, LANES), BF16),
                        pltpu.VMEM((2, s, LANES), F32),
                        pltpu.VMEM((2, s, LANES), F32)],
        compiler_params=_params("parallel", "parallel", "arbitrary"),
        name="moba",
    )(q, k, v)


def _c_out_kernel(h_ref, o_ref, w_ref, out_ref, cat_ref):
    for hp in range(C_WIDTH // LANES):
        cat_ref[:, hp * LANES:(hp + 1) * LANES] = o_ref[hp]
    out_ref[...] = h_ref[...] + jnp.dot(cat_ref[...], w_ref[...], preferred_element_type=F32)


def _c_out(h, o, w_out):
    b, s, d = h.shape
    hp = C_WIDTH // LANES
    row = pl.BlockSpec((None, PROJ_ROWS, d), lambda bi, i: (bi, i, 0))
    return pl.pallas_call(
        _c_out_kernel,
        grid=(b, s // PROJ_ROWS),
        in_specs=[row, pl.BlockSpec((None, hp, PROJ_ROWS, LANES), lambda bi, i: (bi, 0, i, 0)),
                  _resident((C_WIDTH, d))],
        out_specs=row,
        out_shape=jax.ShapeDtypeStruct((b, s, d), F32),
        scratch_shapes=[pltpu.VMEM((PROJ_ROWS, C_WIDTH), BF16)],
        compiler_params=_params("parallel", "parallel"),
        name="c_out_proj",
    )(h, o, w_out)


def kernel(x, ffn1_norm, ffn1_w_gate, ffn1_w_up, ffn1_w_down, mix_norm, ffn2_norm, ffn2_w_gate, ffn2_w_up,
           ffn2_w_down, ab_w_in, ab_v_norm, ab_w_spatial, ab_b_spatial, ab_w_out, c_w_in, c_w_out, final_norm):
    b, s, d = x.shape
    assert DILATIONS == (1, 4, 16)
    assert d == D_MODEL and s % SWA_TOKENS == 0 and s % MOBA_ROWS == 0 and (b * s) % FFN_ROWS == 0
    assert s // C_BLOCK <= HEAD_DIM
    bf = lambda w: w.astype(BF16)

    def ffn(h, layer, norm, wg, wu, wd, final_g=None):
        return _ffn(h.reshape(b * s, d), norm[layer], wg, wu, wd, layer, final_g).reshape(b, s, d)

    h = x
    h = ffn(h, 0, ffn1_norm, ffn1_w_gate, ffn1_w_up, ffn1_w_down)
    qkvs, b_out = _ab_in(h, mix_norm[0], bf(ab_w_in[0]), ab_v_norm[0], ab_w_spatial[0], ab_b_spatial[0])
    outs, lses = zip(*[_swa(*qkv, dil) for qkv, dil in zip(qkvs, DILATIONS)])
    h = _ab_out(h, outs, lses, b_out, bf(ab_w_out[0]))
    h = ffn(h, 0, ffn2_norm, ffn2_w_gate, ffn2_w_up, ffn2_w_down)
    h = ffn(h, 1, ffn1_norm, ffn1_w_gate, ffn1_w_up, ffn1_w_down)
    qc, kc, vc = _c_in(h, mix_norm[1], bf(c_w_in[0]))
    h = _c_out(h, _moba(qc, kc, vc), bf(c_w_out[0]))
    h = ffn(h, 1, ffn2_norm, ffn2_w_gate, ffn2_w_up, ffn2_w_down, final_g=final_norm)
    return h
```

```python
import functools

import jax
import jax.numpy as jnp
from jax import lax
from jax.experimental import pallas as pl
from jax.experimental.pallas import tpu as pltpu

F32 = jnp.float32
BF16 = jnp.bfloat16

D_MODEL = 1024
D_FF = 2816
EPS = 1e-6
HEAD_DIM = 64
LANES = 128
A_HEADS = 8
A_PATTERNS = ((128, 1), (512, 4), (2048, 16))
A_WIDTH = A_HEADS * HEAD_DIM
DILATIONS = tuple(d for _, d in A_PATTERNS)
B_GROUPS = 4
B_CHUNK = 128
B_WIDTH = B_GROUPS * LANES
C_HEADS = 16
C_WIDTH = C_HEADS * HEAD_DIM
C_BLOCK = 256
C_TOPK = 3
SCALE = HEAD_DIM ** -0.5
LOG2E = 1.4426950408889634
LN2 = 0.6931471805599453
NEG_BIG = -1e30

M_INIT = -(2.0 ** 60)
SCORE_OFF = -(2.0 ** 100)

VMEM_LIMIT_BYTES = 56 * 1024 * 1024

FFN_ROWS = 1024
FFN_CHUNK = 256
PROJ_ROWS = 1024
SWA_TOKENS = 4096
SWA_WIN = 128
MOBA_ROWS = 1024

CONTRACT_LAST = (((1,), (1,)), ((), ()))


def _params(*sem):
    return pltpu.CompilerParams(dimension_semantics=sem, vmem_limit_bytes=VMEM_LIMIT_BYTES)


def _rms_norm(x, g):
    return x * lax.rsqrt(jnp.mean(x * x, axis=-1, keepdims=True) + EPS) * g


def _resident(shape):
    nd = len(shape)
    return pl.BlockSpec(shape, lambda *_: (0,) * nd, pipeline_mode=pl.Buffered(1))


FFN_WEIGHT_STEPS = D_FF // FFN_CHUNK


def _ffn_kernel(x_ref, g_ref, wg_ref, wu_ref, wd_ref, *rest, final):
    if final:
        gf_ref, o_ref, wg_bf, wu_bf, wd_bf, xn_ref, acc_ref = rest
    else:
        o_ref, wg_bf, wu_bf, wd_bf, xn_ref, acc_ref = rest
    i = pl.program_id(0)

    def start():
        xn_ref[...] = _rms_norm(x_ref[...], g_ref[...]).astype(BF16)
        acc_ref[...] = jnp.zeros_like(acc_ref)

    def apply_chunk(c, carry=0):
        off = pl.multiple_of(c * FFN_CHUNK, FFN_CHUNK)
        xn = xn_ref[...]
        gate = jnp.dot(xn, wg_bf[:, pl.ds(off, FFN_CHUNK)], preferred_element_type=F32)
        up = jnp.dot(xn, wu_bf[:, pl.ds(off, FFN_CHUNK)], preferred_element_type=F32)
        act = (jax.nn.silu(gate) * up).astype(BF16)
        acc_ref[...] += jnp.dot(act, wd_bf[pl.ds(off, FFN_CHUNK), :], preferred_element_type=F32)
        return carry

    def finish():
        y = x_ref[...] + 0.5 * acc_ref[...]
        if final:
            y = _rms_norm(y, gf_ref[...])
        o_ref[...] = y

    @pl.when(i < FFN_WEIGHT_STEPS)
    def _():
        off = pl.multiple_of(i * FFN_CHUNK, FFN_CHUNK)
        wg_bf[:, pl.ds(off, FFN_CHUNK)] = wg_ref[...].astype(BF16)
        wu_bf[:, pl.ds(off, FFN_CHUNK)] = wu_ref[...].astype(BF16)
        wd_bf[pl.ds(off, FFN_CHUNK), :] = wd_ref[...].astype(BF16)
        pl.when(i == 0)(start)
        apply_chunk(i)
        pl.when(i == FFN_WEIGHT_STEPS - 1)(finish)

    @pl.when(i >= FFN_WEIGHT_STEPS)
    def _():
        start()
        lax.fori_loop(0, FFN_WEIGHT_STEPS, apply_chunk, 0, unroll=True)
        finish()


def _ffn(x2, g, wg, wu, wd, layer, final_g=None):
    t, d = x2.shape
    final = final_g is not None
    row_spec = pl.BlockSpec((FFN_ROWS, d), lambda i: (jnp.maximum(i - (FFN_WEIGHT_STEPS - 1), 0), 0))
    col_chunk = pl.BlockSpec((None, d, FFN_CHUNK), lambda i: (layer, 0, jnp.minimum(i, FFN_WEIGHT_STEPS - 1)))
    row_chunk = pl.BlockSpec((None, FFN_CHUNK, d), lambda i: (layer, jnp.minimum(i, FFN_WEIGHT_STEPS - 1), 0))
    in_specs = [row_spec, _resident((1, d)), col_chunk, col_chunk, row_chunk]
    args = [x2, g.reshape(1, d), wg, wu, wd]
    if final:
        in_specs.append(_resident((1, d)))
        args.append(final_g.reshape(1, d))
    return pl.pallas_call(
        functools.partial(_ffn_kernel, final=final),
        grid=(FFN_WEIGHT_STEPS - 1 + t // FFN_ROWS,),
        in_specs=in_specs,
        out_specs=row_spec,
        out_shape=jax.ShapeDtypeStruct((t, d), F32),
        scratch_shapes=[pltpu.VMEM((d, D_FF), BF16), pltpu.VMEM((d, D_FF), BF16), pltpu.VMEM((D_FF, d), BF16),
                        pltpu.VMEM((FFN_ROWS, d), BF16), pltpu.VMEM((FFN_ROWS, d), F32)],
        compiler_params=_params("arbitrary"),
        name="ffn_final" if final else "ffn",
    )(*args)


def _ab_in_kernel(h_ref, g_ref, w_ref, vn_ref, ws_ref, bs_ref, *rest):
    qkv_refs, b_ref, z1_ref, z4_ref = rest[:3 * len(DILATIONS)], rest[-3], rest[-2], rest[-1]
    n_hp = A_WIDTH // LANES
    hn = _rms_norm(h_ref[...], g_ref[...]).astype(BF16)

    def project(col, width):
        return jnp.dot(hn, w_ref[:, col:col + width], preferred_element_type=F32)

    row = lax.broadcasted_iota(jnp.int32, (B_CHUNK, B_CHUNK), 0)
    col = lax.broadcasted_iota(jnp.int32, (B_CHUNK, B_CHUNK), 1)
    causal = row >= col
    zu = project(3 * A_WIDTH, B_WIDTH)
    zv = project(3 * A_WIDTH + B_WIDTH, B_WIDTH)
    zs = [project(t * A_WIDTH, A_WIDTH) for t in range(3)]
    zs[0] = zs[0] * (SCALE * LOG2E)
    for g in range(B_GROUPS):
        lo = g * LANES
        u = jax.nn.gelu(zu[:, lo:lo + LANES])
        v = jax.nn.gelu(zv[:, lo:lo + LANES])
        vn = _rms_norm(v, vn_ref[:, lo:lo + LANES]).astype(BF16)
        ws = jnp.where(causal, ws_ref[g], 0.0).astype(BF16)
        bias = bs_ref[:, g:g + 1]
        for c in range(PROJ_ROWS // B_CHUNK):
            r0 = c * B_CHUNK
            mixed = jnp.dot(ws, vn[r0:r0 + B_CHUNK], preferred_element_type=F32) + bias
            b_ref[r0:r0 + B_CHUNK, lo:lo + LANES] = (u[r0:r0 + B_CHUNK] * mixed).astype(BF16)
    for t in range(3):
        z = zs[t]
        for hp in range(n_hp):
            x1 = z[:, hp * LANES:(hp + 1) * LANES]
            qkv_refs[t][hp] = x1.astype(BF16)
            z1_ref[t * n_hp + hp] = x1
        for hp in range(n_hp):
            for r1 in range(4):
                x4 = z1_ref[t * n_hp + hp, pl.ds(r1, PROJ_ROWS // 4, stride=4), :]
                qkv_refs[3 + t][hp, :, r1 * LANES:(r1 + 1) * LANES] = x4.astype(BF16)
                z4_ref[(t * n_hp + hp) * 4 + r1] = x4
        for hp in range(n_hp):
            for r1 in range(4):
                for r2 in range(4):
                    x16 = z4_ref[(t * n_hp + hp) * 4 + r1, pl.ds(r2, PROJ_ROWS // 16, stride=4), :]
                    r = 4 * r2 + r1
                    qkv_refs[6 + t][hp, :, r * LANES:(r + 1) * LANES] = x16.astype(BF16)


def _dilated_spec(hp, d):
    return pl.BlockSpec((None, hp, PROJ_ROWS // d, d * LANES), lambda bi, i: (bi, 0, i, 0))


def _ab_in(h, g, w_in, v_norm, w_s, b_s):
    b, s, d = h.shape
    hp = A_WIDTH // LANES
    width = w_in.shape[1]
    qkv_specs = [_dilated_spec(hp, dil) for dil in DILATIONS for _ in range(3)]
    qkv_shapes = [jax.ShapeDtypeStruct((b, hp, s // dil, dil * LANES), BF16) for dil in DILATIONS for _ in range(3)]
    outs = pl.pallas_call(
        _ab_in_kernel,
        grid=(b, s // PROJ_ROWS),
        in_specs=[
            pl.BlockSpec((None, PROJ_ROWS, d), lambda bi, i: (bi, i, 0)),
            _resident((1, d)),
            _resident((d, width)),
            _resident((1, B_WIDTH)),
            _resident((B_GROUPS, B_CHUNK, B_CHUNK)),
            _resident((B_CHUNK, B_GROUPS)),
        ],
        out_specs=qkv_specs + [pl.BlockSpec((None, PROJ_ROWS, B_WIDTH), lambda bi, i: (bi, i, 0))],
        out_shape=qkv_shapes + [jax.ShapeDtypeStruct((b, s, B_WIDTH), BF16)],
        scratch_shapes=[pltpu.VMEM((3 * hp, PROJ_ROWS, LANES), F32),
                        pltpu.VMEM((3 * hp * 4, PROJ_ROWS // 4, LANES), F32)],
        compiler_params=_params("parallel", "parallel"),
        name="ab_in_proj",
    )(h, g.reshape(1, d), w_in, v_norm.reshape(1, B_WIDTH), w_s, b_s.T)
    return [outs[3 * i:3 * i + 3] for i in range(len(DILATIONS))], outs[-1]


def _swa_kernel(q_ref, kp_ref, kc_ref, vp_ref, vc_ref, o_ref, lse_ref, kk_ref, vv_ref, *, rows, res):
    n = pl.program_id(3)
    nsub = rows // SWA_WIN
    kk_ref[0:SWA_WIN] = kp_ref[...]
    kk_ref[SWA_WIN:] = kc_ref[...]
    vv_ref[0:SWA_WIN] = vp_ref[...]
    vv_ref[SWA_WIN:] = vc_ref[...]
    cols = [slice(r * LANES, (r + 1) * LANES) for r in range(res)]
    q = jnp.concatenate([q_ref[:, c].reshape(nsub, SWA_WIN, LANES) for c in cols], axis=0)
    kwin = jnp.stack([kk_ref[j * SWA_WIN:(j + 2) * SWA_WIN, c] for c in cols for j in range(nsub)])
    vwin = jnp.stack([vv_ref[j * SWA_WIN:(j + 2) * SWA_WIN, c] for c in cols for j in range(nsub)])
    qi = lax.broadcasted_iota(jnp.int32, (SWA_WIN, 2 * SWA_WIN), 0)
    kc = lax.broadcasted_iota(jnp.int32, (SWA_WIN, 2 * SWA_WIN), 1)
    band = (kc >= qi) & (kc <= qi + SWA_WIN)
    bias = jnp.where(band, 0.0, NEG_BIG)
    bias_first = jnp.where(n == 0, jnp.where(band & (kc >= SWA_WIN), 0.0, NEG_BIG), bias)
    bias = jnp.stack([bias_first if j == 0 else bias for _ in cols for j in range(nsub)])
    q_head0 = lax.broadcasted_iota(jnp.int32, q.shape, 2) < HEAD_DIM
    zero = jnp.zeros_like(q)
    v_ext = jnp.concatenate([vwin, jnp.ones_like(vwin)], axis=-1)
    ms, pvs, dens = [], [], []
    for qm in (jnp.where(q_head0, q, zero), jnp.where(q_head0, zero, q)):
        s = jnp.einsum("bqd,bkd->bqk", qm, kwin, preferred_element_type=F32) + bias
        m = jnp.max(s, axis=-1, keepdims=True)
        p = jnp.exp2(s - m).astype(BF16)
        pv = jnp.einsum("bqk,bkd->bqd", p, v_ext, preferred_element_type=F32)
        ms.append(m)
        pvs.append(pv[..., :LANES])
        dens.append(pv[..., LANES:])
    den = jnp.where(q_head0, dens[0], dens[1])
    out = jnp.where(q_head0, pvs[0], pvs[1]) / den
    lse = jnp.where(q_head0, ms[0], ms[1]) * LN2 + jnp.log(den)
    for r, c in enumerate(cols):
        o_ref[:, c] = out[r * nsub:(r + 1) * nsub].reshape(rows, LANES).astype(BF16)
        lse_ref[:, c] = lse[r * nsub:(r + 1) * nsub].reshape(rows, LANES)


def _swa(q, k, v, dilation):
    b, hp, sub, _ = q.shape
    res = min(dilation, SWA_TOKENS // SWA_WIN)
    rows = min(sub, SWA_TOKENS // res)
    per_step = rows // SWA_WIN
    cur = pl.BlockSpec((None, None, rows, res * LANES), lambda bi, h, r, n: (bi, h, n, r))
    prev = pl.BlockSpec((None, None, SWA_WIN, res * LANES),
                        lambda bi, h, r, n: (bi, h, jnp.maximum(n * per_step - 1, 0), r))
    return pl.pallas_call(
        functools.partial(_swa_kernel, rows=rows, res=res),
        grid=(b, hp, dilation // res, sub // rows),
        in_specs=[cur, prev, cur, prev, cur],
        out_specs=[cur, cur],
        out_shape=[jax.ShapeDtypeStruct(q.shape, BF16), jax.ShapeDtypeStruct(q.shape, F32)],
        scratch_shapes=[pltpu.VMEM((rows + SWA_WIN, res * LANES), BF16),
                        pltpu.VMEM((rows + SWA_WIN, res * LANES), BF16)],
        compiler_params=_params("parallel", "parallel", "parallel", "parallel"),
        name=f"swa_d{dilation}",
    )(q, k, k, v, v)


def _ab_out_kernel(h_ref, *rest):
    n_pat = len(DILATIONS)
    o_refs, l_refs = rest[:n_pat], rest[n_pat:2 * n_pat]
    b_ref, w_ref, out_ref, cat_ref, tok_ref = rest[2 * n_pat:]
    for hp in range(A_WIDTH // LANES):
        outs, lses = [], []
        for i, d in enumerate(DILATIONS):
            vals = []
            for kind, ref in enumerate((o_refs[i], l_refs[i])):
                if d == 1:
                    vals.append(ref[hp].astype(F32))
                    continue
                slot = (hp * n_pat + i) * 2 + kind
                for r in range(d):
                    tok_ref[slot, pl.ds(r, PROJ_ROWS // d, stride=d), :] = (
                        ref[hp, :, r * LANES:(r + 1) * LANES].astype(F32))
                vals.append(tok_ref[slot])
            outs.append(vals[0])
            lses.append(vals[1])
        m = functools.reduce(jnp.maximum, lses)
        es = [jnp.exp(l - m) for l in lses]
        mix = sum(e * o for e, o in zip(es, outs)) / sum(es)
        cat_ref[:, hp * LANES:(hp + 1) * LANES] = mix.astype(BF16)
    cat_ref[:, A_WIDTH:] = b_ref[...]
    out_ref[...] = h_ref[...] + jnp.dot(cat_ref[...], w_ref[...], preferred_element_type=F32)


def _ab_out(h, outs, lses, b_out, w_out):
    b, s, d = h.shape
    hp = A_WIDTH // LANES
    row = pl.BlockSpec((None, PROJ_ROWS, d), lambda bi, i: (bi, i, 0))
    heads = [_dilated_spec(hp, dil) for dil in DILATIONS]
    return pl.pallas_call(
        _ab_out_kernel,
        grid=(b, s // PROJ_ROWS),
        in_specs=[row] + heads + heads + [pl.BlockSpec((None, PROJ_ROWS, B_WIDTH), lambda bi, i: (bi, i, 0)),
                                          _resident((A_WIDTH + B_WIDTH, d))],
        out_specs=row,
        out_shape=jax.ShapeDtypeStruct((b, s, d), F32),
        scratch_shapes=[pltpu.VMEM((PROJ_ROWS, A_WIDTH + B_WIDTH), BF16),
                        pltpu.VMEM((hp * len(DILATIONS) * 2, PROJ_ROWS, LANES), F32)],
        compiler_params=_params("parallel", "parallel"),
        name="ab_out_proj",
    )(h, *outs, *lses, b_out, w_out)


def _c_in_kernel(h_ref, g_ref, w_ref, q_ref, k_ref, v_ref):
    hn = _rms_norm(h_ref[...], g_ref[...]).astype(BF16)
    z = jnp.dot(hn, w_ref[...], preferred_element_type=F32)
    for hp in range(C_WIDTH // LANES):
        lo = hp * LANES
        q_ref[hp] = (z[:, lo:lo + LANES] * (SCALE * LOG2E)).astype(BF16)
        k_ref[hp] = z[:, C_WIDTH + lo:C_WIDTH + lo + LANES].astype(BF16)
        v_ref[hp] = z[:, 2 * C_WIDTH + lo:2 * C_WIDTH + lo + LANES].astype(BF16)


def _c_in(h, g, w_in):
    b, s, d = h.shape
    hp = C_WIDTH // LANES
    spec = pl.BlockSpec((None, hp, PROJ_ROWS, LANES), lambda bi, i: (bi, 0, i, 0))
    shape = jax.ShapeDtypeStruct((b, hp, s, LANES), BF16)
    return pl.pallas_call(
        _c_in_kernel,
        grid=(b, s // PROJ_ROWS),
        in_specs=[pl.BlockSpec((None, PROJ_ROWS, d), lambda bi, i: (bi, i, 0)),
                  _resident((1, d)), _resident((d, 3 * C_WIDTH))],
        out_specs=[spec, spec, spec],
        out_shape=[shape, shape, shape],
        compiler_params=_params("parallel", "parallel"),
        name="c_in_proj",
    )(h, g.reshape(1, d), w_in)


def _moba_select(q_ref, k_ref, qaug_ref, st_ref, m_ref, nb):
    seq = q_ref.shape[0]
    kmean = jnp.concatenate(
        [jnp.mean(k_ref[j * C_BLOCK:(j + 1) * C_BLOCK].astype(F32), axis=0, keepdims=True) for j in range(nb)],
        axis=0).astype(BF16)
    lane = lax.broadcasted_iota(jnp.int32, (MOBA_ROWS, LANES), 1)
    blk = lax.broadcasted_iota(jnp.int32, (nb, MOBA_ROWS), 0)
    qpos = lax.broadcasted_iota(jnp.int32, (nb, MOBA_ROWS), 1)
    zeros = lambda n: jnp.zeros((n, MOBA_ROWS), F32)

    def chunk(t, carry):
        r0 = pl.multiple_of(t * MOBA_ROWS, MOBA_ROWS)
        q = q_ref[pl.ds(r0, MOBA_ROWS), :]
        own = (qpos + r0) // C_BLOCK
        past = blk < own
        for h in range(2):
            qm = jnp.where(lane < HEAD_DIM if h == 0 else lane >= HEAD_DIM, q, jnp.zeros_like(q))
            gate = lax.dot_general(kmean, qm, CONTRACT_LAST, preferred_element_type=F32)
            gate = jnp.where(past, gate, -jnp.inf)
            chosen = jnp.zeros(gate.shape, jnp.bool_)
            for _ in range(C_TOPK):
                best = jnp.max(gate, axis=0, keepdims=True)
                first = jnp.min(jnp.where(gate == best, blk, nb), axis=0, keepdims=True)
                hit = blk == first
                chosen = chosen | hit
                gate = jnp.where(hit, -jnp.inf, gate)
            unselected = 1.0 - ((chosen & past) | (blk == own)).astype(F32)
            if h == 0:
                extra_t = jnp.concatenate([zeros(HEAD_DIM), unselected, zeros(HEAD_DIM - nb)], axis=0)
            else:
                extra_t = jnp.concatenate([unselected, zeros(LANES - nb)], axis=0)
            qaug_ref[h, pl.ds(r0, MOBA_ROWS), :] = (qm.astype(F32) + extra_t.T).astype(BF16)
        return carry

    lax.fori_loop(0, seq // MOBA_ROWS, chunk, 0, unroll=True)
    st_ref[...] = jnp.zeros_like(st_ref)
    m_ref[...] = jnp.full(m_ref.shape, M_INIT, F32)


def _moba_kernel(q_ref, k_ref, v_ref, o_ref, qaug_ref, st_ref, m_ref):
    j = pl.program_id(2)
    seq = q_ref.shape[0]
    nb = seq // C_BLOCK

    @pl.when(j == 0)
    def _():
        _moba_select(q_ref, k_ref, qaug_ref, st_ref, m_ref, nb)

    j0 = pl.multiple_of(j * MOBA_ROWS, MOBA_ROWS)
    kj = k_ref[pl.ds(j0, MOBA_ROWS), :]
    vj = v_ref[pl.ds(j0, MOBA_ROWS), :]
    lane = lax.broadcasted_iota(jnp.int32, (MOBA_ROWS, LANES), 1)
    key_blk = j * (MOBA_ROWS // C_BLOCK) + lax.broadcasted_iota(jnp.int32, (MOBA_ROWS, LANES), 0) // C_BLOCK
    in_head = [lane < HEAD_DIM, lane >= HEAD_DIM]
    flag_lane = [HEAD_DIM + key_blk, key_blk]
    k_aug = [jnp.where(in_head[h], kj, jnp.where(lane == flag_lane[h], SCORE_OFF, 0.0).astype(BF16))
             for h in range(2)]
    v_aug = [jnp.where(in_head[h], vj, jnp.ones_like(vj)) for h in range(2)]

    def update(h, rows, causal):
        qa = qaug_ref[h, pl.ds(rows, MOBA_ROWS), :]
        s = lax.dot_general(qa, k_aug[h], CONTRACT_LAST, preferred_element_type=F32)
        if causal:
            qi = lax.broadcasted_iota(jnp.int32, s.shape, 0)
            ki = lax.broadcasted_iota(jnp.int32, s.shape, 1)
            s = jnp.where(ki <= qi, s, SCORE_OFF)
        m_old = m_ref[h, pl.ds(rows, MOBA_ROWS), :]
        m_new = jnp.maximum(m_old, jnp.max(s, axis=-1, keepdims=True))
        p = jnp.exp2(s - jnp.concatenate([m_new] * (MOBA_ROWS // LANES), axis=1))
        pv = jnp.dot(p.astype(BF16), v_aug[h], preferred_element_type=F32)
        st = jnp.exp2(m_old - m_new) * st_ref[h, pl.ds(rows, MOBA_ROWS), :] + pv
        return st, m_new

    def diagonal():
        done = []
        for h in range(2):
            st, _ = update(h, j0, True)
            done.append(st / pltpu.roll(st, HEAD_DIM, 1))
        o_ref[pl.ds(j0, MOBA_ROWS), :] = jnp.where(in_head[0], done[0], done[1]).astype(BF16)

    def chunk(t, carry=0):
        rows = pl.multiple_of(t * MOBA_ROWS, MOBA_ROWS)
        for h in range(2):
            st, m_new = update(h, rows, False)
            st_ref[h, pl.ds(rows, MOBA_ROWS), :] = st
            m_ref[h, pl.ds(rows, MOBA_ROWS), :] = m_new
        return carry

    n_chunks = seq // MOBA_ROWS
    n_todo = n_chunks - j

    @pl.when(n_todo >= 2)
    def _():
        diagonal()
        chunk(j + 1)

    @pl.when(n_todo < 2)
    def _():
        diagonal()

    first = j + 2
    lax.fori_loop(0, (n_todo - 2) // 2, lambda i, c: chunk(first + 2 * i + 1, chunk(first + 2 * i, c)), 0)

    @pl.when((n_todo >= 3) & (n_todo % 2 == 1))
    def _():
        chunk(n_chunks - 1)


def _moba(q, k, v):
    b, hp, s, _ = q.shape
    whole = pl.BlockSpec((None, None, s, LANES), lambda bi, h, j: (bi, h, 0, 0))
    return pl.pallas_call(
        _moba_kernel,
        grid=(b, hp, s // MOBA_ROWS),
        in_specs=[whole, whole, whole],
        out_specs=whole,
        out_shape=jax.ShapeDtypeStruct((b, hp, s, LANES), BF16),
        scratch_shapes=[pltpu.VMEM((2, s, LANES), BF16),
                        pltpu.VMEM((2, s, LANES), F32),
                        pltpu.VMEM((2, s, LANES), F32)],
        compiler_params=_params("parallel", "parallel", "arbitrary"),
        name="moba",
    )(q, k, v)


def _c_out_kernel(h_ref, o_ref, w_ref, out_ref, cat_ref):
    for hp in range(C_WIDTH // LANES):
        cat_ref[:, hp * LANES:(hp + 1) * LANES] = o_ref[hp]
    out_ref[...] = h_ref[...] + jnp.dot(cat_ref[...], w_ref[...], preferred_element_type=F32)


def _c_out(h, o, w_out):
    b, s, d = h.shape
    hp = C_WIDTH // LANES
    row = pl.BlockSpec((None, PROJ_ROWS, d), lambda bi, i: (bi, i, 0))
    return pl.pallas_call(
        _c_out_kernel,
        grid=(b, s // PROJ_ROWS),
        in_specs=[row, pl.BlockSpec((None, hp, PROJ_ROWS, LANES), lambda bi, i: (bi, 0, i, 0)),
                  _resident((C_WIDTH, d))],
        out_specs=row,
        out_shape=jax.ShapeDtypeStruct((b, s, d), F32),
        scratch_shapes=[pltpu.VMEM((PROJ_ROWS, C_WIDTH), BF16)],
        compiler_params=_params("parallel", "parallel"),
        name="c_out_proj",
    )(h, o, w_out)


def kernel(x, ffn1_norm, ffn1_w_gate, ffn1_w_up, ffn1_w_down, mix_norm, ffn2_norm, ffn2_w_gate, ffn2_w_up,
           ffn2_w_down, ab_w_in, ab_v_norm, ab_w_spatial, ab_b_spatial, ab_w_out, c_w_in, c_w_out, final_norm):
    b, s, d = x.shape
    assert DILATIONS == (1, 4, 16)
    assert d == D_MODEL and s % SWA_TOKENS == 0 and s % MOBA_ROWS == 0 and (b * s) % FFN_ROWS == 0
    assert s // C_BLOCK <= HEAD_DIM
    bf = lambda w: w.astype(BF16)

    def ffn(h, layer, norm, wg, wu, wd, final_g=None):
        return _ffn(h.reshape(b * s, d), norm[layer], wg, wu, wd, layer, final_g).reshape(b, s, d)

    h = x
    h = ffn(h, 0, ffn1_norm, ffn1_w_gate, ffn1_w_up, ffn1_w_down)
    qkvs, b_out = _ab_in(h, mix_norm[0], bf(ab_w_in[0]), ab_v_norm[0], ab_w_spatial[0], ab_b_spatial[0])
    outs, lses = zip(*[_swa(*qkv, dil) for qkv, dil in zip(qkvs, DILATIONS)])
    h = _ab_out(h, outs, lses, b_out, bf(ab_w_out[0]))
    h = ffn(h, 0, ffn2_norm, ffn2_w_gate, ffn2_w_up, ffn2_w_down)
    h = ffn(h, 1, ffn1_norm, ffn1_w_gate, ffn1_w_up, ffn1_w_down)
    qc, kc, vc = _c_in(h, mix_norm[1], bf(c_w_in[0]))
    h = _c_out(h, _moba(qc, kc, vc), bf(c_w_out[0]))
    h = ffn(h, 1, ffn2_norm, ffn2_w_gate, ffn2_w_up, ffn2_w_down, final_g=final_norm)
    return h
```

```python
import functools

import jax
import jax.numpy as jnp
from jax import lax
from jax.experimental import pallas as pl
from jax.experimental.pallas import tpu as pltpu

F32 = jnp.float32
BF16 = jnp.bfloat16

D_MODEL = 1024
D_FF = 2816
EPS = 1e-6
HEAD_DIM = 64
LANES = 128
A_HEADS = 8
A_PATTERNS = ((128, 1), (512, 4), (2048, 16))
A_WIDTH = A_HEADS * HEAD_DIM
DILATIONS = tuple(d for _, d in A_PATTERNS)
B_GROUPS = 4
B_CHUNK = 128
B_WIDTH = B_GROUPS * LANES
C_HEADS = 16
C_WIDTH = C_HEADS * HEAD_DIM
C_BLOCK = 256
C_TOPK = 3
SCALE = HEAD_DIM ** -0.5
LOG2E = 1.4426950408889634
LN2 = 0.6931471805599453
NEG_BIG = -1e30

M_INIT = -(2.0 ** 60)
SCORE_OFF = -(2.0 ** 100)

VMEM_LIMIT_BYTES = 56 * 1024 * 1024

FFN_ROWS = 1024
FFN_CHUNK = 256
PROJ_ROWS = 1024
SWA_TOKENS = 4096
SWA_WIN = 128
MOBA_ROWS = 1024

CONTRACT_LAST = (((1,), (1,)), ((), ()))


def _params(*sem, fuse_operand=None, n_operands=0):
    fusion = None if fuse_operand is None else [i == fuse_operand for i in range(n_operands)]
    return pltpu.CompilerParams(dimension_semantics=sem, vmem_limit_bytes=VMEM_LIMIT_BYTES, allow_input_fusion=fusion)


def _rms_norm(x, g):
    return x * lax.rsqrt(jnp.mean(x * x, axis=-1, keepdims=True) + EPS) * g


def _resident(shape):
    nd = len(shape)
    return pl.BlockSpec(shape, lambda *_: (0,) * nd, pipeline_mode=pl.Buffered(1))


FFN_WEIGHT_STEPS = D_FF // FFN_CHUNK


def _ffn_kernel(x_ref, g_ref, wg_ref, wu_ref, wd_ref, *rest, final):
    if final:
        gf_ref, o_ref, wg_bf, wu_bf, wd_bf, xn_ref, acc_ref = rest
    else:
        o_ref, wg_bf, wu_bf, wd_bf, xn_ref, acc_ref = rest
    i = pl.program_id(0)

    def start():
        xn_ref[...] = _rms_norm(x_ref[...], g_ref[...]).astype(BF16)
        acc_ref[...] = jnp.zeros_like(acc_ref)

    def apply_chunk(c, carry=0):
        off = pl.multiple_of(c * FFN_CHUNK, FFN_CHUNK)
        xn = xn_ref[...]
        gate = jnp.dot(xn, wg_bf[:, pl.ds(off, FFN_CHUNK)], preferred_element_type=F32)
        up = jnp.dot(xn, wu_bf[:, pl.ds(off, FFN_CHUNK)], preferred_element_type=F32)
        act = (jax.nn.silu(gate) * up).astype(BF16)
        acc_ref[...] += jnp.dot(act, wd_bf[pl.ds(off, FFN_CHUNK), :], preferred_element_type=F32)
        return carry

    def finish():
        y = x_ref[...] + 0.5 * acc_ref[...]
        if final:
            y = _rms_norm(y, gf_ref[...])
        o_ref[...] = y

    @pl.when(i < FFN_WEIGHT_STEPS)
    def _():
        off = pl.multiple_of(i * FFN_CHUNK, FFN_CHUNK)
        wg_bf[:, pl.ds(off, FFN_CHUNK)] = wg_ref[...].astype(BF16)
        wu_bf[:, pl.ds(off, FFN_CHUNK)] = wu_ref[...].astype(BF16)
        wd_bf[pl.ds(off, FFN_CHUNK), :] = wd_ref[...].astype(BF16)
        pl.when(i == 0)(start)
        apply_chunk(i)
        pl.when(i == FFN_WEIGHT_STEPS - 1)(finish)

    @pl.when(i >= FFN_WEIGHT_STEPS)
    def _():
        start()
        lax.fori_loop(0, FFN_WEIGHT_STEPS, apply_chunk, 0, unroll=True)
        finish()


def _ffn(x2, g, wg, wu, wd, layer, final_g=None):
    t, d = x2.shape
    final = final_g is not None
    row_spec = pl.BlockSpec((FFN_ROWS, d), lambda i: (jnp.maximum(i - (FFN_WEIGHT_STEPS - 1), 0), 0))
    col_chunk = pl.BlockSpec((None, d, FFN_CHUNK), lambda i: (layer, 0, jnp.minimum(i, FFN_WEIGHT_STEPS - 1)))
    row_chunk = pl.BlockSpec((None, FFN_CHUNK, d), lambda i: (layer, jnp.minimum(i, FFN_WEIGHT_STEPS - 1), 0))
    in_specs = [row_spec, _resident((1, d)), col_chunk, col_chunk, row_chunk]
    args = [x2, g.reshape(1, d), wg, wu, wd]
    if final:
        in_specs.append(_resident((1, d)))
        args.append(final_g.reshape(1, d))
    return pl.pallas_call(
        functools.partial(_ffn_kernel, final=final),
        grid=(FFN_WEIGHT_STEPS - 1 + t // FFN_ROWS,),
        in_specs=in_specs,
        out_specs=row_spec,
        out_shape=jax.ShapeDtypeStruct((t, d), F32),
        scratch_shapes=[pltpu.VMEM((d, D_FF), BF16), pltpu.VMEM((d, D_FF), BF16), pltpu.VMEM((D_FF, d), BF16),
                        pltpu.VMEM((FFN_ROWS, d), BF16), pltpu.VMEM((FFN_ROWS, d), F32)],
        compiler_params=_params("arbitrary"),
        name="ffn_final" if final else "ffn",
    )(*args)


def _ab_in_kernel(h_ref, g_ref, w_ref, vn_ref, ws_ref, bs_ref, *rest):
    qkv_refs, b_ref, z1_ref, z4_ref = rest[:3 * len(DILATIONS)], rest[-3], rest[-2], rest[-1]
    n_hp = A_WIDTH // LANES
    hn = _rms_norm(h_ref[...], g_ref[...]).astype(BF16)

    def project(col, width):
        return jnp.dot(hn, w_ref[:, col:col + width], preferred_element_type=F32)

    row = lax.broadcasted_iota(jnp.int32, (B_CHUNK, B_CHUNK), 0)
    col = lax.broadcasted_iota(jnp.int32, (B_CHUNK, B_CHUNK), 1)
    causal = row >= col
    zu = project(3 * A_WIDTH, B_WIDTH)
    zv = project(3 * A_WIDTH + B_WIDTH, B_WIDTH)
    zs = [project(t * A_WIDTH, A_WIDTH) for t in range(3)]
    zs[0] = zs[0] * (SCALE * LOG2E)
    for g in range(B_GROUPS):
        lo = g * LANES
        u = jax.nn.gelu(zu[:, lo:lo + LANES])
        v = jax.nn.gelu(zv[:, lo:lo + LANES])
        vn = _rms_norm(v, vn_ref[:, lo:lo + LANES]).astype(BF16)
        ws = jnp.where(causal, ws_ref[g], 0.0).astype(BF16)
        bias = bs_ref[:, g:g + 1]
        for c in range(PROJ_ROWS // B_CHUNK):
            r0 = c * B_CHUNK
            mixed = jnp.dot(ws, vn[r0:r0 + B_CHUNK], preferred_element_type=F32) + bias
            b_ref[r0:r0 + B_CHUNK, lo:lo + LANES] = (u[r0:r0 + B_CHUNK] * mixed).astype(BF16)
    for t in range(3):
        z = zs[t]
        for hp in range(n_hp):
            x1 = z[:, hp * LANES:(hp + 1) * LANES]
            qkv_refs[t][hp] = x1.astype(BF16)
            z1_ref[t * n_hp + hp] = x1
        for hp in range(n_hp):
            for r1 in range(4):
                x4 = z1_ref[t * n_hp + hp, pl.ds(r1, PROJ_ROWS // 4, stride=4), :]
                qkv_refs[3 + t][hp, :, r1 * LANES:(r1 + 1) * LANES] = x4.astype(BF16)
                z4_ref[(t * n_hp + hp) * 4 + r1] = x4
        for hp in range(n_hp):
            for r1 in range(4):
                for r2 in range(4):
                    x16 = z4_ref[(t * n_hp + hp) * 4 + r1, pl.ds(r2, PROJ_ROWS // 16, stride=4), :]
                    r = 4 * r2 + r1
                    qkv_refs[6 + t][hp, :, r * LANES:(r + 1) * LANES] = x16.astype(BF16)


def _dilated_spec(hp, d):
    return pl.BlockSpec((None, hp, PROJ_ROWS // d, d * LANES), lambda bi, i: (bi, 0, i, 0))


def _ab_in(h, g, w_in, v_norm, w_s, b_s):
    b, s, d = h.shape
    hp = A_WIDTH // LANES
    width = w_in.shape[1]
    qkv_specs = [_dilated_spec(hp, dil) for dil in DILATIONS for _ in range(3)]
    qkv_shapes = [jax.ShapeDtypeStruct((b, hp, s // dil, dil * LANES), BF16) for dil in DILATIONS for _ in range(3)]
    outs = pl.pallas_call(
        _ab_in_kernel,
        grid=(b, s // PROJ_ROWS),
        in_specs=[
            pl.BlockSpec((None, PROJ_ROWS, d), lambda bi, i: (bi, i, 0)),
            _resident((1, d)),
            _resident((d, width)),
            _resident((1, B_WIDTH)),
            _resident((B_GROUPS, B_CHUNK, B_CHUNK)),
            _resident((B_CHUNK, B_GROUPS)),
        ],
        out_specs=qkv_specs + [pl.BlockSpec((None, PROJ_ROWS, B_WIDTH), lambda bi, i: (bi, i, 0))],
        out_shape=qkv_shapes + [jax.ShapeDtypeStruct((b, s, B_WIDTH), BF16)],
        scratch_shapes=[pltpu.VMEM((3 * hp, PROJ_ROWS, LANES), F32),
                        pltpu.VMEM((3 * hp * 4, PROJ_ROWS // 4, LANES), F32)],
        compiler_params=_params("parallel", "parallel", fuse_operand=2, n_operands=6),
        name="ab_in_proj",
    )(h, g.reshape(1, d), w_in, v_norm.reshape(1, B_WIDTH), w_s, b_s.T)
    return [outs[3 * i:3 * i + 3] for i in range(len(DILATIONS))], outs[-1]


def _swa_kernel(q_ref, kp_ref, kc_ref, vp_ref, vc_ref, o_ref, lse_ref, kk_ref, vv_ref, *, rows, res):
    n = pl.program_id(3)
    nsub = rows // SWA_WIN
    kk_ref[0:SWA_WIN] = kp_ref[...]
    kk_ref[SWA_WIN:] = kc_ref[...]
    vv_ref[0:SWA_WIN] = vp_ref[...]
    vv_ref[SWA_WIN:] = vc_ref[...]
    cols = [slice(r * LANES, (r + 1) * LANES) for r in range(res)]
    q = jnp.concatenate([q_ref[:, c].reshape(nsub, SWA_WIN, LANES) for c in cols], axis=0)
    kwin = jnp.stack([kk_ref[j * SWA_WIN:(j + 2) * SWA_WIN, c] for c in cols for j in range(nsub)])
    vwin = jnp.stack([vv_ref[j * SWA_WIN:(j + 2) * SWA_WIN, c] for c in cols for j in range(nsub)])
    qi = lax.broadcasted_iota(jnp.int32, (SWA_WIN, 2 * SWA_WIN), 0)
    kc = lax.broadcasted_iota(jnp.int32, (SWA_WIN, 2 * SWA_WIN), 1)
    band = (kc >= qi) & (kc <= qi + SWA_WIN)
    bias = jnp.where(band, 0.0, NEG_BIG)
    bias_first = jnp.where(n == 0, jnp.where(band & (kc >= SWA_WIN), 0.0, NEG_BIG), bias)
    bias = jnp.stack([bias_first if j == 0 else bias for _ in cols for j in range(nsub)])
    q_head0 = lax.broadcasted_iota(jnp.int32, q.shape, 2) < HEAD_DIM
    zero = jnp.zeros_like(q)
    v_ext = jnp.concatenate([vwin, jnp.ones_like(vwin)], axis=-1)
    ms, pvs, dens = [], [], []
    for qm in (jnp.where(q_head0, q, zero), jnp.where(q_head0, zero, q)):
        s = jnp.einsum("bqd,bkd->bqk", qm, kwin, preferred_element_type=F32) + bias
        m = jnp.max(s, axis=-1, keepdims=True)
        p = jnp.exp2(s - m).astype(BF16)
        pv = jnp.einsum("bqk,bkd->bqd", p, v_ext, preferred_element_type=F32)
        ms.append(m)
        pvs.append(pv[..., :LANES])
        dens.append(pv[..., LANES:])
    den = jnp.where(q_head0, dens[0], dens[1])
    out = jnp.where(q_head0, pvs[0], pvs[1]) / den
    lse = jnp.where(q_head0, ms[0], ms[1]) * LN2 + jnp.log(den)
    for r, c in enumerate(cols):
        o_ref[:, c] = out[r * nsub:(r + 1) * nsub].reshape(rows, LANES).astype(BF16)
        lse_ref[:, c] = lse[r * nsub:(r + 1) * nsub].reshape(rows, LANES)


def _swa(q, k, v, dilation):
    b, hp, sub, _ = q.shape
    res = min(dilation, SWA_TOKENS // SWA_WIN)
    rows = min(sub, SWA_TOKENS // res)
    per_step = rows // SWA_WIN
    cur = pl.BlockSpec((None, None, rows, res * LANES), lambda bi, h, r, n: (bi, h, n, r))
    prev = pl.BlockSpec((None, None, SWA_WIN, res * LANES),
                        lambda bi, h, r, n: (bi, h, jnp.maximum(n * per_step - 1, 0), r))
    return pl.pallas_call(
        functools.partial(_swa_kernel, rows=rows, res=res),
        grid=(b, hp, dilation // res, sub // rows),
        in_specs=[cur, prev, cur, prev, cur],
        out_specs=[cur, cur],
        out_shape=[jax.ShapeDtypeStruct(q.shape, BF16), jax.ShapeDtypeStruct(q.shape, F32)],
        scratch_shapes=[pltpu.VMEM((rows + SWA_WIN, res * LANES), BF16),
                        pltpu.VMEM((rows + SWA_WIN, res * LANES), BF16)],
        compiler_params=_params("parallel", "parallel", "parallel", "parallel"),
        name=f"swa_d{dilation}",
    )(q, k, k, v, v)


def _ab_out_kernel(h_ref, *rest):
    n_pat = len(DILATIONS)
    o_refs, l_refs = rest[:n_pat], rest[n_pat:2 * n_pat]
    b_ref, w_ref, out_ref, cat_ref, tok_ref = rest[2 * n_pat:]
    for hp in range(A_WIDTH // LANES):
        outs, lses = [], []
        for i, d in enumerate(DILATIONS):
            vals = []
            for kind, ref in enumerate((o_refs[i], l_refs[i])):
                if d == 1:
                    vals.append(ref[hp].astype(F32))
                    continue
                slot = (hp * n_pat + i) * 2 + kind
                for r in range(d):
                    tok_ref[slot, pl.ds(r, PROJ_ROWS // d, stride=d), :] = (
                        ref[hp, :, r * LANES:(r + 1) * LANES].astype(F32))
                vals.append(tok_ref[slot])
            outs.append(vals[0])
            lses.append(vals[1])
        m = functools.reduce(jnp.maximum, lses)
        es = [jnp.exp(l - m) for l in lses]
        mix = sum(e * o for e, o in zip(es, outs)) / sum(es)
        cat_ref[:, hp * LANES:(hp + 1) * LANES] = mix.astype(BF16)
    cat_ref[:, A_WIDTH:] = b_ref[...]
    out_ref[...] = h_ref[...] + jnp.dot(cat_ref[...], w_ref[...], preferred_element_type=F32)


def _ab_out(h, outs, lses, b_out, w_out):
    b, s, d = h.shape
    hp = A_WIDTH // LANES
    row = pl.BlockSpec((None, PROJ_ROWS, d), lambda bi, i: (bi, i, 0))
    heads = [_dilated_spec(hp, dil) for dil in DILATIONS]
    return pl.pallas_call(
        _ab_out_kernel,
        grid=(b, s // PROJ_ROWS),
        in_specs=[row] + heads + heads + [pl.BlockSpec((None, PROJ_ROWS, B_WIDTH), lambda bi, i: (bi, i, 0)),
                                          _resident((A_WIDTH + B_WIDTH, d))],
        out_specs=row,
        out_shape=jax.ShapeDtypeStruct((b, s, d), F32),
        scratch_shapes=[pltpu.VMEM((PROJ_ROWS, A_WIDTH + B_WIDTH), BF16),
                        pltpu.VMEM((hp * len(DILATIONS) * 2, PROJ_ROWS, LANES), F32)],
        compiler_params=_params("parallel", "parallel", fuse_operand=8, n_operands=9),
        name="ab_out_proj",
    )(h, *outs, *lses, b_out, w_out)


def _c_in_kernel(h_ref, g_ref, w_ref, q_ref, k_ref, v_ref):
    hn = _rms_norm(h_ref[...], g_ref[...]).astype(BF16)
    z = jnp.dot(hn, w_ref[...], preferred_element_type=F32)
    for hp in range(C_WIDTH // LANES):
        lo = hp * LANES
        q_ref[hp] = (z[:, lo:lo + LANES] * (SCALE * LOG2E)).astype(BF16)
        k_ref[hp] = z[:, C_WIDTH + lo:C_WIDTH + lo + LANES].astype(BF16)
        v_ref[hp] = z[:, 2 * C_WIDTH + lo:2 * C_WIDTH + lo + LANES].astype(BF16)


def _c_in(h, g, w_in):
    b, s, d = h.shape
    hp = C_WIDTH // LANES
    spec = pl.BlockSpec((None, hp, PROJ_ROWS, LANES), lambda bi, i: (bi, 0, i, 0))
    shape = jax.ShapeDtypeStruct((b, hp, s, LANES), BF16)
    return pl.pallas_call(
        _c_in_kernel,
        grid=(b, s // PROJ_ROWS),
        in_specs=[pl.BlockSpec((None, PROJ_ROWS, d), lambda bi, i: (bi, i, 0)),
                  _resident((1, d)), _resident((d, 3 * C_WIDTH))],
        out_specs=[spec, spec, spec],
        out_shape=[shape, shape, shape],
        compiler_params=_params("parallel", "parallel", fuse_operand=2, n_operands=3),
        name="c_in_proj",
    )(h, g.reshape(1, d), w_in)


def _moba_select(q_ref, k_ref, qaug_ref, st_ref, m_ref, nb):
    seq = q_ref.shape[0]
    kmean = jnp.concatenate(
        [jnp.mean(k_ref[j * C_BLOCK:(j + 1) * C_BLOCK].astype(F32), axis=0, keepdims=True) for j in range(nb)],
        axis=0).astype(BF16)
    lane = lax.broadcasted_iota(jnp.int32, (MOBA_ROWS, LANES), 1)
    blk = lax.broadcasted_iota(jnp.int32, (nb, MOBA_ROWS), 0)
    qpos = lax.broadcasted_iota(jnp.int32, (nb, MOBA_ROWS), 1)
    zeros = lambda n: jnp.zeros((n, MOBA_ROWS), F32)

    def chunk(t, carry):
        r0 = pl.multiple_of(t * MOBA_ROWS, MOBA_ROWS)
        q = q_ref[pl.ds(r0, MOBA_ROWS), :]
        own = (qpos + r0) // C_BLOCK
        past = blk < own
        for h in range(2):
            qm = jnp.where(lane < HEAD_DIM if h == 0 else lane >= HEAD_DIM, q, jnp.zeros_like(q))
            gate = lax.dot_general(kmean, qm, CONTRACT_LAST, preferred_element_type=F32)
            gate = jnp.where(past, gate, -jnp.inf)
            chosen = jnp.zeros(gate.shape, jnp.bool_)
            for _ in range(C_TOPK):
                best = jnp.max(gate, axis=0, keepdims=True)
                first = jnp.min(jnp.where(gate == best, blk, nb), axis=0, keepdims=True)
                hit = blk == first
                chosen = chosen | hit
                gate = jnp.where(hit, -jnp.inf, gate)
            unselected = 1.0 - ((chosen & past) | (blk == own)).astype(F32)
            if h == 0:
                extra_t = jnp.concatenate([zeros(HEAD_DIM), unselected, zeros(HEAD_DIM - nb)], axis=0)
            else:
                extra_t = jnp.concatenate([unselected, zeros(LANES - nb)], axis=0)
            qaug_ref[h, pl.ds(r0, MOBA_ROWS), :] = (qm.astype(F32) + extra_t.T).astype(BF16)
        return carry

    lax.fori_loop(0, seq // MOBA_ROWS, chunk, 0, unroll=True)
    st_ref[...] = jnp.zeros_like(st_ref)
    m_ref[...] = jnp.full(m_ref.shape, M_INIT, F32)


def _moba_kernel(q_ref, k_ref, v_ref, o_ref, qaug_ref, st_ref, m_ref):
    j = pl.program_id(2)
    seq = q_ref.shape[0]
    nb = seq // C_BLOCK

    @pl.when(j == 0)
    def _():
        _moba_select(q_ref, k_ref, qaug_ref, st_ref, m_ref, nb)

    j0 = pl.multiple_of(j * MOBA_ROWS, MOBA_ROWS)
    kj = k_ref[pl.ds(j0, MOBA_ROWS), :]
    vj = v_ref[pl.ds(j0, MOBA_ROWS), :]
    lane = lax.broadcasted_iota(jnp.int32, (MOBA_ROWS, LANES), 1)
    key_blk = j * (MOBA_ROWS // C_BLOCK) + lax.broadcasted_iota(jnp.int32, (MOBA_ROWS, LANES), 0) // C_BLOCK
    in_head = [lane < HEAD_DIM, lane >= HEAD_DIM]
    flag_lane = [HEAD_DIM + key_blk, key_blk]
    k_aug = [jnp.where(in_head[h], kj, jnp.where(lane == flag_lane[h], SCORE_OFF, 0.0).astype(BF16))
             for h in range(2)]
    v_aug = [jnp.where(in_head[h], vj, jnp.ones_like(vj)) for h in range(2)]

    def update(h, rows, causal):
        qa = qaug_ref[h, pl.ds(rows, MOBA_ROWS), :]
        s = lax.dot_general(qa, k_aug[h], CONTRACT_LAST, preferred_element_type=F32)
        if causal:
            qi = lax.broadcasted_iota(jnp.int32, s.shape, 0)
            ki = lax.broadcasted_iota(jnp.int32, s.shape, 1)
            s = jnp.where(ki <= qi, s, SCORE_OFF)
        m_old = m_ref[h, pl.ds(rows, MOBA_ROWS), :]
        m_new = jnp.maximum(m_old, jnp.max(s, axis=-1, keepdims=True))
        p = jnp.exp2(s - jnp.concatenate([m_new] * (MOBA_ROWS // LANES), axis=1))
        pv = jnp.dot(p.astype(BF16), v_aug[h], preferred_element_type=F32)
        st = jnp.exp2(m_old - m_new) * st_ref[h, pl.ds(rows, MOBA_ROWS), :] + pv
        return st, m_new

    done = []
    for h in range(2):
        st, _ = update(h, j0, True)
        done.append(st / pltpu.roll(st, HEAD_DIM, 1))
    o_ref[pl.ds(j0, MOBA_ROWS), :] = jnp.where(in_head[0], done[0], done[1]).astype(BF16)

    def chunk(t, carry):
        rows = pl.multiple_of(t * MOBA_ROWS, MOBA_ROWS)
        for h in range(2):
            st, m_new = update(h, rows, False)
            st_ref[h, pl.ds(rows, MOBA_ROWS), :] = st
            m_ref[h, pl.ds(rows, MOBA_ROWS), :] = m_new
        return carry

    first = j + 1
    n_later = seq // MOBA_ROWS - first
    lax.fori_loop(0, n_later // 2, lambda i, c: chunk(first + 2 * i + 1, chunk(first + 2 * i, c)), 0)

    @pl.when(n_later % 2 == 1)
    def _():
        chunk(seq // MOBA_ROWS - 1, 0)


def _moba(q, k, v):
    b, hp, s, _ = q.shape
    whole = pl.BlockSpec((None, None, s, LANES), lambda bi, h, j: (bi, h, 0, 0))
    return pl.pallas_call(
        _moba_kernel,
        grid=(b, hp, s // MOBA_ROWS),
        in_specs=[whole, whole, whole],
        out_specs=whole,
        out_shape=jax.ShapeDtypeStruct((b, hp, s, LANES), BF16),
        scratch_shapes=[pltpu.VMEM((2, s, LANES), BF16),
                        pltpu.VMEM((2, s, LANES), F32),
                        pltpu.VMEM((2, s, LANES), F32)],
        compiler_params=_params("parallel", "parallel", "arbitrary"),
        name="moba",
    )(q, k, v)


def _c_out_kernel(h_ref, o_ref, w_ref, out_ref, cat_ref):
    for hp in range(C_WIDTH // LANES):
        cat_ref[:, hp * LANES:(hp + 1) * LANES] = o_ref[hp]
    out_ref[...] = h_ref[...] + jnp.dot(cat_ref[...], w_ref[...], preferred_element_type=F32)


def _c_out(h, o, w_out):
    b, s, d = h.shape
    hp = C_WIDTH // LANES
    row = pl.BlockSpec((None, PROJ_ROWS, d), lambda bi, i: (bi, i, 0))
    return pl.pallas_call(
        _c_out_kernel,
        grid=(b, s // PROJ_ROWS),
        in_specs=[row, pl.BlockSpec((None, hp, PROJ_ROWS, LANES), lambda bi, i: (bi, 0, i, 0)),
                  _resident((C_WIDTH, d))],
        out_specs=row,
        out_shape=jax.ShapeDtypeStruct((b, s, d), F32),
        scratch_shapes=[pltpu.VMEM((PROJ_ROWS, C_WIDTH), BF16)],
        compiler_params=_params("parallel", "parallel", fuse_operand=2, n_operands=3),
        name="c_out_proj",
    )(h, o, w_out)


def kernel(x, ffn1_norm, ffn1_w_gate, ffn1_w_up, ffn1_w_down, mix_norm, ffn2_norm, ffn2_w_gate, ffn2_w_up,
           ffn2_w_down, ab_w_in, ab_v_norm, ab_w_spatial, ab_b_spatial, ab_w_out, c_w_in, c_w_out, final_norm):
    b, s, d = x.shape
    assert DILATIONS == (1, 4, 16)
    assert d == D_MODEL and s % SWA_TOKENS == 0 and s % MOBA_ROWS == 0 and (b * s) % FFN_ROWS == 0
    assert s // C_BLOCK <= HEAD_DIM
    bf = lambda w: w.astype(BF16)

    def ffn(h, layer, norm, wg, wu, wd, final_g=None):
        return _ffn(h.reshape(b * s, d), norm[layer], wg, wu, wd, layer, final_g).reshape(b, s, d)

    h = x
    h = ffn(h, 0, ffn1_norm, ffn1_w_gate, ffn1_w_up, ffn1_w_down)
    qkvs, b_out = _ab_in(h, mix_norm[0], bf(ab_w_in[0]), ab_v_norm[0], ab_w_spatial[0], ab_b_spatial[0])
    outs, lses = zip(*[_swa(*qkv, dil) for qkv, dil in zip(qkvs, DILATIONS)])
    h = _ab_out(h, outs, lses, b_out, bf(ab_w_out[0]))
    h = ffn(h, 0, ffn2_norm, ffn2_w_gate, ffn2_w_up, ffn2_w_down)
    h = ffn(h, 1, ffn1_norm, ffn1_w_gate, ffn1_w_up, ffn1_w_down)
    qc, kc, vc = _c_in(h, mix_norm[1], bf(c_w_in[0]))
    h = _c_out(h, _moba(qc, kc, vc), bf(c_w_out[0]))
    h = ffn(h, 1, ffn2_norm, ffn2_w_gate, ffn2_w_up, ffn2_w_down, final_g=final_norm)
    return h
```
